```python
import math
import jax, jax.numpy as jnp
from jax import lax
import numpy as np

D_MODEL = 1024
BATCH = 16
SEQ = 2048
DEPTH = 2
DEC_BATCH = 128
DEC_SEQ = 4
PAST_LEN = 16384
PAGE_SIZE = 128

N_A_LAYERS = DEPTH // 2
N_B_LAYERS = DEPTH - N_A_LAYERS
GROUP_SIZE = 16
N_GROUPS = D_MODEL // GROUP_SIZE
STATE_N = 64
HEAD_DIM = 64
N_HEADS = D_MODEL // HEAD_DIM
N_KV_HEADS = 4
Q_PER_KV = N_HEADS // N_KV_HEADS
WINDOW = 128
D_FF = 4 * D_MODEL
EPS = 1e-6
DT_MIN = 1e-3
DT_MAX = 1e-1
F32 = jnp.float32

kernel_name = "yoco_s5_swa_sink_decoder_step"


def rmsnorm(x, g):
    xf = x.astype(F32)
    y = xf * lax.rsqrt(jnp.mean(xf * xf, axis=-1, keepdims=True) + EPS)
    return (y * g.astype(F32)).astype(x.dtype)


def alibi_slopes():
    return 2.0 ** (-8.0 * jnp.arange(1, N_HEADS + 1, dtype=F32) / N_HEADS)


def s5_discretize(a_re, a_im, log_dt, b_re, b_im):
    dt = jnp.exp(log_dt.astype(F32))[:, None]
    a_re = a_re.astype(F32)
    a_im = a_im.astype(F32)
    mag = jnp.exp(dt * a_re)
    ab_re = mag * jnp.cos(dt * a_im)
    ab_im = mag * jnp.sin(dt * a_im)
    den = a_re * a_re + a_im * a_im
    nr = ab_re - 1.0
    ni = ab_im
    f_re = (nr * a_re + ni * a_im) / den
    f_im = (ni * a_re - nr * a_im) / den
    b_re = b_re.astype(F32)
    b_im = b_im.astype(F32)
    bb_re = f_re[..., None] * b_re - f_im[..., None] * b_im
    bb_im = f_re[..., None] * b_im + f_im[..., None] * b_re
    return ab_re, ab_im, bb_re, bb_im


def _ssm_combine(e1, e2):
    a1r, a1i, b1r, b1i = e1
    a2r, a2i, b2r, b2i = e2
    ar = a2r * a1r - a2i * a1i
    ai = a2r * a1i + a2i * a1r
    br = a2r * b1r - a2i * b1i + b2r
    bi = a2r * b1i + a2i * b1r + b2i
    return (ar, ai, br, bi)


def s5_mixer(u, h0_re, h0_im, a_re, a_im, log_dt, b_re, b_im, c_re, c_im, d_skip, w_glu):
    bsz, length, _ = u.shape
    uf = u.astype(F32).reshape(bsz, length, N_GROUPS, GROUP_SIZE)
    ab_re, ab_im, bb_re, bb_im = s5_discretize(a_re, a_im, log_dt, b_re, b_im)
    bu_re = jnp.einsum('blgc,gnc->blgn', uf, bb_re)
    bu_im = jnp.einsum('blgc,gnc->blgn', uf, bb_im)
    h0r = h0_re.astype(F32)
    h0i = h0_im.astype(F32)
    bu_re = bu_re.at[:, 0].add(ab_re * h0r - ab_im * h0i)
    bu_im = bu_im.at[:, 0].add(ab_re * h0i + ab_im * h0r)
    a_seq_re = jnp.broadcast_to(ab_re, (1, length, N_GROUPS, STATE_N))
    a_seq_im = jnp.broadcast_to(ab_im, (1, length, N_GROUPS, STATE_N))
    _, _, xr, xi = lax.associative_scan(_ssm_combine, (a_seq_re, a_seq_im, bu_re, bu_im), axis=1)
    y = (jnp.einsum('blgn,gcn->blgc', xr, c_re.astype(F32))
         - jnp.einsum('blgn,gcn->blgc', xi, c_im.astype(F32)))
    y = y.reshape(bsz, length, D_MODEL) + d_skip.astype(F32) * u.astype(F32)
    z = jax.nn.gelu(y).astype(u.dtype) @ w_glu
    gate_in, gate = jnp.split(z, 2, axis=-1)
    out = gate_in * jax.nn.sigmoid(gate)
    return out, xr[:, -1].astype(h0_re.dtype), xi[:, -1].astype(h0_im.dtype)


def sqrelu_mlp(x, w_up, w_down):
    return jnp.square(jax.nn.relu(x @ w_up)) @ w_down


def shared_kv(h, kv_norm, w_kv, k_norm):
    bsz, length, _ = h.shape
    kv = rmsnorm(h, kv_norm) @ w_kv
    k, v = jnp.split(kv, 2, axis=-1)
    k = rmsnorm(k.reshape(bsz, length, N_KV_HEADS, HEAD_DIM), k_norm)
    v = v.reshape(bsz, length, N_KV_HEADS, HEAD_DIM)
    return k, v


def gqa_sink_attention(q, k, v, dist, valid, sinks):
    qg = q.reshape(q.shape[:-2] + (N_KV_HEADS, Q_PER_KV, HEAD_DIM))
    s = jnp.einsum('...qhgd,...khd->...hgqk', qg, k, preferred_element_type=F32) * (HEAD_DIM ** -0.5)
    slopes = alibi_slopes().reshape(N_KV_HEADS, Q_PER_KV, 1, 1)
    s = s - slopes * dist.astype(F32)
    s = jnp.where(valid, s, -jnp.inf)
    sink = jnp.broadcast_to(sinks.astype(F32).reshape(N_KV_HEADS, Q_PER_KV, 1, 1), s.shape[:-1] + (1,))
    p = jax.nn.softmax(jnp.concatenate([s, sink], axis=-1), axis=-1)[..., :-1]
    o = jnp.einsum('...hgqk,...khd->...qhgd', p.astype(v.dtype), v)
    return o.reshape(q.shape)


def banded_prompt_attention(q, k, v, sinks):
    bsz, length = q.shape[:2]
    nb = length // WINDOW
    qb = q.reshape(bsz, nb, WINDOW, N_HEADS, HEAD_DIM)
    kb = k.reshape(bsz, nb, WINDOW, N_KV_HEADS, HEAD_DIM)
    vb = v.reshape(bsz, nb, WINDOW, N_KV_HEADS, HEAD_DIM)
    pad = ((0, 0), (1, 0), (0, 0), (0, 0), (0, 0))
    kk = jnp.concatenate([jnp.pad(kb, pad)[:, :-1], kb], axis=2)
    vv = jnp.concatenate([jnp.pad(vb, pad)[:, :-1], vb], axis=2)
    qi = jnp.arange(WINDOW)[:, None]
    kj = jnp.arange(2 * WINDOW)[None, :]
    dist = qi - kj + WINDOW
    blk = jnp.arange(nb)[:, None, None]
    valid = (dist >= 0) & (dist < WINDOW) & ((blk > 0) | (kj >= WINDOW))
    valid = valid[:, None, None]
    o = gqa_sink_attention(qb, kk, vv, dist, valid, sinks)
    return o.reshape(bsz, length, N_HEADS, HEAD_DIM)


def window_cache_attention(q, k_all, v_all, sinks):
    t_new = q.shape[1]
    qi = jnp.arange(t_new)[:, None]
    kj = jnp.arange(k_all.shape[1])[None, :]
    dist = qi + WINDOW - kj
    valid = (dist >= 0) & (dist < WINDOW)
    return gqa_sink_attention(q, k_all, v_all, dist, valid, sinks)


def _trunk(x, h0_re, h0_im, cache_k, cache_v, norm_mix, norm_ffn, ssm_a_re, ssm_a_im, ssm_log_dt,
           ssm_b_re, ssm_b_im, ssm_c_re, ssm_c_im, ssm_d, ssm_w_glu, kv_norm, w_kv, k_norm,
           w_q, q_norm, attn_sinks, w_o, w_up, w_down):
    is_prompt = cache_k is None
    bsz, length, _ = x.shape
    new_re, new_im = [], []
    k_all = v_all = k_buf = v_buf = None
    for layer in range(DEPTH):
        hn = rmsnorm(x, norm_mix[layer])
        if layer < N_A_LAYERS:
            y, hr, hi = s5_mixer(hn, h0_re[layer], h0_im[layer], ssm_a_re[layer], ssm_a_im[layer],
                                 ssm_log_dt[layer], ssm_b_re[layer], ssm_b_im[layer], ssm_c_re[layer],
                                 ssm_c_im[layer], ssm_d[layer], ssm_w_glu[layer])
            new_re.append(hr)
            new_im.append(hi)
        else:
            b = layer - N_A_LAYERS
            q = rmsnorm((hn @ w_q[b]).reshape(bsz, length, N_HEADS, HEAD_DIM), q_norm[b])
            if is_prompt:
                o = banded_prompt_attention(q, k_all, v_all, attn_sinks[b])
            else:
                o = window_cache_attention(q, k_all, v_all, attn_sinks[b])
            y = o.reshape(bsz, length, N_HEADS * HEAD_DIM) @ w_o[b]
        x = x + y
        x = x + sqrelu_mlp(rmsnorm(x, norm_ffn[layer]), w_up[layer], w_down[layer])
        if layer == N_A_LAYERS - 1:
            k_sh, v_sh = shared_kv(x, kv_norm, w_kv, k_norm)
            if is_prompt:
                k_all, v_all = k_sh, v_sh
            else:
                k_all = jnp.concatenate([cache_k, k_sh.astype(cache_k.dtype)], axis=1)
                v_all = jnp.concatenate([cache_v, v_sh.astype(cache_v.dtype)], axis=1)
            k_buf = k_all[:, -WINDOW:]
            v_buf = v_all[:, -WINDOW:]
    return x, jnp.stack(new_re), jnp.stack(new_im), k_buf, v_buf


def setup_inputs(seed: int = 0) -> dict:
    key = jax.random.key(seed)
    ks = jax.random.split(key, 32)

    def nrm(k, shape, scale):
        return jax.random.normal(k, shape, F32) * scale

    n_idx = jnp.arange(STATE_N, dtype=F32)
    a_shape = (N_A_LAYERS, N_GROUPS, STATE_N)
    return {
        "x_prompt": nrm(ks[0], (BATCH, SEQ, D_MODEL), 1.0),
        "x_sample": nrm(ks[1], (DEC_BATCH, DEC_SEQ, D_MODEL), 1.0),
        "state_ssm_re": nrm(ks[2], (N_A_LAYERS, DEC_BATCH, N_GROUPS, STATE_N), 0.5),
        "state_ssm_im": nrm(ks[3], (N_A_LAYERS, DEC_BATCH, N_GROUPS, STATE_N), 0.5),
        "cache_k": nrm(ks[4], (DEC_BATCH, WINDOW, N_KV_HEADS, HEAD_DIM), 1.0),
        "cache_v": nrm(ks[5], (DEC_BATCH, WINDOW, N_KV_HEADS, HEAD_DIM), 1.0),
        "norm_mix": 1.0 + nrm(ks[6], (DEPTH, D_MODEL), 0.02),
        "norm_ffn": 1.0 + nrm(ks[7], (DEPTH, D_MODEL), 0.02),
        "ssm_a_re": -0.5 + nrm(ks[8], a_shape, 0.01),
        "ssm_a_im": math.pi * n_idx + nrm(ks[9], a_shape, 0.01),
        "ssm_log_dt": jax.random.uniform(ks[10], (N_A_LAYERS, N_GROUPS), F32,
                                         minval=math.log(DT_MIN), maxval=math.log(DT_MAX)),
        "ssm_b_re": nrm(ks[11], (N_A_LAYERS, N_GROUPS, STATE_N, GROUP_SIZE), (2 * GROUP_SIZE) ** -0.5),
        "ssm_b_im": nrm(ks[12], (N_A_LAYERS, N_GROUPS, STATE_N, GROUP_SIZE), (2 * GROUP_SIZE) ** -0.5),
        "ssm_c_re": nrm(ks[13], (N_A_LAYERS, N_GROUPS, GROUP_SIZE, STATE_N), (2 * STATE_N) ** -0.5),
        "ssm_c_im": nrm(ks[14], (N_A_LAYERS, N_GROUPS, GROUP_SIZE, STATE_N), (2 * STATE_N) ** -0.5),
        "ssm_d": nrm(ks[15], (N_A_LAYERS, D_MODEL), 0.5),
        "ssm_w_glu": nrm(ks[16], (N_A_LAYERS, D_MODEL, 2 * D_MODEL), D_MODEL ** -0.5),
        "kv_norm": 1.0 + nrm(ks[17], (D_MODEL,), 0.02),
        "w_kv": nrm(ks[18], (D_MODEL, 2 * N_KV_HEADS * HEAD_DIM), D_MODEL ** -0.5),
        "k_norm": 1.0 + nrm(ks[19], (HEAD_DIM,), 0.02),
        "w_q": nrm(ks[20], (N_B_LAYERS, D_MODEL, N_HEADS * HEAD_DIM), D_MODEL ** -0.5),
        "q_norm": 1.0 + nrm(ks[21], (N_B_LAYERS, HEAD_DIM), 0.02),
        "attn_sinks": nrm(ks[22], (N_B_LAYERS, N_HEADS), 0.5),
        "w_o": nrm(ks[23], (N_B_LAYERS, N_HEADS * HEAD_DIM, D_MODEL), (N_HEADS * HEAD_DIM) ** -0.5),
        "w_up": nrm(ks[24], (DEPTH, D_MODEL, D_FF), D_MODEL ** -0.5),
        "w_down": nrm(ks[25], (DEPTH, D_FF, D_MODEL), D_FF ** -0.5),
    }


def reference(x_prompt, x_sample, state_ssm_re, state_ssm_im, cache_k, cache_v, norm_mix, norm_ffn,
              ssm_a_re, ssm_a_im, ssm_log_dt, ssm_b_re, ssm_b_im, ssm_c_re, ssm_c_im, ssm_d, ssm_w_glu,
              kv_norm, w_kv, k_norm, w_q, q_norm, attn_sinks, w_o, w_up, w_down):
    weights = (norm_mix, norm_ffn, ssm_a_re, ssm_a_im, ssm_log_dt, ssm_b_re, ssm_b_im, ssm_c_re,
               ssm_c_im, ssm_d, ssm_w_glu, kv_norm, w_kv, k_norm, w_q, q_norm, attn_sinks, w_o,
               w_up, w_down)
    zeros_state = jnp.zeros((N_A_LAYERS, x_prompt.shape[0], N_GROUPS, STATE_N), state_ssm_re.dtype)
    y_prompt, re_p, im_p, k_p, v_p = _trunk(x_prompt, zeros_state, zeros_state, None, None, *weights)
    y_sample, re_s, im_s, k_s, v_s = _trunk(x_sample, state_ssm_re, state_ssm_im, cache_k, cache_v, *weights)
    return (y_prompt, y_sample, re_p, im_p, k_p, v_p, re_s, im_s, k_s, v_s)
```

```python
import functools
import math

import jax
import jax.numpy as jnp
import numpy as np
from jax import lax
from jax.experimental import pallas as pl
from jax.experimental.pallas import tpu as pltpu

F32 = jnp.float32
BF16 = jnp.bfloat16

D_MODEL = 1024
GROUP_SIZE = 16
N_GROUPS = D_MODEL // GROUP_SIZE
STATE_N = 64
D_STATE = N_GROUPS * STATE_N
HEAD_DIM = 64
N_HEADS = D_MODEL // HEAD_DIM
N_KV_HEADS = 4
Q_PER_KV = N_HEADS // N_KV_HEADS
D_KV = N_KV_HEADS * HEAD_DIM
WINDOW = 128
D_FF = 4 * D_MODEL
EPS = 1e-6
MASKED = -1e30

LANES = 128
GROUPS_PER_BLOCK = LANES // GROUP_SIZE
N_LANE_BLOCKS = D_MODEL // LANES
STATES_PER_BLOCK = GROUPS_PER_BLOCK * STATE_N
SUB_BATCH = 16
SCAN_LANES = 256
VMEM_LIMIT = 56 * 1024 * 1024


def _resident(a):
    return pl.BlockSpec(a.shape, lambda *_: (0,) * a.ndim, pipeline_mode=pl.Buffered(1))


def _rms(x, g):
    return x * lax.rsqrt(jnp.mean(x * x, axis=-1, keepdims=True) + EPS) * g


def _head_mean_sq(x, avg):
    sq = x * x
    hi = sq.astype(BF16)
    lo = (sq - hi.astype(F32)).astype(BF16)
    return (jnp.dot(hi, avg, preferred_element_type=F32)
            + jnp.dot(lo, avg, preferred_element_type=F32))


def _discretize_kernel(a_re_ref, a_im_ref, log_dt_ref, b_re_ref, b_im_ref,
                       ab_re_ref, ab_im_ref, bb_re_ref, bb_im_ref):
    dt = jnp.exp(log_dt_ref[...])
    a_re = a_re_ref[...]
    a_im = a_im_ref[...]
    mag = jnp.exp(dt * a_re)
    ab_re = mag * jnp.cos(dt * a_im)
    ab_im = mag * jnp.sin(dt * a_im)
    den = a_re * a_re + a_im * a_im
    nr = ab_re - 1.0
    ni = ab_im
    f_re = (nr * a_re + ni * a_im) / den
    f_im = (ni * a_re - nr * a_im) / den
    ab_re_ref[...] = ab_re
    ab_im_ref[...] = ab_im
    b_re = b_re_ref[...]
    b_im = b_im_ref[...]
    fr = f_re[:, None, :]
    fi = f_im[:, None, :]
    bb_re_ref[...] = fr * b_re - fi * b_im
    bb_im_ref[...] = fr * b_im + fi * b_re


def _s5_discretize(a_re, a_im, log_dt, b_re, b_im):
    g, n = a_re.shape
    c = b_re.shape[-1]
    out = pl.pallas_call(
        _discretize_kernel,
        out_shape=(jax.ShapeDtypeStruct((g, n), F32), jax.ShapeDtypeStruct((g, n), F32),
                   jax.ShapeDtypeStruct((g, c, n), F32), jax.ShapeDtypeStruct((g, c, n), F32)),
        name="s5_discretize",
    )(a_re, a_im, log_dt.reshape(g, 1), jnp.swapaxes(b_re, 1, 2), jnp.swapaxes(b_im, 1, 2))
    return out


def _block_diag_weights(bb_re, bb_im, c_re, c_im):
    nb, gb = N_LANE_BLOCKS, GROUPS_PER_BLOCK
    eye = jnp.eye(gb, dtype=F32)
    bb = jnp.stack([bb_re, bb_im], axis=0).reshape(2, nb, gb, GROUP_SIZE, STATE_N)
    b_blk = jnp.einsum("rjgcn,gh->jgcrhn", bb, eye).reshape(nb, LANES, 2 * STATES_PER_BLOCK)
    cc = jnp.stack([c_re, -c_im], axis=0).reshape(2, nb, gb, GROUP_SIZE, STATE_N)
    c_blk = jnp.einsum("rjgcn,gh->jrgnhc", cc, eye).reshape(nb, 2 * STATES_PER_BLOCK, LANES)
    return b_blk.astype(BF16), c_blk.astype(BF16)


def _s5_layer_kernel(x_ref, h0r_ref, h0i_ref, g_ref, d_ref, abr_ref, abi_ref, bblk_ref, cblk_ref,
                     wglu_ref, o_ref, hr_ref, hi_ref, bur, bui, hn_sc, y_sc):
    tc, bb, d = x_ref.shape
    rows = tc * bb
    spb = STATES_PER_BLOCK

    @pl.when(pl.program_id(1) == 0)
    def _():
        hr_ref[...] = h0r_ref[...]
        hi_ref[...] = h0i_ref[...]

    x = x_ref[...].reshape(rows, d)
    hn = _rms(x, g_ref[...])
    hn_sc[...] = hn
    hnb = hn.astype(BF16)
    for j in range(N_LANE_BLOCKS):
        bu = jnp.dot(hnb[:, j * LANES:(j + 1) * LANES], bblk_ref[j], preferred_element_type=F32)
        bur[:, j * spb:(j + 1) * spb] = bu[:, :spb]
        bui[:, j * spb:(j + 1) * spb] = bu[:, spb:]

    for s in range(D_STATE // SCAN_LANES):
        cs = slice(s * SCAN_LANES, (s + 1) * SCAN_LANES)
        ar = jnp.broadcast_to(abr_ref[:, cs], (SUB_BATCH, SCAN_LANES))
        ai = jnp.broadcast_to(abi_ref[:, cs], (SUB_BATCH, SCAN_LANES))

        def sub_batch(sb, carry, cs=cs, ar=ar, ai=ai):
            r0 = pl.multiple_of(sb * SUB_BATCH, SUB_BATCH)

            def step(t, h):
                h_re, h_im = h
                row = pl.multiple_of(t * bb + r0, SUB_BATCH)
                n_re = ar * h_re - ai * h_im + bur[pl.ds(row, SUB_BATCH), cs]
                n_im = ar * h_im + ai * h_re + bui[pl.ds(row, SUB_BATCH), cs]
                bur[pl.ds(row, SUB_BATCH), cs] = n_re
                bui[pl.ds(row, SUB_BATCH), cs] = n_im
                return n_re, n_im

            h_re, h_im = lax.fori_loop(
                0, tc, step, (hr_ref[pl.ds(r0, SUB_BATCH), cs], hi_ref[pl.ds(r0, SUB_BATCH), cs]),
                unroll=min(tc, 4))
            hr_ref[pl.ds(r0, SUB_BATCH), cs] = h_re
            hi_ref[pl.ds(r0, SUB_BATCH), cs] = h_im
            return carry

        lax.fori_loop(0, bb // SUB_BATCH, sub_batch, 0)

    for j in range(N_LANE_BLOCKS):
        xr = bur[:, j * spb:(j + 1) * spb].astype(BF16)
        xi = bui[:, j * spb:(j + 1) * spb].astype(BF16)
        y_sc[:, j * LANES:(j + 1) * LANES] = (
            jnp.dot(xr, cblk_ref[j, :spb, :], preferred_element_type=F32)
            + jnp.dot(xi, cblk_ref[j, spb:, :], preferred_element_type=F32))

    y = y_sc[...] + d_ref[...] * hn_sc[...]
    z = jnp.dot(jax.nn.gelu(y).astype(BF16), wglu_ref[...], preferred_element_type=F32)
    out = z[:, :d] * jax.nn.sigmoid(z[:, d:])
    o_ref[...] = (x_ref[...].reshape(rows, d) + out).reshape(tc, bb, d)


def _s5_layer(x_tb, h0r, h0i, g_mix, d_skip, ab_re, ab_im, b_blk, c_blk, w_glu, *, tc, bb):
    length, bsz, d = x_tb.shape
    rows = tc * bb
    weights = (g_mix, d_skip, ab_re, ab_im, b_blk, c_blk, w_glu)
    return pl.pallas_call(
        _s5_layer_kernel,
        grid=(bsz // bb, length // tc),
        in_specs=[
            pl.BlockSpec((tc, bb, d), lambda b, i: (i, b, 0)),
            pl.BlockSpec((bb, D_STATE), lambda b, i: (b, 0)),
            pl.BlockSpec((bb, D_STATE), lambda b, i: (b, 0)),
        ] + [_resident(w) for w in weights],
        out_specs=(
            pl.BlockSpec((tc, bb, d), lambda b, i: (i, b, 0)),
            pl.BlockSpec((bb, D_STATE), lambda b, i: (b, 0)),
            pl.BlockSpec((bb, D_STATE), lambda b, i: (b, 0)),
        ),
        out_shape=(jax.ShapeDtypeStruct((length, bsz, d), F32),
                   jax.ShapeDtypeStruct((bsz, D_STATE), F32),
                   jax.ShapeDtypeStruct((bsz, D_STATE), F32)),
        scratch_shapes=[pltpu.VMEM((rows, D_STATE), F32), pltpu.VMEM((rows, D_STATE), F32),
                        pltpu.VMEM((rows, d), F32), pltpu.VMEM((rows, d), F32)],
        compiler_params=pltpu.CompilerParams(
            dimension_semantics=("parallel", "arbitrary"), vmem_limit_bytes=VMEM_LIMIT),
        name="s5_layer",
    )(x_tb, h0r, h0i, *weights)


def _mlp(x, g, w_up_ref, w_down_ref):
    hn = _rms(x, g).astype(BF16)
    acc = x
    n_chunks = w_up_ref.shape[1] // D_MODEL
    for c in range(n_chunks):
        cs = slice(c * D_MODEL, (c + 1) * D_MODEL)
        u = jnp.dot(hn, w_up_ref[:, cs], preferred_element_type=F32)
        u = jnp.square(jnp.maximum(u, 0.0)).astype(BF16)
        acc = acc + jnp.dot(u, w_down_ref[cs, :], preferred_element_type=F32)
    return acc


def _mlp_proj_kernel(x_ref, gffn_ref, wup_ref, wdown_ref, gkv_ref, wkv_ref, gk_ref, gq1_ref, wq_ref,
                     gq_ref, avg_ref, x2_ref, q_ref, kd_ref, vd_ref, k_ref, v_ref):
    x2 = _mlp(x_ref[...], gffn_ref[...], wup_ref, wdown_ref)
    x2_ref[...] = x2
    normed = x2 * lax.rsqrt(jnp.mean(x2 * x2, axis=-1, keepdims=True) + EPS)
    avg = avg_ref[...]

    kv = jnp.dot((normed * gkv_ref[...]).astype(BF16), wkv_ref[...], preferred_element_type=F32)
    k = kv[:, :D_KV]
    k_ref[...] = k * lax.rsqrt(_head_mean_sq(k, avg) + EPS) * gk_ref[:, :D_KV]
    v_ref[...] = kv[:, D_KV:2 * D_KV]
    for c in range(2):
        cs = slice(2 * D_KV + c * D_KV, 2 * D_KV + (c + 1) * D_KV)
        kd = kv[:, cs]
        kd_ref[:, c * D_KV:(c + 1) * D_KV] = (
            kd * lax.rsqrt(_head_mean_sq(kd, avg) + EPS) * gk_ref[:, :D_KV]).astype(BF16)
    vd_ref[...] = kv[:, 4 * D_KV:].astype(BF16)

    hq = (normed * gq1_ref[...]).astype(BF16)
    for c in range(D_MODEL // D_KV):
        cs = slice(c * D_KV, (c + 1) * D_KV)
        q = jnp.dot(hq, wq_ref[:, cs], preferred_element_type=F32)
        q = q * lax.rsqrt(_head_mean_sq(q, avg) + EPS) * gq_ref[:, :D_KV]
        q_ref[:, cs] = (q * (HEAD_DIM ** -0.5)).astype(BF16)


def _mlp_proj(x2d, n_col_blocks, tm, g_ffn, w_up, w_down, g_kv, w_kv_ext, g_k, g_q1, w_q, g_q, avg):
    rows = x2d.shape[0]
    nt = rows // tm
    total = rows * n_col_blocks
    row_out = lambda width: pl.BlockSpec((tm, width), lambda b, i: (b * nt + i, 0))
    weights = (g_ffn, w_up, w_down, g_kv, w_kv_ext, g_k, g_q1, w_q, g_q, avg)
    return pl.pallas_call(
        _mlp_proj_kernel,
        grid=(n_col_blocks, nt),
        in_specs=[pl.BlockSpec((tm, D_MODEL), lambda b, i: (i, b))] + [_resident(w) for w in weights],
        out_specs=(row_out(D_MODEL), row_out(D_MODEL), row_out(2 * D_KV), row_out(2 * D_KV),
                   row_out(D_KV), row_out(D_KV)),
        out_shape=(jax.ShapeDtypeStruct((total, D_MODEL), F32),
                   jax.ShapeDtypeStruct((total, D_MODEL), BF16),
                   jax.ShapeDtypeStruct((total, 2 * D_KV), BF16),
                   jax.ShapeDtypeStruct((total, 2 * D_KV), BF16),
                   jax.ShapeDtypeStruct((total, D_KV), F32),
                   jax.ShapeDtypeStruct((total, D_KV), F32)),
        compiler_params=pltpu.CompilerParams(
            dimension_semantics=("parallel", "parallel"), vmem_limit_bytes=VMEM_LIMIT),
        name="mlp_proj",
    )(x2d, *weights)


def _out_mlp_kernel(o_ref, x_ref, wo_ref, gffn_ref, wup_ref, wdown_ref, y_ref):
    x3 = x_ref[...] + jnp.dot(o_ref[...], wo_ref[...], preferred_element_type=F32)
    y_ref[...] = _mlp(x3, gffn_ref[...], wup_ref, wdown_ref)


def _prompt_attn_kernel(q_ref, kd_ref, vd_ref, kdp_ref, vdp_ref, x_ref, bias_ref, sink_ref, lmask_ref,
                        wo_ref, gffn_ref, wup_ref, wdown_ref, y_ref, o_sc):
    tq = q_ref.shape[0]
    first = (pl.program_id(1) == 0).astype(jnp.int32)
    m_lo = lmask_ref[0:1, :]
    m_hi = lmask_ref[1:2, :]
    ones_lo = jnp.broadcast_to(m_lo, (2 * WINDOW, LANES))
    ones_hi = jnp.broadcast_to(m_hi, (2 * WINDOW, LANES))
    nt = (((1,), (1,)), ((), ()))

    for r in range(tq // WINDOW):
        rows = slice(r * WINDOW, (r + 1) * WINDOW)
        sel = first if r == 0 else 0
        for kh in range(N_KV_HEADS):
            ls = slice(kh * LANES, (kh + 1) * LANES)
            if r == 0:
                kwin = jnp.concatenate([kdp_ref[:, ls], kd_ref[0:WINDOW, ls]], axis=0)
                vwin = jnp.concatenate([vdp_ref[:, ls], vd_ref[0:WINDOW, ls]], axis=0)
            else:
                kwin = kd_ref[(r - 1) * WINDOW:(r + 1) * WINDOW, ls]
                vwin = vd_ref[(r - 1) * WINDOW:(r + 1) * WINDOW, ls]
            k_lo = kwin * m_lo
            k_hi = kwin * m_hi
            v_lo = jnp.concatenate([vwin * m_lo, ones_lo], axis=1)
            v_hi = jnp.concatenate([vwin * m_hi, ones_hi], axis=1)
            for pair in range(Q_PER_KV // 2):
                blk = kh * (Q_PER_KV // 2) + pair
                h0 = 2 * blk
                qp = q_ref[rows, blk * LANES:(blk + 1) * LANES]
                s0 = lax.dot_general(qp, k_lo, nt, preferred_element_type=F32) + bias_ref[sel, h0]
                s1 = lax.dot_general(qp, k_hi, nt, preferred_element_type=F32) + bias_ref[sel, h0 + 1]
                m0 = jnp.maximum(jnp.max(s0, axis=-1, keepdims=True), sink_ref[:, h0:h0 + 1])
                m1 = jnp.maximum(jnp.max(s1, axis=-1, keepdims=True), sink_ref[:, h0 + 1:h0 + 2])
                p0 = jnp.exp(s0 - m0).astype(BF16)
                p1 = jnp.exp(s1 - m1).astype(BF16)
                ext = (jnp.dot(p0, v_lo, preferred_element_type=F32)
                       + jnp.dot(p1, v_hi, preferred_element_type=F32))
                e0 = jnp.exp(sink_ref[:, h0:h0 + 1] - m0)
                e1 = jnp.exp(sink_ref[:, h0 + 1:h0 + 2] - m1)
                lane = lax.broadcasted_iota(jnp.int32, (WINDOW, LANES), 1)
                den = ext[:, LANES:] + jnp.where(lane < HEAD_DIM, e0, e1)
                o_sc[rows, blk * LANES:(blk + 1) * LANES] = (ext[:, :LANES] / den).astype(BF16)

    x3 = x_ref[...] + jnp.dot(o_sc[...], wo_ref[...], preferred_element_type=F32)
    y_ref[...] = _mlp(x3, gffn_ref[...], wup_ref, wdown_ref)


def _prompt_attention(q, kd, vd, x2, bias, sinks, lmask, w_o, g_ffn, w_up, w_down, *, bsz, length, tq):
    nt = length // tq
    per_win = tq // WINDOW
    row = lambda width: pl.BlockSpec((tq, width), lambda b, i: (b * nt + i, 0))
    prev = pl.BlockSpec(
        (WINDOW, 2 * D_KV),
        lambda b, i: (jnp.maximum((b * nt + i) * per_win - 1, 0), 0))
    weights = (bias, sinks, lmask, w_o, g_ffn, w_up, w_down)
    return pl.pallas_call(
        _prompt_attn_kernel,
        grid=(bsz, nt),
        in_specs=[row(D_MODEL), row(2 * D_KV), row(2 * D_KV), prev, prev, row(D_MODEL)]
        + [_resident(w) for w in weights],
        out_specs=row(D_MODEL),
        out_shape=jax.ShapeDtypeStruct((bsz * length, D_MODEL), F32),
        scratch_shapes=[pltpu.VMEM((tq, D_MODEL), BF16)],
        compiler_params=pltpu.CompilerParams(
            dimension_semantics=("parallel", "parallel"), vmem_limit_bytes=VMEM_LIMIT),
        name="prompt_attention",
    )(q, kd, vd, kd, vd, x2, *weights)


def _prompt_bias():
    slopes = 2.0 ** (-8.0 * jnp.arange(1, N_HEADS + 1, dtype=F32) / N_HEADS)
    qi = jnp.arange(WINDOW)[:, None]
    kj = jnp.arange(2 * WINDOW)[None, :]
    dist = qi - kj + WINDOW
    valid = (dist >= 0) & (dist < WINDOW)
    valid = jnp.stack([valid, valid & (kj >= WINDOW)], axis=0)[:, None]
    pen = -slopes[:, None, None] * dist.astype(F32)[None]
    return jnp.where(valid, pen[None], MASKED)


SAMPLE_GROUP = 8


def _sample_attn_kernel(qf_ref, kc_ref, vc_ref, kn_ref, vn_ref, bias_c_ref, bias_n_ref, sink_ref, o_ref):
    nt = (((1,), (1,)), ((), ()))
    qf = qf_ref[...]
    s_c = lax.dot_general(qf, kc_ref[...].astype(BF16), nt, preferred_element_type=F32) + bias_c_ref[...]
    s_n = lax.dot_general(qf, kn_ref[...].astype(BF16), nt, preferred_element_type=F32) + bias_n_ref[...]
    sink = sink_ref[...]
    m = jnp.maximum(jnp.maximum(jnp.max(s_c, axis=-1, keepdims=True),
                                jnp.max(s_n, axis=-1, keepdims=True)), sink)
    p_c = jnp.exp(s_c - m)
    p_n = jnp.exp(s_n - m)
    den = (jnp.sum(p_c, axis=-1, keepdims=True) + jnp.sum(p_n, axis=-1, keepdims=True)
           + jnp.exp(sink - m))
    o = (jnp.dot(p_c.astype(BF16), vc_ref[...].astype(BF16), preferred_element_type=F32)
         + jnp.dot(p_n.astype(BF16), vn_ref[...].astype(BF16), preferred_element_type=F32))
    o_ref[...] = o / den


def _sample_attention(qf, cache_k2, cache_v2, k_new, v_new, bias_c, bias_n, sink_rows, *, bsz, t_new):
    g = SAMPLE_GROUP
    rows_q = g * N_HEADS * t_new
    return pl.pallas_call(
        _sample_attn_kernel,
        grid=(bsz // g,),
        in_specs=[pl.BlockSpec((rows_q, D_KV), lambda i: (i, 0)),
                  pl.BlockSpec((g * WINDOW, D_KV), lambda i: (i, 0)),
                  pl.BlockSpec((g * WINDOW, D_KV), lambda i: (i, 0)),
                  pl.BlockSpec((g * t_new, D_KV), lambda i: (i, 0)),
                  pl.BlockSpec((g * t_new, D_KV), lambda i: (i, 0)),
                  _resident(bias_c), _resident(bias_n), _resident(sink_rows)],
        out_specs=pl.BlockSpec((rows_q, D_KV), lambda i: (i, 0)),
        out_shape=jax.ShapeDtypeStruct((bsz * N_HEADS * t_new, D_KV), F32),
        compiler_params=pltpu.CompilerParams(
            dimension_semantics=("parallel",), vmem_limit_bytes=VMEM_LIMIT),
        name="sample_attention",
    )(qf, cache_k2, cache_v2, k_new, v_new, bias_c, bias_n, sink_rows)


def _sample_bias(t_new):
    g = SAMPLE_GROUP
    slopes = 2.0 ** (-8.0 * jnp.arange(1, N_HEADS + 1, dtype=F32) / N_HEADS)
    seq_q = jnp.arange(g)[:, None, None, None, None]
    slope = slopes[None, :, None, None, None]
    t = jnp.arange(t_new)[None, None, :, None, None]
    seq_k = jnp.arange(g)[None, None, None, :, None]
    j = jnp.arange(WINDOW)[None, None, None, None, :]
    dist = t + WINDOW - j
    ok = (seq_q == seq_k) & (dist >= 0) & (dist < WINDOW)
    bias_c = jnp.where(ok, -slope * dist.astype(F32), MASKED).reshape(g * N_HEADS * t_new, g * WINDOW)
    tn = jnp.arange(t_new)[None, None, None, None, :]
    dist_n = t - tn
    ok_n = (seq_q == seq_k) & (dist_n >= 0) & (dist_n < WINDOW)
    bias_n = jnp.where(ok_n, -slope * dist_n.astype(F32), MASKED).reshape(g * N_HEADS * t_new, g * t_new)
    return bias_c, bias_n


def _out_mlp(o, x2, w_o, g_ffn, w_up, w_down, tm):
    rows = x2.shape[0]
    row = pl.BlockSpec((tm, D_MODEL), lambda i: (i, 0))
    weights = (w_o, g_ffn, w_up, w_down)
    return pl.pallas_call(
        _out_mlp_kernel,
        grid=(rows // tm,),
        in_specs=[row, row] + [_resident(w) for w in weights],
        out_specs=row,
        out_shape=jax.ShapeDtypeStruct((rows, D_MODEL), F32),
        compiler_params=pltpu.CompilerParams(
            dimension_semantics=("parallel",), vmem_limit_bytes=VMEM_LIMIT),
        name="out_mlp",
    )(o, x2, *weights)


def _repeat_heads(w):
    d = w.shape[0]
    return jnp.tile(w.reshape(d, N_KV_HEADS, 1, HEAD_DIM), (1, 1, 2, 1)).reshape(d, 2 * D_KV)


def kernel(x_prompt, x_sample, state_ssm_re, state_ssm_im, cache_k, cache_v, norm_mix, norm_ffn,
           ssm_a_re, ssm_a_im, ssm_log_dt, ssm_b_re, ssm_b_im, ssm_c_re, ssm_c_im, ssm_d, ssm_w_glu,
           kv_norm, w_kv, k_norm, w_q, q_norm, attn_sinks, w_o, w_up, w_down):
    bsz, length, d = x_prompt.shape
    dec_b, dec_t, _ = x_sample.shape

    ab_re, ab_im, bb_re, bb_im = _s5_discretize(ssm_a_re[0], ssm_a_im[0], ssm_log_dt[0],
                                                ssm_b_re[0], ssm_b_im[0])
    b_blk, c_blk = _block_diag_weights(bb_re, bb_im, ssm_c_re[0], ssm_c_im[0])
    ab_re = ab_re.reshape(1, D_STATE)
    ab_im = ab_im.reshape(1, D_STATE)
    row = lambda v: v.reshape(1, -1).astype(F32)
    w_glu = ssm_w_glu[0].astype(BF16)
    w_up0, w_up1 = w_up[0].astype(BF16), w_up[1].astype(BF16)
    w_down0, w_down1 = w_down[0].astype(BF16), w_down[1].astype(BF16)
    w_k, w_v = w_kv[:, :D_KV], w_kv[:, D_KV:]
    w_kv_ext = jnp.concatenate([w_k, w_v, _repeat_heads(w_k), _repeat_heads(w_v)], axis=1).astype(BF16)
    w_q0 = w_q[0].astype(BF16)
    w_o0 = w_o[0].astype(BF16)
    g_k = row(jnp.tile(k_norm, D_KV // HEAD_DIM))
    g_q = row(jnp.tile(q_norm[0], D_KV // HEAD_DIM))
    avg = jnp.kron(jnp.eye(D_KV // HEAD_DIM, dtype=F32),
                   jnp.full((HEAD_DIM, HEAD_DIM), 1.0 / HEAD_DIM, F32)).astype(BF16)
    lane = jnp.arange(LANES)
    lmask = jnp.stack([lane < HEAD_DIM, lane >= HEAD_DIM]).astype(BF16)
    sinks = row(attn_sinks[0])

    def layer0(x_tb, h0r, h0i, tc, bb):
        return _s5_layer(x_tb, h0r, h0i, row(norm_mix[0]), row(ssm_d[0]), ab_re, ab_im, b_blk, c_blk,
                         w_glu, tc=tc, bb=bb)

    def mlp_proj(x2d, n_col_blocks, tm):
        return _mlp_proj(x2d, n_col_blocks, tm, row(norm_ffn[0]), w_up0, w_down0, row(kv_norm),
                         w_kv_ext, g_k, row(norm_mix[1]), w_q0, g_q, avg)

    zeros = jnp.zeros((bsz, D_STATE), F32)
    x1, hr_p, hi_p = layer0(jnp.swapaxes(x_prompt, 0, 1), zeros, zeros, 16, bsz)
    x2, q, kd, vd, k, v = mlp_proj(x1.reshape(length, bsz * d), bsz, 512)
    y_prompt = _prompt_attention(q, kd, vd, x2, _prompt_bias(), sinks, lmask, w_o0, row(norm_ffn[1]),
                                 w_up1, w_down1, bsz=bsz, length=length, tq=512)
    y_prompt = y_prompt.reshape(bsz, length, d)
    k_p = k.reshape(bsz, length, N_KV_HEADS, HEAD_DIM)[:, -WINDOW:]
    v_p = v.reshape(bsz, length, N_KV_HEADS, HEAD_DIM)[:, -WINDOW:]
    state_shape = (1, bsz, N_GROUPS, STATE_N)
    re_p = hr_p.reshape(state_shape).astype(state_ssm_re.dtype)
    im_p = hi_p.reshape(state_shape).astype(state_ssm_im.dtype)

    x1s, hr_s, hi_s = layer0(jnp.swapaxes(x_sample, 0, 1), state_ssm_re[0].reshape(dec_b, D_STATE),
                             state_ssm_im[0].reshape(dec_b, D_STATE), dec_t, dec_b)
    x1s = jnp.swapaxes(x1s, 0, 1).reshape(dec_b * dec_t, d)
    x2s, qs, _, _, ks, vs = mlp_proj(x1s, 1, dec_b * dec_t)
    q5 = qs.reshape(dec_b, dec_t, N_KV_HEADS, Q_PER_KV, HEAD_DIM).transpose(0, 2, 3, 1, 4)
    own = jnp.eye(N_KV_HEADS, dtype=BF16)[None, :, None, None, :, None]
    qf = (q5[:, :, :, :, None, :] * own).reshape(dec_b * N_HEADS * dec_t, D_KV)
    bias_c, bias_n = _sample_bias(dec_t)
    sink_rows = jnp.broadcast_to(attn_sinks[0].astype(F32)[None, :, None],
                                 (SAMPLE_GROUP, N_HEADS, dec_t)).reshape(-1, 1)
    o_full = _sample_attention(qf, cache_k.reshape(dec_b * WINDOW, D_KV), cache_v.reshape(dec_b * WINDOW, D_KV),
                               ks, vs, bias_c, bias_n, sink_rows, bsz=dec_b, t_new=dec_t)
    o6 = o_full.reshape(dec_b, N_KV_HEADS, Q_PER_KV, dec_t, N_KV_HEADS, HEAD_DIM)
    o_s = jnp.stack([o6[:, kh, :, :, kh, :] for kh in range(N_KV_HEADS)], axis=1)
    o_s = o_s.transpose(0, 3, 1, 2, 4).reshape(dec_b * dec_t, d).astype(BF16)
    y_sample = _out_mlp(o_s, x2s, w_o0, row(norm_ffn[1]), w_up1, w_down1, dec_b * dec_t)
    y_sample = y_sample.reshape(dec_b, dec_t, d)
    ks4 = ks.reshape(dec_b, dec_t, N_KV_HEADS, HEAD_DIM).astype(cache_k.dtype)
    vs4 = vs.reshape(dec_b, dec_t, N_KV_HEADS, HEAD_DIM).astype(cache_v.dtype)
    k_s = jnp.concatenate([cache_k, ks4], axis=1)[:, -WINDOW:]
    v_s = jnp.concatenate([cache_v, vs4], axis=1)[:, -WINDOW:]
    state_shape_s = (1, dec_b, N_GROUPS, STATE_N)
    re_s = hr_s.reshape(state_shape_s).astype(state_ssm_re.dtype)
    im_s = hi_s.reshape(state_shape_s).astype(state_ssm_im.dtype)

    return (y_prompt, y_sample, re_p, im_p, k_p, v_p, re_s, im_s, k_s, v_s)
```

```python
import functools

import jax
import jax.numpy as jnp
import numpy as np
from jax import lax
from jax.experimental import pallas as pl
from jax.experimental.pallas import tpu as pltpu

F32 = jnp.float32
BF16 = jnp.bfloat16

D_MODEL = 1024
GROUP_SIZE = 16
N_GROUPS = D_MODEL // GROUP_SIZE
STATE_N = 64
D_STATE = N_GROUPS * STATE_N
HEAD_DIM = 64
N_HEADS = D_MODEL // HEAD_DIM
N_KV_HEADS = 4
Q_PER_KV = N_HEADS // N_KV_HEADS
D_KV = N_KV_HEADS * HEAD_DIM
WINDOW = 128
D_FF = 4 * D_MODEL
EPS = 1e-6
MASKED = -1e30

LANES = 128
GROUPS_PER_BLOCK = LANES // GROUP_SIZE
N_LANE_BLOCKS = D_MODEL // LANES
STATES_PER_BLOCK = GROUPS_PER_BLOCK * STATE_N
SUB_BATCH = 16
SCAN_LANES = 256
SAMPLE_GROUP = 8
VMEM_LIMIT = 56 * 1024 * 1024
CAST_BLOCK_BYTES = 4 * 1024 * 1024


def _resident(a):
    return pl.BlockSpec(a.shape, lambda *_: (0,) * a.ndim, pipeline_mode=pl.Buffered(1))


def _resident_layer(a, layer):
    return pl.BlockSpec((None,) + a.shape[1:], lambda *_: (layer,) + (0,) * (a.ndim - 1),
                        pipeline_mode=pl.Buffered(1))


def _rms(x, g):
    return x * lax.rsqrt(jnp.mean(x * x, axis=-1, keepdims=True) + EPS) * g


def _head_mean_sq(x, avg):
    sq = x * x
    hi = sq.astype(BF16)
    lo = (sq - hi.astype(F32)).astype(BF16)
    return (jnp.dot(hi, avg, preferred_element_type=F32)
            + jnp.dot(lo, avg, preferred_element_type=F32))


def _cast_kernel(x_ref, o_ref):
    o_ref[...] = x_ref[...].astype(o_ref.dtype)


def _to_bf16(w):
    n, r, c = w.shape
    br = min(r, CAST_BLOCK_BYTES // (4 * c))
    spec = pl.BlockSpec((1, br, c), lambda i, j: (i, j, 0))
    return pl.pallas_call(
        _cast_kernel, grid=(n, r // br), in_specs=[spec], out_specs=spec,
        out_shape=jax.ShapeDtypeStruct(w.shape, BF16),
        compiler_params=pltpu.CompilerParams(dimension_semantics=("parallel", "parallel")),
        name="to_bf16",
    )(w)


def _discretize_kernel(a_re_ref, a_im_ref, log_dt_ref, b_re_ref, b_im_ref,
                       ab_re_ref, ab_im_ref, bb_re_ref, bb_im_ref):
    dt = jnp.exp(log_dt_ref[...])
    a_re = a_re_ref[...]
    a_im = a_im_ref[...]
    mag = jnp.exp(dt * a_re)
    ab_re = mag * jnp.cos(dt * a_im)
    ab_im = mag * jnp.sin(dt * a_im)
    den = a_re * a_re + a_im * a_im
    nr = ab_re - 1.0
    ni = ab_im
    f_re = (nr * a_re + ni * a_im) / den
    f_im = (ni * a_re - nr * a_im) / den
    ab_re_ref[...] = ab_re
    ab_im_ref[...] = ab_im
    b_re = b_re_ref[...]
    b_im = b_im_ref[...]
    fr = f_re[:, None, :]
    fi = f_im[:, None, :]
    bb_re_ref[...] = fr * b_re - fi * b_im
    bb_im_ref[...] = fr * b_im + fi * b_re


def _s5_discretize(a_re, a_im, log_dt, b_re, b_im):
    g, n = a_re.shape
    c = b_re.shape[-1]
    out = pl.pallas_call(
        _discretize_kernel,
        out_shape=(jax.ShapeDtypeStruct((g, n), F32), jax.ShapeDtypeStruct((g, n), F32),
                   jax.ShapeDtypeStruct((g, c, n), F32), jax.ShapeDtypeStruct((g, c, n), F32)),
        name="s5_discretize",
    )(a_re, a_im, log_dt.reshape(g, 1), jnp.swapaxes(b_re, 1, 2), jnp.swapaxes(b_im, 1, 2))
    return out


def _block_diag_weights(bb_re, bb_im, c_re, c_im):
    nb, gb = N_LANE_BLOCKS, GROUPS_PER_BLOCK
    eye = jnp.eye(gb, dtype=F32)
    bb = jnp.stack([bb_re, bb_im], axis=0).reshape(2, nb, gb, GROUP_SIZE, STATE_N)
    b_blk = jnp.einsum("rjgcn,gh->jgcrhn", bb, eye).reshape(nb, LANES, 2 * STATES_PER_BLOCK)
    cc = jnp.stack([c_re, -c_im], axis=0).reshape(2, nb, gb, GROUP_SIZE, STATE_N)
    c_blk = jnp.einsum("rjgcn,gh->jrgnhc", cc, eye).reshape(nb, 2 * STATES_PER_BLOCK, LANES)
    return b_blk.astype(BF16), c_blk.astype(BF16)


def _s5_layer_kernel(x_ref, h0r_ref, h0i_ref, g_ref, d_ref, abr_ref, abi_ref, bblk_ref, cblk_ref,
                     wglu_ref, o_ref, hro_ref, hio_ref, hr_sc, hi_sc, bur, bui, hn_sl, mix_sl,
                     *, state_lanes_batch):
    bb, tc, d = x_ref.shape
    spb = STATES_PER_BLOCK
    step_idx = pl.program_id(1)

    @pl.when(step_idx == 0)
    def _():
        if state_lanes_batch:
            hr_sc[...] = h0r_ref[...].T
            hi_sc[...] = h0i_ref[...].T
        else:
            hr_sc[...] = h0r_ref[...]
            hi_sc[...] = h0i_ref[...]

    g = g_ref[...]

    def load_seq(b, carry):
        hb = _rms(x_ref[b], g)
        for j in range(N_LANE_BLOCKS):
            hn_sl[j, pl.ds(b, tc, stride=bb), :] = hb[:, j * LANES:(j + 1) * LANES]
        return carry

    lax.fori_loop(0, bb, load_seq, 0)

    for j in range(N_LANE_BLOCKS):
        bu = jnp.dot(hn_sl[j].astype(BF16), bblk_ref[j], preferred_element_type=F32)
        bur[:, j * spb:(j + 1) * spb] = bu[:, :spb]
        bui[:, j * spb:(j + 1) * spb] = bu[:, spb:]

    for s in range(D_STATE // SCAN_LANES):
        cs = slice(s * SCAN_LANES, (s + 1) * SCAN_LANES)
        ar = jnp.broadcast_to(abr_ref[:, cs], (SUB_BATCH, SCAN_LANES))
        ai = jnp.broadcast_to(abi_ref[:, cs], (SUB_BATCH, SCAN_LANES))

        def sub_batch(sb, carry, cs=cs, ar=ar, ai=ai):
            r0 = pl.multiple_of(sb * SUB_BATCH, SUB_BATCH)

            def step(t, h):
                h_re, h_im = h
                row = pl.multiple_of(t * bb + r0, SUB_BATCH)
                n_re = ar * h_re - ai * h_im + bur[pl.ds(row, SUB_BATCH), cs]
                n_im = ar * h_im + ai * h_re + bui[pl.ds(row, SUB_BATCH), cs]
                bur[pl.ds(row, SUB_BATCH), cs] = n_re
                bui[pl.ds(row, SUB_BATCH), cs] = n_im
                return n_re, n_im

            h_re, h_im = lax.fori_loop(
                0, tc, step, (hr_sc[pl.ds(r0, SUB_BATCH), cs], hi_sc[pl.ds(r0, SUB_BATCH), cs]),
                unroll=min(tc, 4))
            hr_sc[pl.ds(r0, SUB_BATCH), cs] = h_re
            hi_sc[pl.ds(r0, SUB_BATCH), cs] = h_im
            return carry

        lax.fori_loop(0, bb // SUB_BATCH, sub_batch, 0)

    gelu_cols = []
    for j in range(N_LANE_BLOCKS):
        xr = bur[:, j * spb:(j + 1) * spb].astype(BF16)
        xi = bui[:, j * spb:(j + 1) * spb].astype(BF16)
        y = (jnp.dot(xr, cblk_ref[j, :spb, :], preferred_element_type=F32)
             + jnp.dot(xi, cblk_ref[j, spb:, :], preferred_element_type=F32))
        y = y + d_ref[:, j * LANES:(j + 1) * LANES] * hn_sl[j]
        gelu_cols.append(jax.nn.gelu(y).astype(BF16))
    z = jnp.dot(jnp.concatenate(gelu_cols, axis=1), wglu_ref[...], preferred_element_type=F32)
    mix = z[:, :d] * jax.nn.sigmoid(z[:, d:])
    for j in range(N_LANE_BLOCKS):
        mix_sl[j] = mix[:, j * LANES:(j + 1) * LANES]

    def store_seq(b, carry):
        for j in range(N_LANE_BLOCKS):
            ls = slice(j * LANES, (j + 1) * LANES)
            o_ref[b, :, ls] = x_ref[b, :, ls] + mix_sl[j, pl.ds(b, tc, stride=bb), :]
        return carry

    lax.fori_loop(0, bb, store_seq, 0)

    @pl.when(step_idx == pl.num_programs(1) - 1)
    def _():
        if state_lanes_batch:
            hro_ref[...] = hr_sc[...].T
            hio_ref[...] = hi_sc[...].T
        else:
            hro_ref[...] = hr_sc[...]
            hio_ref[...] = hi_sc[...]


def _s5_layer(x, h0r, h0i, g_mix, d_skip, ab_re, ab_im, b_blk, c_blk, w_glu, *, tc, bb,
              state_lanes_batch):
    bsz, length, d = x.shape
    rows = tc * bb
    weights = (g_mix, d_skip, ab_re, ab_im, b_blk, c_blk, w_glu)
    if state_lanes_batch:
        state_spec = pl.BlockSpec((D_STATE, bb), lambda b, i: (0, b))
        state_shape = jax.ShapeDtypeStruct((D_STATE, bsz), F32)
    else:
        state_spec = pl.BlockSpec((bb, D_STATE), lambda b, i: (b, 0))
        state_shape = jax.ShapeDtypeStruct((bsz, D_STATE), F32)
    return pl.pallas_call(
        functools.partial(_s5_layer_kernel, state_lanes_batch=state_lanes_batch),
        grid=(bsz // bb, length // tc),
        in_specs=[pl.BlockSpec((bb, tc, d), lambda b, i: (b, i, 0)), state_spec, state_spec]
        + [_resident(w) for w in weights],
        out_specs=(pl.BlockSpec((bb, tc, d), lambda b, i: (b, i, 0)), state_spec, state_spec),
        out_shape=(jax.ShapeDtypeStruct((bsz, length, d), F32), state_shape, state_shape),
        scratch_shapes=[pltpu.VMEM((bb, D_STATE), F32), pltpu.VMEM((bb, D_STATE), F32),
                        pltpu.VMEM((rows, D_STATE), F32), pltpu.VMEM((rows, D_STATE), F32),
                        pltpu.VMEM((N_LANE_BLOCKS, rows, LANES), F32),
                        pltpu.VMEM((N_LANE_BLOCKS, rows, LANES), F32)],
        compiler_params=pltpu.CompilerParams(
            dimension_semantics=("parallel", "arbitrary"), vmem_limit_bytes=VMEM_LIMIT),
        name="s5_layer",
    )(x, h0r, h0i, *weights)


def _mlp(x, g, w_up_ref, w_down_ref):
    hn = _rms(x, g).astype(BF16)
    acc = x
    n_chunks = w_up_ref.shape[1] // D_MODEL
    for c in range(n_chunks):
        cs = slice(c * D_MODEL, (c + 1) * D_MODEL)
        u = jnp.dot(hn, w_up_ref[:, cs], preferred_element_type=F32)
        u = jnp.square(jnp.maximum(u, 0.0)).astype(BF16)
        acc = acc + jnp.dot(u, w_down_ref[cs, :], preferred_element_type=F32)
    return acc


def _mlp_proj_kernel(x_ref, gffn_ref, wup_ref, wdown_ref, gkv_ref, wkv_ref, gk_ref, gq1_ref, wq_ref,
                     gq_ref, avg_ref, x2_ref, q_ref, kd_ref, vd_ref, k_ref, v_ref):
    x2 = _mlp(x_ref[...], gffn_ref[...], wup_ref, wdown_ref)
    x2_ref[...] = x2
    normed = x2 * lax.rsqrt(jnp.mean(x2 * x2, axis=-1, keepdims=True) + EPS)
    avg = avg_ref[...]

    kv = jnp.dot((normed * gkv_ref[...]).astype(BF16), wkv_ref[...], preferred_element_type=F32)
    k = kv[:, :D_KV]
    k_ref[...] = k * lax.rsqrt(_head_mean_sq(k, avg) + EPS) * gk_ref[...]
    v_ref[...] = kv[:, D_KV:2 * D_KV]
    for c in range(2):
        cs = slice(2 * D_KV + c * D_KV, 2 * D_KV + (c + 1) * D_KV)
        kd = kv[:, cs]
        kd_ref[:, c * D_KV:(c + 1) * D_KV] = (
            kd * lax.rsqrt(_head_mean_sq(kd, avg) + EPS) * gk_ref[...]).astype(BF16)
    vd_ref[...] = kv[:, 4 * D_KV:].astype(BF16)

    hq = (normed * gq1_ref[...]).astype(BF16)
    for c in range(D_MODEL // D_KV):
        cs = slice(c * D_KV, (c + 1) * D_KV)
        q = jnp.dot(hq, wq_ref[:, cs], preferred_element_type=F32)
        q = q * lax.rsqrt(_head_mean_sq(q, avg) + EPS) * gq_ref[...]
        q_ref[:, cs] = (q * (HEAD_DIM ** -0.5)).astype(BF16)


def _mlp_proj(x2d, tm, g_ffn, w_up, w_down, g_kv, w_kv_ext, g_k, g_q1, w_q, g_q, avg):
    rows = x2d.shape[0]
    row = lambda width: pl.BlockSpec((tm, width), lambda i: (i, 0))
    tail = (g_kv, w_kv_ext, g_k, g_q1, w_q, g_q, avg)
    return pl.pallas_call(
        _mlp_proj_kernel,
        grid=(rows // tm,),
        in_specs=[row(D_MODEL), _resident(g_ffn), _resident_layer(w_up, 0), _resident_layer(w_down, 0)]
        + [_resident(w) for w in tail],
        out_specs=(row(D_MODEL), row(D_MODEL), row(2 * D_KV), row(2 * D_KV), row(D_KV), row(D_KV)),
        out_shape=(jax.ShapeDtypeStruct((rows, D_MODEL), F32),
                   jax.ShapeDtypeStruct((rows, D_MODEL), BF16),
                   jax.ShapeDtypeStruct((rows, 2 * D_KV), BF16),
                   jax.ShapeDtypeStruct((rows, 2 * D_KV), BF16),
                   jax.ShapeDtypeStruct((rows, D_KV), F32),
                   jax.ShapeDtypeStruct((rows, D_KV), F32)),
        compiler_params=pltpu.CompilerParams(
            dimension_semantics=("parallel",), vmem_limit_bytes=VMEM_LIMIT),
        name="mlp_proj",
    )(x2d, g_ffn, w_up, w_down, *tail)


def _prompt_attn_kernel(q_ref, kd_ref, vd_ref, kdp_ref, vdp_ref, x_ref, bias_ref, sink_ref, lmask_ref,
                        wo_ref, gffn_ref, wup_ref, wdown_ref, y_ref, o_sc):
    tq = q_ref.shape[0]
    first = (pl.program_id(1) == 0).astype(jnp.int32)
    m_lo = lmask_ref[0:1, :]
    m_hi = lmask_ref[1:2, :]
    ones_lo = jnp.broadcast_to(m_lo, (2 * WINDOW, LANES))
    ones_hi = jnp.broadcast_to(m_hi, (2 * WINDOW, LANES))
    nt = (((1,), (1,)), ((), ()))

    for r in range(tq // WINDOW):
        rows = slice(r * WINDOW, (r + 1) * WINDOW)
        sel = first if r == 0 else 0
        for kh in range(N_KV_HEADS):
            ls = slice(kh * LANES, (kh + 1) * LANES)
            if r == 0:
                kwin = jnp.concatenate([kdp_ref[:, ls], kd_ref[0:WINDOW, ls]], axis=0)
                vwin = jnp.concatenate([vdp_ref[:, ls], vd_ref[0:WINDOW, ls]], axis=0)
            else:
                kwin = kd_ref[(r - 1) * WINDOW:(r + 1) * WINDOW, ls]
                vwin = vd_ref[(r - 1) * WINDOW:(r + 1) * WINDOW, ls]
            k_lo = kwin * m_lo
            k_hi = kwin * m_hi
            v_lo = jnp.concatenate([vwin * m_lo, ones_lo], axis=1)
            v_hi = jnp.concatenate([vwin * m_hi, ones_hi], axis=1)
            for pair in range(Q_PER_KV // 2):
                blk = kh * (Q_PER_KV // 2) + pair
                h0 = 2 * blk
                qp = q_ref[rows, blk * LANES:(blk + 1) * LANES]
                s0 = lax.dot_general(qp, k_lo, nt, preferred_element_type=F32) + bias_ref[sel, h0]
                s1 = lax.dot_general(qp, k_hi, nt, preferred_element_type=F32) + bias_ref[sel, h0 + 1]
                m0 = jnp.maximum(jnp.max(s0, axis=-1, keepdims=True), sink_ref[:, h0:h0 + 1])
                m1 = jnp.maximum(jnp.max(s1, axis=-1, keepdims=True), sink_ref[:, h0 + 1:h0 + 2])
                p0 = jnp.exp(s0 - m0).astype(BF16)
                p1 = jnp.exp(s1 - m1).astype(BF16)
                ext = (jnp.dot(p0, v_lo, preferred_element_type=F32)
                       + jnp.dot(p1, v_hi, preferred_element_type=F32))
                e0 = jnp.exp(sink_ref[:, h0:h0 + 1] - m0)
                e1 = jnp.exp(sink_ref[:, h0 + 1:h0 + 2] - m1)
                lane = lax.broadcasted_iota(jnp.int32, (WINDOW, LANES), 1)
                den = ext[:, LANES:] + jnp.where(lane < HEAD_DIM, e0, e1)
                o_sc[rows, blk * LANES:(blk + 1) * LANES] = (ext[:, :LANES] / den).astype(BF16)

    x3 = x_ref[...] + jnp.dot(o_sc[...], wo_ref[...], preferred_element_type=F32)
    y_ref[...] = _mlp(x3, gffn_ref[...], wup_ref, wdown_ref)


def _prompt_attention(q, kd, vd, x2, bias, sinks, lmask, w_o, g_ffn, w_up, w_down, *, bsz, length, tq):
    nt = length // tq
    per_win = tq // WINDOW
    row = lambda width: pl.BlockSpec((tq, width), lambda b, i: (b * nt + i, 0))
    prev = pl.BlockSpec(
        (WINDOW, 2 * D_KV),
        lambda b, i: (jnp.maximum((b * nt + i) * per_win - 1, 0), 0))
    return pl.pallas_call(
        _prompt_attn_kernel,
        grid=(bsz, nt),
        in_specs=[row(D_MODEL), row(2 * D_KV), row(2 * D_KV), prev, prev, row(D_MODEL),
                  _resident(bias), _resident(sinks), _resident(lmask), _resident(w_o), _resident(g_ffn),
                  _resident_layer(w_up, 1), _resident_layer(w_down, 1)],
        out_specs=row(D_MODEL),
        out_shape=jax.ShapeDtypeStruct((bsz * length, D_MODEL), F32),
        scratch_shapes=[pltpu.VMEM((tq, D_MODEL), BF16)],
        compiler_params=pltpu.CompilerParams(
            dimension_semantics=("parallel", "parallel"), vmem_limit_bytes=VMEM_LIMIT),
        name="prompt_attention",
    )(q, kd, vd, kd, vd, x2, bias, sinks, lmask, w_o, g_ffn, w_up, w_down)


def _alibi_slopes():
    return (2.0 ** (-8.0 * np.arange(1, N_HEADS + 1, dtype=np.float32) / N_HEADS)).astype(np.float32)


def _prompt_bias():
    qi = np.arange(WINDOW)[:, None]
    kj = np.arange(2 * WINDOW)[None, :]
    dist = qi - kj + WINDOW
    valid = (dist >= 0) & (dist < WINDOW)
    valid = np.stack([valid, valid & (kj >= WINDOW)], axis=0)[:, None]
    pen = -_alibi_slopes()[:, None, None] * dist.astype(np.float32)[None]
    return np.where(valid, pen[None], np.float32(MASKED)).astype(np.float32)


def _sample_attn_kernel(q_ref, kc_ref, vc_ref, kn_ref, vn_ref, bias_c_ref, bias_n_ref, sink_ref,
                        o_ref, kbuf_ref, vbuf_ref):
    g = kc_ref.shape[0]
    t_new = kn_ref.shape[0] // g
    nt = (((1,), (1,)), ((), ()))
    pad = jnp.zeros((LANES - g * t_new, D_KV), F32)
    kn_t = jnp.concatenate([kn_ref[...], pad], axis=0).T
    vn_t = jnp.concatenate([vn_ref[...], pad], axis=0).T
    lane = lax.broadcasted_iota(jnp.int32, (HEAD_DIM, WINDOW), 1)
    is_new = lane >= WINDOW - t_new

    for kh in range(N_KV_HEADS):
        hs = slice(kh * HEAD_DIM, (kh + 1) * HEAD_DIM)
        q = q_ref[kh]
        k_t = jnp.concatenate([kc_ref[s, kh] for s in range(g)], axis=1)
        v_t = jnp.concatenate([vc_ref[s, kh] for s in range(g)], axis=1)
        s_c = jnp.dot(q, k_t.astype(BF16), preferred_element_type=F32) + bias_c_ref[kh]
        s_n = jnp.dot(q, kn_t[hs].astype(BF16), preferred_element_type=F32) + bias_n_ref[kh]
        sink = sink_ref[kh]
        m = jnp.maximum(jnp.maximum(jnp.max(s_c, axis=-1, keepdims=True),
                                    jnp.max(s_n, axis=-1, keepdims=True)), sink)
        p_c = jnp.exp(s_c - m)
        p_n = jnp.exp(s_n - m)
        den = (jnp.sum(p_c, axis=-1, keepdims=True) + jnp.sum(p_n, axis=-1, keepdims=True)
               + jnp.exp(sink - m))
        o = (lax.dot_general(p_c.astype(BF16), v_t.astype(BF16), nt, preferred_element_type=F32)
             + lax.dot_general(p_n.astype(BF16), vn_t[hs].astype(BF16), nt, preferred_element_type=F32))
        o_ref[kh] = o / den

        for s in range(g):
            shift_new = (WINDOW - t_new - s * t_new) % WINDOW
            k_new = pltpu.roll(kn_t[hs], shift_new, 1)
            v_new = pltpu.roll(vn_t[hs], shift_new, 1)
            kbuf_ref[s, kh] = jnp.where(is_new, k_new, pltpu.roll(kc_ref[s, kh], WINDOW - t_new, 1))
            vbuf_ref[s, kh] = jnp.where(is_new, v_new, pltpu.roll(vc_ref[s, kh], WINDOW - t_new, 1))


def _sample_attention(q_heads, cache_kt, cache_vt, k_new, v_new, bias_c, bias_n, sink_rows, *, t_new):
    g = SAMPLE_GROUP
    bsz = cache_kt.shape[0]
    rows_q = g * Q_PER_KV * t_new
    cache_spec = pl.BlockSpec((g, N_KV_HEADS, HEAD_DIM, WINDOW), lambda i: (i, 0, 0, 0))
    q_spec = pl.BlockSpec((N_KV_HEADS, rows_q, HEAD_DIM), lambda i: (0, i, 0))
    new_spec = pl.BlockSpec((g * t_new, D_KV), lambda i: (i, 0))
    return pl.pallas_call(
        _sample_attn_kernel,
        grid=(bsz // g,),
        in_specs=[q_spec, cache_spec, cache_spec, new_spec, new_spec,
                  _resident(bias_c), _resident(bias_n), _resident(sink_rows)],
        out_specs=(q_spec, cache_spec, cache_spec),
        out_shape=(jax.ShapeDtypeStruct(q_heads.shape, F32),
                   jax.ShapeDtypeStruct(cache_kt.shape, cache_kt.dtype),
                   jax.ShapeDtypeStruct(cache_vt.shape, cache_vt.dtype)),
        compiler_params=pltpu.CompilerParams(
            dimension_semantics=("parallel",), vmem_limit_bytes=VMEM_LIMIT),
        name="sample_attention",
    )(q_heads, cache_kt, cache_vt, k_new, v_new, bias_c, bias_n, sink_rows)


def _sample_bias(t_new):
    g = SAMPLE_GROUP
    slope = _alibi_slopes().reshape(N_KV_HEADS, 1, Q_PER_KV, 1, 1, 1)
    seq_q = np.arange(g).reshape(1, g, 1, 1, 1, 1)
    t = np.arange(t_new).reshape(1, 1, 1, t_new, 1, 1)
    seq_k = np.arange(g).reshape(1, 1, 1, 1, g, 1)
    j = np.arange(WINDOW).reshape(1, 1, 1, 1, 1, WINDOW)
    dist = t + WINDOW - j
    ok = (seq_q == seq_k) & (dist >= 0) & (dist < WINDOW)
    rows = g * Q_PER_KV * t_new
    bias_c = np.where(ok, -slope * dist.astype(np.float32), np.float32(MASKED))
    bias_c = bias_c.reshape(N_KV_HEADS, rows, g * WINDOW).astype(np.float32)
    tn = np.arange(t_new).reshape(1, 1, 1, 1, 1, t_new)
    dist_n = t - tn
    ok_n = (seq_q == seq_k) & (dist_n >= 0) & (dist_n < WINDOW)
    bias_n = np.where(ok_n, -slope * dist_n.astype(np.float32), np.float32(MASKED))
    bias_n = bias_n.reshape(N_KV_HEADS, rows, g * t_new)
    bias_n = np.pad(bias_n, ((0, 0), (0, 0), (0, LANES - g * t_new)), constant_values=MASKED)
    return bias_c, bias_n.astype(np.float32)


def _out_mlp_kernel(o_ref, x_ref, wo_ref, gffn_ref, wup_ref, wdown_ref, y_ref):
    x3 = x_ref[...] + jnp.dot(o_ref[...], wo_ref[...], preferred_element_type=F32)
    y_ref[...] = _mlp(x3, gffn_ref[...], wup_ref, wdown_ref)


def _out_mlp(o, x2, w_o, g_ffn, w_up, w_down, tm):
    rows = x2.shape[0]
    row = pl.BlockSpec((tm, D_MODEL), lambda i: (i, 0))
    return pl.pallas_call(
        _out_mlp_kernel,
        grid=(rows // tm,),
        in_specs=[row, row, _resident(w_o), _resident(g_ffn), _resident_layer(w_up, 1),
                  _resident_layer(w_down, 1)],
        out_specs=row,
        out_shape=jax.ShapeDtypeStruct((rows, D_MODEL), F32),
        compiler_params=pltpu.CompilerParams(
            dimension_semantics=("parallel",), vmem_limit_bytes=VMEM_LIMIT),
        name="out_mlp",
    )(o, x2, w_o, g_ffn, w_up, w_down)


def _repeat_heads(w):
    d = w.shape[0]
    return jnp.tile(w.reshape(d, N_KV_HEADS, 1, HEAD_DIM), (1, 1, 2, 1)).reshape(d, 2 * D_KV)


def kernel(x_prompt, x_sample, state_ssm_re, state_ssm_im, cache_k, cache_v, norm_mix, norm_ffn,
           ssm_a_re, ssm_a_im, ssm_log_dt, ssm_b_re, ssm_b_im, ssm_c_re, ssm_c_im, ssm_d, ssm_w_glu,
           kv_norm, w_kv, k_norm, w_q, q_norm, attn_sinks, w_o, w_up, w_down):
    bsz, length, d = x_prompt.shape
    dec_b, dec_t, _ = x_sample.shape

    ab_re, ab_im, bb_re, bb_im = _s5_discretize(ssm_a_re[0], ssm_a_im[0], ssm_log_dt[0],
                                                ssm_b_re[0], ssm_b_im[0])
    b_blk, c_blk = _block_diag_weights(bb_re, bb_im, ssm_c_re[0], ssm_c_im[0])
    ab_re = ab_re.reshape(1, D_STATE)
    ab_im = ab_im.reshape(1, D_STATE)
    row = lambda v: v.reshape(1, -1).astype(F32)
    w_glu = ssm_w_glu[0].astype(BF16)
    w_up_b = _to_bf16(w_up)
    w_down_b = _to_bf16(w_down)
    w_k, w_v = w_kv[:, :D_KV], w_kv[:, D_KV:]
    w_kv_ext = jnp.concatenate([w_k, w_v, _repeat_heads(w_k), _repeat_heads(w_v)], axis=1).astype(BF16)
    w_q0 = w_q[0].astype(BF16)
    w_o0 = w_o[0].astype(BF16)
    g_k = row(jnp.tile(k_norm, D_KV // HEAD_DIM))
    g_q = row(jnp.tile(q_norm[0], D_KV // HEAD_DIM))
    avg = jnp.asarray(np.kron(np.eye(D_KV // HEAD_DIM, dtype=np.float32),
                              np.full((HEAD_DIM, HEAD_DIM), 1.0 / HEAD_DIM, np.float32)), BF16)
    lane = np.arange(LANES)
    lmask = jnp.asarray(np.stack([lane < HEAD_DIM, lane >= HEAD_DIM]), BF16)
    sinks = row(attn_sinks[0])

    def layer0(x, h0r, h0i, tc, bb, state_lanes_batch):
        return _s5_layer(x, h0r, h0i, row(norm_mix[0]), row(ssm_d[0]), ab_re, ab_im, b_blk, c_blk,
                         w_glu, tc=tc, bb=bb, state_lanes_batch=state_lanes_batch)

    def mlp_proj(x2d, tm):
        return _mlp_proj(x2d, tm, row(norm_ffn[0]), w_up_b, w_down_b, row(kv_norm), w_kv_ext, g_k,
                         row(norm_mix[1]), w_q0, g_q, avg)

    zeros = jnp.zeros((bsz, D_STATE), F32)
    x1, hr_p, hi_p = layer0(x_prompt, zeros, zeros, 16, bsz, False)
    x2, q, kd, vd, k, v = mlp_proj(x1.reshape(bsz * length, d), 512)
    y_prompt = _prompt_attention(q, kd, vd, x2, jnp.asarray(_prompt_bias()), sinks, lmask, w_o0,
                                 row(norm_ffn[1]), w_up_b, w_down_b, bsz=bsz, length=length, tq=512)
    y_prompt = y_prompt.reshape(bsz, length, d)
    k_p = k.reshape(bsz, length, D_KV)[:, -WINDOW:].reshape(bsz, WINDOW, N_KV_HEADS, HEAD_DIM)
    v_p = v.reshape(bsz, length, D_KV)[:, -WINDOW:].reshape(bsz, WINDOW, N_KV_HEADS, HEAD_DIM)
    state_shape = (1, bsz, N_GROUPS, STATE_N)
    re_p = hr_p.reshape(state_shape).astype(state_ssm_re.dtype)
    im_p = hi_p.reshape(state_shape).astype(state_ssm_im.dtype)

    h0r_t = jnp.transpose(state_ssm_re[0], (1, 2, 0)).reshape(D_STATE, dec_b).astype(F32)
    h0i_t = jnp.transpose(state_ssm_im[0], (1, 2, 0)).reshape(D_STATE, dec_b).astype(F32)
    x1s, hr_s, hi_s = layer0(x_sample, h0r_t, h0i_t, dec_t, dec_b, True)
    x2s, qs, _, _, ks, vs = mlp_proj(x1s.reshape(dec_b * dec_t, d), dec_b * dec_t)
    q_heads = qs.reshape(dec_b, dec_t, N_KV_HEADS, Q_PER_KV, HEAD_DIM).transpose(2, 0, 3, 1, 4)
    q_heads = q_heads.reshape(N_KV_HEADS, dec_b * Q_PER_KV * dec_t, HEAD_DIM)
    bias_c, bias_n = _sample_bias(dec_t)
    sink_rows = jnp.broadcast_to(
        attn_sinks[0].astype(F32).reshape(N_KV_HEADS, 1, Q_PER_KV, 1),
        (N_KV_HEADS, SAMPLE_GROUP, Q_PER_KV, dec_t)).reshape(N_KV_HEADS, -1, 1)
    o_heads, k_s_t, v_s_t = _sample_attention(
        q_heads, jnp.transpose(cache_k, (0, 2, 3, 1)), jnp.transpose(cache_v, (0, 2, 3, 1)),
        ks.astype(cache_k.dtype), vs.astype(cache_v.dtype), jnp.asarray(bias_c), jnp.asarray(bias_n),
        sink_rows, t_new=dec_t)
    o_s = o_heads.reshape(N_KV_HEADS, dec_b, Q_PER_KV, dec_t, HEAD_DIM).transpose(1, 3, 0, 2, 4)
    o_s = o_s.reshape(dec_b * dec_t, d).astype(BF16)
    y_sample = _out_mlp(o_s, x2s, w_o0, row(norm_ffn[1]), w_up_b, w_down_b, dec_b * dec_t)
    y_sample = y_sample.reshape(dec_b, dec_t, d)
    k_s = jnp.transpose(k_s_t, (0, 3, 1, 2))
    v_s = jnp.transpose(v_s_t, (0, 3, 1, 2))
    re_s = jnp.transpose(hr_s.reshape(N_GROUPS, STATE_N, dec_b), (2, 0, 1))[None].astype(state_ssm_re.dtype)
    im_s = jnp.transpose(hi_s.reshape(N_GROUPS, STATE_N, dec_b), (2, 0, 1))[None].astype(state_ssm_im.dtype)

    return (y_prompt, y_sample, re_p, im_p, k_p, v_p, re_s, im_s, k_s, v_s)
```

```python
import functools

import jax
import jax.numpy as jnp
import numpy as np
from jax import lax
from jax.experimental import pallas as pl
from jax.experimental.pallas import tpu as pltpu

F32 = jnp.float32
BF16 = jnp.bfloat16

D_MODEL = 1024
GROUP_SIZE = 16
N_GROUPS = D_MODEL // GROUP_SIZE
STATE_N = 64
D_STATE = N_GROUPS * STATE_N
HEAD_DIM = 64
N_HEADS = D_MODEL // HEAD_DIM
N_KV_HEADS = 4
Q_PER_KV = N_HEADS // N_KV_HEADS
D_KV = N_KV_HEADS * HEAD_DIM
WINDOW = 128
D_FF = 4 * D_MODEL
EPS = 1e-6
MASKED = -1e30

LANES = 128
GROUPS_PER_BLOCK = LANES // GROUP_SIZE
N_LANE_BLOCKS = D_MODEL // LANES
STATES_PER_BLOCK = GROUPS_PER_BLOCK * STATE_N
SUB_BATCH = 16
SCAN_LANES = 256
SAMPLE_GROUP = 8
VMEM_LIMIT = 56 * 1024 * 1024
CAST_BLOCK_BYTES = 4 * 1024 * 1024


def _resident(a):
    return pl.BlockSpec(a.shape, lambda *_: (0,) * a.ndim, pipeline_mode=pl.Buffered(1))


def _resident_layer(a, layer):
    return pl.BlockSpec((None,) + a.shape[1:], lambda *_: (layer,) + (0,) * (a.ndim - 1),
                        pipeline_mode=pl.Buffered(1))


def _rms(x, g):
    return x * lax.rsqrt(jnp.mean(x * x, axis=-1, keepdims=True) + EPS) * g


def _head_mean_sq(x, avg):
    sq = x * x
    hi = sq.astype(BF16)
    lo = (sq - hi.astype(F32)).astype(BF16)
    return (jnp.dot(hi, avg, preferred_element_type=F32)
            + jnp.dot(lo, avg, preferred_element_type=F32))


def _cast_kernel(x_ref, o_ref):
    o_ref[...] = x_ref[...].astype(o_ref.dtype)


def _to_bf16(w):
    n, r, c = w.shape
    br = min(r, CAST_BLOCK_BYTES // (4 * c))
    spec = pl.BlockSpec((1, br, c), lambda i, j: (i, j, 0))
    return pl.pallas_call(
        _cast_kernel, grid=(n, r // br), in_specs=[spec], out_specs=spec,
        out_shape=jax.ShapeDtypeStruct(w.shape, BF16),
        compiler_params=pltpu.CompilerParams(dimension_semantics=("parallel", "parallel")),
        name="to_bf16",
    )(w)


def _discretize_kernel(a_re_ref, a_im_ref, log_dt_ref, b_re_ref, b_im_ref,
                       ab_re_ref, ab_im_ref, bb_re_ref, bb_im_ref):
    dt = jnp.exp(log_dt_ref[...])
    a_re = a_re_ref[...]
    a_im = a_im_ref[...]
    mag = jnp.exp(dt * a_re)
    ab_re = mag * jnp.cos(dt * a_im)
    ab_im = mag * jnp.sin(dt * a_im)
    den = a_re * a_re + a_im * a_im
    nr = ab_re - 1.0
    ni = ab_im
    f_re = (nr * a_re + ni * a_im) / den
    f_im = (ni * a_re - nr * a_im) / den
    ab_re_ref[...] = ab_re
    ab_im_ref[...] = ab_im
    b_re = b_re_ref[...]
    b_im = b_im_ref[...]
    fr = f_re[:, None, :]
    fi = f_im[:, None, :]
    bb_re_ref[...] = fr * b_re - fi * b_im
    bb_im_ref[...] = fr * b_im + fi * b_re


def _s5_discretize(a_re, a_im, log_dt, b_re, b_im):
    g, n = a_re.shape
    c = b_re.shape[-1]
    out = pl.pallas_call(
        _discretize_kernel,
        out_shape=(jax.ShapeDtypeStruct((g, n), F32), jax.ShapeDtypeStruct((g, n), F32),
                   jax.ShapeDtypeStruct((g, c, n), F32), jax.ShapeDtypeStruct((g, c, n), F32)),
        name="s5_discretize",
    )(a_re, a_im, log_dt.reshape(g, 1), jnp.swapaxes(b_re, 1, 2), jnp.swapaxes(b_im, 1, 2))
    return out


def _block_diag_weights(bb_re, bb_im, c_re, c_im):
    nb, gb = N_LANE_BLOCKS, GROUPS_PER_BLOCK
    eye = jnp.eye(gb, dtype=F32)
    bb = jnp.stack([bb_re, bb_im], axis=0).reshape(2, nb, gb, GROUP_SIZE, STATE_N)
    b_blk = jnp.einsum("rjgcn,gh->jgcrhn", bb, eye).reshape(nb, LANES, 2 * STATES_PER_BLOCK)
    cc = jnp.stack([c_re, -c_im], axis=0).reshape(2, nb, gb, GROUP_SIZE, STATE_N)
    c_blk = jnp.einsum("rjgcn,gh->jrgnhc", cc, eye).reshape(nb, 2 * STATES_PER_BLOCK, LANES)
    return b_blk.astype(BF16), c_blk.astype(BF16)


def _s5_layer_kernel(x_hbm, h0r_ref, h0i_ref, g_ref, d_ref, abr_ref, abi_ref, bblk_ref, cblk_ref,
                     wglu_ref, o_hbm, hro_ref, hio_ref, xbuf, obuf, in_sem, out_sem, hr_sc, hi_sc,
                     bur, bui, xsr, xsi, hn_sc, *, state_lanes_batch):
    _, tc, bb, d = xbuf.shape
    rows = tc * bb
    spb = STATES_PER_BLOCK
    step_idx = pl.program_id(0)
    n_steps = pl.num_programs(0)
    slot = step_idx % 2

    def in_copy(step, sl, b):
        return pltpu.make_async_copy(x_hbm.at[b, pl.ds(step * tc, tc), :], xbuf.at[sl, :, b, :],
                                     in_sem.at[sl])

    def out_copy(step, sl, b):
        return pltpu.make_async_copy(obuf.at[sl, :, b, :], o_hbm.at[b, pl.ds(step * tc, tc), :],
                                     out_sem.at[sl])

    @pl.when(step_idx == 0)
    def _():
        for b in range(bb):
            in_copy(0, 0, b).start()
        if state_lanes_batch:
            hr_sc[...] = h0r_ref[...].T
            hi_sc[...] = h0i_ref[...].T
        else:
            hr_sc[...] = h0r_ref[...]
            hi_sc[...] = h0i_ref[...]

    @pl.when(step_idx + 1 < n_steps)
    def _():
        for b in range(bb):
            in_copy(step_idx + 1, 1 - slot, b).start()

    for b in range(bb):
        in_copy(step_idx, slot, b).wait()

    hn = _rms(xbuf[slot].reshape(rows, d), g_ref[...])
    hn_sc[...] = hn
    hnb = hn.astype(BF16)

    def input_map(j):
        bu = jnp.dot(hnb[:, j * LANES:(j + 1) * LANES], bblk_ref[j], preferred_element_type=F32)
        bur[j] = bu[:, :spb]
        bui[j] = bu[:, spb:]

    def recurrence(j):
        for half in range(spb // SCAN_LANES):
            cs = slice(half * SCAN_LANES, (half + 1) * SCAN_LANES)
            gs = slice(j * spb + half * SCAN_LANES, j * spb + (half + 1) * SCAN_LANES)
            ar = jnp.broadcast_to(abr_ref[:, gs], (SUB_BATCH, SCAN_LANES))
            ai = jnp.broadcast_to(abi_ref[:, gs], (SUB_BATCH, SCAN_LANES))
            for sb in range(bb // SUB_BATCH):
                seqs = slice(sb * SUB_BATCH, (sb + 1) * SUB_BATCH)
                h_re = hr_sc[seqs, gs]
                h_im = hi_sc[seqs, gs]
                for t in range(tc):
                    rs = slice(t * bb + sb * SUB_BATCH, t * bb + (sb + 1) * SUB_BATCH)
                    n_re = ar * h_re - ai * h_im + bur[j, rs, cs]
                    n_im = ar * h_im + ai * h_re + bui[j, rs, cs]
                    xsr[j, rs, cs] = n_re.astype(BF16)
                    xsi[j, rs, cs] = n_im.astype(BF16)
                    h_re, h_im = n_re, n_im
                hr_sc[seqs, gs] = h_re
                hi_sc[seqs, gs] = h_im

    def output_map(j):
        ls = slice(j * LANES, (j + 1) * LANES)
        y = (jnp.dot(xsr[j], cblk_ref[j, :spb, :], preferred_element_type=F32)
             + jnp.dot(xsi[j], cblk_ref[j, spb:, :], preferred_element_type=F32))
        y = y + d_ref[:, ls] * hn_sc[:, ls]
        return jax.nn.gelu(y).astype(BF16)

    gelu_cols = []
    input_map(0)
    for j in range(N_LANE_BLOCKS):
        if j + 1 < N_LANE_BLOCKS:
            input_map(j + 1)
        recurrence(j)
        if j > 0:
            gelu_cols.append(output_map(j - 1))
    gelu_cols.append(output_map(N_LANE_BLOCKS - 1))

    z = jnp.dot(jnp.concatenate(gelu_cols, axis=1), wglu_ref[...], preferred_element_type=F32)
    mix = z[:, :d] * jax.nn.sigmoid(z[:, d:])

    @pl.when(step_idx >= 2)
    def _():
        for b in range(bb):
            out_copy(step_idx - 2, slot, b).wait()

    obuf[slot] = (xbuf[slot].reshape(rows, d) + mix).reshape(tc, bb, d)
    for b in range(bb):
        out_copy(step_idx, slot, b).start()

    @pl.when(step_idx == n_steps - 1)
    def _():
        @pl.when(n_steps > 1)
        def _():
            for b in range(bb):
                out_copy(step_idx - 1, 1 - slot, b).wait()
        for b in range(bb):
            out_copy(step_idx, slot, b).wait()
        if state_lanes_batch:
            hro_ref[...] = hr_sc[...].T
            hio_ref[...] = hi_sc[...].T
        else:
            hro_ref[...] = hr_sc[...]
            hio_ref[...] = hi_sc[...]


def _s5_layer(x, h0r, h0i, g_mix, d_skip, ab_re, ab_im, b_blk, c_blk, w_glu, *, tc, state_lanes_batch):
    bsz, length, d = x.shape
    rows = tc * bsz
    weights = (g_mix, d_skip, ab_re, ab_im, b_blk, c_blk, w_glu)
    state_shape = jax.ShapeDtypeStruct(h0r.shape, F32)
    block_shape = (N_LANE_BLOCKS, rows, STATES_PER_BLOCK)
    return pl.pallas_call(
        functools.partial(_s5_layer_kernel, state_lanes_batch=state_lanes_batch),
        grid=(length // tc,),
        in_specs=[pl.BlockSpec(memory_space=pl.ANY), _resident(h0r), _resident(h0i)]
        + [_resident(w) for w in weights],
        out_specs=(pl.BlockSpec(memory_space=pl.ANY), pl.BlockSpec(h0r.shape, lambda i: (0, 0)),
                   pl.BlockSpec(h0i.shape, lambda i: (0, 0))),
        out_shape=(jax.ShapeDtypeStruct((bsz, length, d), F32), state_shape, state_shape),
        scratch_shapes=[pltpu.VMEM((2, tc, bsz, d), F32), pltpu.VMEM((2, tc, bsz, d), F32),
                        pltpu.SemaphoreType.DMA((2,)), pltpu.SemaphoreType.DMA((2,)),
                        pltpu.VMEM((bsz, D_STATE), F32), pltpu.VMEM((bsz, D_STATE), F32),
                        pltpu.VMEM(block_shape, F32), pltpu.VMEM(block_shape, F32),
                        pltpu.VMEM(block_shape, BF16), pltpu.VMEM(block_shape, BF16),
                        pltpu.VMEM((rows, d), F32)],
        compiler_params=pltpu.CompilerParams(
            dimension_semantics=("arbitrary",), vmem_limit_bytes=VMEM_LIMIT),
        name="s5_layer",
    )(x, h0r, h0i, *weights)


def _mlp(x, g, w_up_ref, w_down_ref):
    hn = _rms(x, g).astype(BF16)
    acc = x
    n_chunks = w_up_ref.shape[1] // D_MODEL
    for c in range(n_chunks):
        cs = slice(c * D_MODEL, (c + 1) * D_MODEL)
        u = jnp.dot(hn, w_up_ref[:, cs], preferred_element_type=F32)
        u = jnp.square(jnp.maximum(u, 0.0)).astype(BF16)
        acc = acc + jnp.dot(u, w_down_ref[cs, :], preferred_element_type=F32)
    return acc


def _mlp_proj_kernel(x_ref, gffn_ref, wup_ref, wdown_ref, gkv_ref, wkv_ref, gk_ref, gq1_ref, wq_ref,
                     gq_ref, avg_ref, x2_ref, q_ref, kd_ref, vd_ref, k_ref, v_ref):
    x2 = _mlp(x_ref[...], gffn_ref[...], wup_ref, wdown_ref)
    x2_ref[...] = x2
    normed = x2 * lax.rsqrt(jnp.mean(x2 * x2, axis=-1, keepdims=True) + EPS)
    avg = avg_ref[...]

    kv = jnp.dot((normed * gkv_ref[...]).astype(BF16), wkv_ref[...], preferred_element_type=F32)
    k = kv[:, :D_KV]
    k_ref[...] = k * lax.rsqrt(_head_mean_sq(k, avg) + EPS) * gk_ref[...]
    v_ref[...] = kv[:, D_KV:2 * D_KV]
    for c in range(2):
        cs = slice(2 * D_KV + c * D_KV, 2 * D_KV + (c + 1) * D_KV)
        kd = kv[:, cs]
        kd_ref[:, c * D_KV:(c + 1) * D_KV] = (
            kd * lax.rsqrt(_head_mean_sq(kd, avg) + EPS) * gk_ref[...]).astype(BF16)
    vd_ref[...] = kv[:, 4 * D_KV:].astype(BF16)

    hq = (normed * gq1_ref[...]).astype(BF16)
    for c in range(D_MODEL // D_KV):
        cs = slice(c * D_KV, (c + 1) * D_KV)
        q = jnp.dot(hq, wq_ref[:, cs], preferred_element_type=F32)
        q = q * lax.rsqrt(_head_mean_sq(q, avg) + EPS) * gq_ref[...]
        q_ref[:, cs] = (q * (HEAD_DIM ** -0.5)).astype(BF16)


def _mlp_proj(x2d, tm, g_ffn, w_up, w_down, g_kv, w_kv_ext, g_k, g_q1, w_q, g_q, avg):
    rows = x2d.shape[0]
    row = lambda width: pl.BlockSpec((tm, width), lambda i: (i, 0))
    tail = (g_kv, w_kv_ext, g_k, g_q1, w_q, g_q, avg)
    return pl.pallas_call(
        _mlp_proj_kernel,
        grid=(rows // tm,),
        in_specs=[row(D_MODEL), _resident(g_ffn), _resident_layer(w_up, 0), _resident_layer(w_down, 0)]
        + [_resident(w) for w in tail],
        out_specs=(row(D_MODEL), row(D_MODEL), row(2 * D_KV), row(2 * D_KV), row(D_KV), row(D_KV)),
        out_shape=(jax.ShapeDtypeStruct((rows, D_MODEL), F32),
                   jax.ShapeDtypeStruct((rows, D_MODEL), BF16),
                   jax.ShapeDtypeStruct((rows, 2 * D_KV), BF16),
                   jax.ShapeDtypeStruct((rows, 2 * D_KV), BF16),
                   jax.ShapeDtypeStruct((rows, D_KV), F32),
                   jax.ShapeDtypeStruct((rows, D_KV), F32)),
        compiler_params=pltpu.CompilerParams(
            dimension_semantics=("parallel",), vmem_limit_bytes=VMEM_LIMIT),
        name="mlp_proj",
    )(x2d, g_ffn, w_up, w_down, *tail)


def _prompt_attn_kernel(q_ref, kd_ref, vd_ref, kdp_ref, vdp_ref, x_ref, bias_ref, sink_ref, lmask_ref,
                        wo_ref, gffn_ref, wup_ref, wdown_ref, y_ref, o_sc):
    tq = q_ref.shape[0]
    first = (pl.program_id(1) == 0).astype(jnp.int32)
    m_lo = lmask_ref[0:1, :]
    m_hi = lmask_ref[1:2, :]
    ones_lo = jnp.broadcast_to(m_lo, (2 * WINDOW, LANES))
    ones_hi = jnp.broadcast_to(m_hi, (2 * WINDOW, LANES))
    nt = (((1,), (1,)), ((), ()))

    for r in range(tq // WINDOW):
        rows = slice(r * WINDOW, (r + 1) * WINDOW)
        sel = first if r == 0 else 0
        for kh in range(N_KV_HEADS):
            ls = slice(kh * LANES, (kh + 1) * LANES)
            if r == 0:
                kwin = jnp.concatenate([kdp_ref[:, ls], kd_ref[0:WINDOW, ls]], axis=0)
                vwin = jnp.concatenate([vdp_ref[:, ls], vd_ref[0:WINDOW, ls]], axis=0)
            else:
                kwin = kd_ref[(r - 1) * WINDOW:(r + 1) * WINDOW, ls]
                vwin = vd_ref[(r - 1) * WINDOW:(r + 1) * WINDOW, ls]
            k_lo = kwin * m_lo
            k_hi = kwin * m_hi
            v_lo = jnp.concatenate([vwin * m_lo, ones_lo], axis=1)
            v_hi = jnp.concatenate([vwin * m_hi, ones_hi], axis=1)
            for pair in range(Q_PER_KV // 2):
                blk = kh * (Q_PER_KV // 2) + pair
                h0 = 2 * blk
                qp = q_ref[rows, blk * LANES:(blk + 1) * LANES]
                s0 = lax.dot_general(qp, k_lo, nt, preferred_element_type=F32) + bias_ref[sel, h0]
                s1 = lax.dot_general(qp, k_hi, nt, preferred_element_type=F32) + bias_ref[sel, h0 + 1]
                m0 = jnp.maximum(jnp.max(s0, axis=-1, keepdims=True), sink_ref[:, h0:h0 + 1])
                m1 = jnp.maximum(jnp.max(s1, axis=-1, keepdims=True), sink_ref[:, h0 + 1:h0 + 2])
                p0 = jnp.exp(s0 - m0).astype(BF16)
                p1 = jnp.exp(s1 - m1).astype(BF16)
                ext = (jnp.dot(p0, v_lo, preferred_element_type=F32)
                       + jnp.dot(p1, v_hi, preferred_element_type=F32))
                e0 = jnp.exp(sink_ref[:, h0:h0 + 1] - m0)
                e1 = jnp.exp(sink_ref[:, h0 + 1:h0 + 2] - m1)
                lane = lax.broadcasted_iota(jnp.int32, (WINDOW, LANES), 1)
                den = ext[:, LANES:] + jnp.where(lane < HEAD_DIM, e0, e1)
                o_sc[rows, blk * LANES:(blk + 1) * LANES] = (ext[:, :LANES] / den).astype(BF16)

    x3 = x_ref[...] + jnp.dot(o_sc[...], wo_ref[...], preferred_element_type=F32)
    y_ref[...] = _mlp(x3, gffn_ref[...], wup_ref, wdown_ref)


def _prompt_attention(q, kd, vd, x2, bias, sinks, lmask, w_o, g_ffn, w_up, w_down, *, bsz, length, tq):
    nt = length // tq
    per_win = tq // WINDOW
    row = lambda width: pl.BlockSpec((tq, width), lambda b, i: (b * nt + i, 0))
    prev = pl.BlockSpec(
        (WINDOW, 2 * D_KV),
        lambda b, i: (jnp.maximum((b * nt + i) * per_win - 1, 0), 0))
    return pl.pallas_call(
        _prompt_attn_kernel,
        grid=(bsz, nt),
        in_specs=[row(D_MODEL), row(2 * D_KV), row(2 * D_KV), prev, prev, row(D_MODEL),
                  _resident(bias), _resident(sinks), _resident(lmask), _resident(w_o), _resident(g_ffn),
                  _resident_layer(w_up, 1), _resident_layer(w_down, 1)],
        out_specs=row(D_MODEL),
        out_shape=jax.ShapeDtypeStruct((bsz * length, D_MODEL), F32),
        scratch_shapes=[pltpu.VMEM((tq, D_MODEL), BF16)],
        compiler_params=pltpu.CompilerParams(
            dimension_semantics=("parallel", "parallel"), vmem_limit_bytes=VMEM_LIMIT),
        name="prompt_attention",
    )(q, kd, vd, kd, vd, x2, bias, sinks, lmask, w_o, g_ffn, w_up, w_down)


def _alibi_slopes():
    return (2.0 ** (-8.0 * np.arange(1, N_HEADS + 1, dtype=np.float32) / N_HEADS)).astype(np.float32)


def _prompt_bias():
    qi = np.arange(WINDOW)[:, None]
    kj = np.arange(2 * WINDOW)[None, :]
    dist = qi - kj + WINDOW
    valid = (dist >= 0) & (dist < WINDOW)
    valid = np.stack([valid, valid & (kj >= WINDOW)], axis=0)[:, None]
    pen = -_alibi_slopes()[:, None, None] * dist.astype(np.float32)[None]
    return np.where(valid, pen[None], np.float32(MASKED)).astype(np.float32)


def _sample_attn_kernel(q_ref, kc_ref, vc_ref, kn_ref, vn_ref, bias_c_ref, bias_n_ref, sink_ref,
                        o_ref, kbuf_ref, vbuf_ref):
    g = kc_ref.shape[0]
    t_new = kn_ref.shape[0] // g
    nt = (((1,), (1,)), ((), ()))
    pad = jnp.zeros((LANES - g * t_new, D_KV), F32)
    kn_t = jnp.concatenate([kn_ref[...], pad], axis=0).T
    vn_t = jnp.concatenate([vn_ref[...], pad], axis=0).T
    lane = lax.broadcasted_iota(jnp.int32, (HEAD_DIM, WINDOW), 1)
    is_new = lane >= WINDOW - t_new

    for kh in range(N_KV_HEADS):
        hs = slice(kh * HEAD_DIM, (kh + 1) * HEAD_DIM)
        q = q_ref[kh]
        k_t = jnp.concatenate([kc_ref[s, kh] for s in range(g)], axis=1)
        v_t = jnp.concatenate([vc_ref[s, kh] for s in range(g)], axis=1)
        s_c = jnp.dot(q, k_t.astype(BF16), preferred_element_type=F32) + bias_c_ref[kh]
        s_n = jnp.dot(q, kn_t[hs].astype(BF16), preferred_element_type=F32) + bias_n_ref[kh]
        sink = sink_ref[kh]
        m = jnp.maximum(jnp.maximum(jnp.max(s_c, axis=-1, keepdims=True),
                                    jnp.max(s_n, axis=-1, keepdims=True)), sink)
        p_c = jnp.exp(s_c - m)
        p_n = jnp.exp(s_n - m)
        den = (jnp.sum(p_c, axis=-1, keepdims=True) + jnp.sum(p_n, axis=-1, keepdims=True)
               + jnp.exp(sink - m))
        o = (lax.dot_general(p_c.astype(BF16), v_t.astype(BF16), nt, preferred_element_type=F32)
             + lax.dot_general(p_n.astype(BF16), vn_t[hs].astype(BF16), nt, preferred_element_type=F32))
        o_ref[kh] = o / den

        for s in range(g):
            shift_new = (WINDOW - t_new - s * t_new) % WINDOW
            k_new = pltpu.roll(kn_t[hs], shift_new, 1)
            v_new = pltpu.roll(vn_t[hs], shift_new, 1)
            kbuf_ref[s, kh] = jnp.where(is_new, k_new, pltpu.roll(kc_ref[s, kh], WINDOW - t_new, 1))
            vbuf_ref[s, kh] = jnp.where(is_new, v_new, pltpu.roll(vc_ref[s, kh], WINDOW - t_new, 1))


def _sample_attention(q_heads, cache_kt, cache_vt, k_new, v_new, bias_c, bias_n, sink_rows, *, t_new):
    g = SAMPLE_GROUP
    bsz = cache_kt.shape[0]
    rows_q = g * Q_PER_KV * t_new
    cache_spec = pl.BlockSpec((g, N_KV_HEADS, HEAD_DIM, WINDOW), lambda i: (i, 0, 0, 0))
    q_spec = pl.BlockSpec((N_KV_HEADS, rows_q, HEAD_DIM), lambda i: (0, i, 0))
    new_spec = pl.BlockSpec((g * t_new, D_KV), lambda i: (i, 0))
    return pl.pallas_call(
        _sample_attn_kernel,
        grid=(bsz // g,),
        in_specs=[q_spec, cache_spec, cache_spec, new_spec, new_spec,
                  _resident(bias_c), _resident(bias_n), _resident(sink_rows)],
        out_specs=(q_spec, cache_spec, cache_spec),
        out_shape=(jax.ShapeDtypeStruct(q_heads.shape, F32),
                   jax.ShapeDtypeStruct(cache_kt.shape, cache_kt.dtype),
                   jax.ShapeDtypeStruct(cache_vt.shape, cache_vt.dtype)),
        compiler_params=pltpu.CompilerParams(
            dimension_semantics=("parallel",), vmem_limit_bytes=VMEM_LIMIT),
        name="sample_attention",
    )(q_heads, cache_kt, cache_vt, k_new, v_new, bias_c, bias_n, sink_rows)


def _sample_bias(t_new):
    g = SAMPLE_GROUP
    slope = _alibi_slopes().reshape(N_KV_HEADS, 1, Q_PER_KV, 1, 1, 1)
    seq_q = np.arange(g).reshape(1, g, 1, 1, 1, 1)
    t = np.arange(t_new).reshape(1, 1, 1, t_new, 1, 1)
    seq_k = np.arange(g).reshape(1, 1, 1, 1, g, 1)
    j = np.arange(WINDOW).reshape(1, 1, 1, 1, 1, WINDOW)
    dist = t + WINDOW - j
    ok = (seq_q == seq_k) & (dist >= 0) & (dist < WINDOW)
    rows = g * Q_PER_KV * t_new
    bias_c = np.where(ok, -slope * dist.astype(np.float32), np.float32(MASKED))
    bias_c = bias_c.reshape(N_KV_HEADS, rows, g * WINDOW).astype(np.float32)
    tn = np.arange(t_new).reshape(1, 1, 1, 1, 1, t_new)
    dist_n = t - tn
    ok_n = (seq_q == seq_k) & (dist_n >= 0) & (dist_n < WINDOW)
    bias_n = np.where(ok_n, -slope * dist_n.astype(np.float32), np.float32(MASKED))
    bias_n = bias_n.reshape(N_KV_HEADS, rows, g * t_new)
    bias_n = np.pad(bias_n, ((0, 0), (0, 0), (0, LANES - g * t_new)), constant_values=MASKED)
    return bias_c, bias_n.astype(np.float32)


def _out_mlp_kernel(o_ref, x_ref, wo_ref, gffn_ref, wup_ref, wdown_ref, y_ref):
    x3 = x_ref[...] + jnp.dot(o_ref[...], wo_ref[...], preferred_element_type=F32)
    y_ref[...] = _mlp(x3, gffn_ref[...], wup_ref, wdown_ref)


def _out_mlp(o, x2, w_o, g_ffn, w_up, w_down, tm):
    rows = x2.shape[0]
    row = pl.BlockSpec((tm, D_MODEL), lambda i: (i, 0))
    return pl.pallas_call(
        _out_mlp_kernel,
        grid=(rows // tm,),
        in_specs=[row, row, _resident(w_o), _resident(g_ffn), _resident_layer(w_up, 1),
                  _resident_layer(w_down, 1)],
        out_specs=row,
        out_shape=jax.ShapeDtypeStruct((rows, D_MODEL), F32),
        compiler_params=pltpu.CompilerParams(
            dimension_semantics=("parallel",), vmem_limit_bytes=VMEM_LIMIT),
        name="out_mlp",
    )(o, x2, w_o, g_ffn, w_up, w_down)


def _repeat_heads(w):
    d = w.shape[0]
    return jnp.tile(w.reshape(d, N_KV_HEADS, 1, HEAD_DIM), (1, 1, 2, 1)).reshape(d, 2 * D_KV)


def kernel(x_prompt, x_sample, state_ssm_re, state_ssm_im, cache_k, cache_v, norm_mix, norm_ffn,
           ssm_a_re, ssm_a_im, ssm_log_dt, ssm_b_re, ssm_b_im, ssm_c_re, ssm_c_im, ssm_d, ssm_w_glu,
           kv_norm, w_kv, k_norm, w_q, q_norm, attn_sinks, w_o, w_up, w_down):
    bsz, length, d = x_prompt.shape
    dec_b, dec_t, _ = x_sample.shape

    ab_re, ab_im, bb_re, bb_im = _s5_discretize(ssm_a_re[0], ssm_a_im[0], ssm_log_dt[0],
                                                ssm_b_re[0], ssm_b_im[0])
    b_blk, c_blk = _block_diag_weights(bb_re, bb_im, ssm_c_re[0], ssm_c_im[0])
    ab_re = ab_re.reshape(1, D_STATE)
    ab_im = ab_im.reshape(1, D_STATE)
    row = lambda v: v.reshape(1, -1).astype(F32)
    w_glu = ssm_w_glu[0].astype(BF16)
    w_up_b = _to_bf16(w_up)
    w_down_b = _to_bf16(w_down)
    w_k, w_v = w_kv[:, :D_KV], w_kv[:, D_KV:]
    w_kv_ext = jnp.concatenate([w_k, w_v, _repeat_heads(w_k), _repeat_heads(w_v)], axis=1).astype(BF16)
    w_q0 = w_q[0].astype(BF16)
    w_o0 = w_o[0].astype(BF16)
    g_k = row(jnp.tile(k_norm, D_KV // HEAD_DIM))
    g_q = row(jnp.tile(q_norm[0], D_KV // HEAD_DIM))
    avg = jnp.asarray(np.kron(np.eye(D_KV // HEAD_DIM, dtype=np.float32),
                              np.full((HEAD_DIM, HEAD_DIM), 1.0 / HEAD_DIM, np.float32)), BF16)
    lane = np.arange(LANES)
    lmask = jnp.asarray(np.stack([lane < HEAD_DIM, lane >= HEAD_DIM]), BF16)
    sinks = row(attn_sinks[0])

    def layer0(x, h0r, h0i, tc, state_lanes_batch):
        return _s5_layer(x, h0r, h0i, row(norm_mix[0]), row(ssm_d[0]), ab_re, ab_im, b_blk, c_blk,
                         w_glu, tc=tc, state_lanes_batch=state_lanes_batch)

    def mlp_proj(x2d, tm):
        return _mlp_proj(x2d, tm, row(norm_ffn[0]), w_up_b, w_down_b, row(kv_norm), w_kv_ext, g_k,
                         row(norm_mix[1]), w_q0, g_q, avg)

    zeros = jnp.zeros((bsz, D_STATE), F32)
    x1, hr_p, hi_p = layer0(x_prompt, zeros, zeros, 16, False)
    x2, q, kd, vd, k, v = mlp_proj(x1.reshape(bsz * length, d), 512)
    y_prompt = _prompt_attention(q, kd, vd, x2, jnp.asarray(_prompt_bias()), sinks, lmask, w_o0,
                                 row(norm_ffn[1]), w_up_b, w_down_b, bsz=bsz, length=length, tq=512)
    y_prompt = y_prompt.reshape(bsz, length, d)
    k_p = k.reshape(bsz, length, D_KV)[:, -WINDOW:].reshape(bsz, WINDOW, N_KV_HEADS, HEAD_DIM)
    v_p = v.reshape(bsz, length, D_KV)[:, -WINDOW:].reshape(bsz, WINDOW, N_KV_HEADS, HEAD_DIM)
    state_shape = (1, bsz, N_GROUPS, STATE_N)
    re_p = hr_p.reshape(state_shape).astype(state_ssm_re.dtype)
    im_p = hi_p.reshape(state_shape).astype(state_ssm_im.dtype)

    h0r_t = jnp.transpose(state_ssm_re[0], (1, 2, 0)).reshape(D_STATE, dec_b).astype(F32)
    h0i_t = jnp.transpose(state_ssm_im[0], (1, 2, 0)).reshape(D_STATE, dec_b).astype(F32)
    x1s, hr_s, hi_s = layer0(x_sample, h0r_t, h0i_t, dec_t, True)
    x2s, qs, _, _, ks, vs = mlp_proj(x1s.reshape(dec_b * dec_t, d), dec_b * dec_t)
    q_heads = qs.reshape(dec_b, dec_t, N_KV_HEADS, Q_PER_KV, HEAD_DIM).transpose(2, 0, 3, 1, 4)
    q_heads = q_heads.reshape(N_KV_HEADS, dec_b * Q_PER_KV * dec_t, HEAD_DIM)
    bias_c, bias_n = _sample_bias(dec_t)
    sink_rows = jnp.broadcast_to(
        attn_sinks[0].astype(F32).reshape(N_KV_HEADS, 1, Q_PER_KV, 1),
        (N_KV_HEADS, SAMPLE_GROUP, Q_PER_KV, dec_t)).reshape(N_KV_HEADS, -1, 1)
    o_heads, k_s_t, v_s_t = _sample_attention(
        q_heads, jnp.transpose(cache_k, (0, 2, 3, 1)), jnp.transpose(cache_v, (0, 2, 3, 1)),
        ks.astype(cache_k.dtype), vs.astype(cache_v.dtype), jnp.asarray(bias_c), jnp.asarray(bias_n),
        sink_rows, t_new=dec_t)
    o_s = o_heads.reshape(N_KV_HEADS, dec_b, Q_PER_KV, dec_t, HEAD_DIM).transpose(1, 3, 0, 2, 4)
    o_s = o_s.reshape(dec_b * dec_t, d).astype(BF16)
    y_sample = _out_mlp(o_s, x2s, w_o0, row(norm_ffn[1]), w_up_b, w_down_b, dec_b * dec_t)
    y_sample = y_sample.reshape(dec_b, dec_t, d)
    k_s = jnp.transpose(k_s_t, (0, 3, 1, 2))
    v_s = jnp.transpose(v_s_t, (0, 3, 1, 2))
    re_s = jnp.transpose(hr_s.reshape(N_GROUPS, STATE_N, dec_b), (2, 0, 1))[None].astype(state_ssm_re.dtype)
    im_s = jnp.transpose(hi_s.reshape(N_GROUPS, STATE_N, dec_b), (2, 0, 1))[None].astype(state_ssm_im.dtype)

    return (y_prompt, y_sample, re_p, im_p, k_p, v_p, re_s, im_s, k_s, v_s)
```

```python
import functools

import jax
import jax.numpy as jnp
import numpy as np
from jax import lax
from jax.experimental import pallas as pl
from jax.experimental.pallas import tpu as pltpu

F32 = jnp.float32
BF16 = jnp.bfloat16

D_MODEL = 1024
GROUP_SIZE = 16
N_GROUPS = D_MODEL // GROUP_SIZE
STATE_N = 64
D_STATE = N_GROUPS * STATE_N
HEAD_DIM = 64
N_HEADS = D_MODEL // HEAD_DIM
N_KV_HEADS = 4
Q_PER_KV = N_HEADS // N_KV_HEADS
D_KV = N_KV_HEADS * HEAD_DIM
WINDOW = 128
D_FF = 4 * D_MODEL
EPS = 1e-6
MASKED = -1e30

LANES = 128
GROUPS_PER_BLOCK = LANES // GROUP_SIZE
N_LANE_BLOCKS = D_MODEL // LANES
STATES_PER_BLOCK = GROUPS_PER_BLOCK * STATE_N
SUB_BATCH = 16
SCAN_LANES = 256
SAMPLE_GROUP = 8
MLP_PIECE = 256
VMEM_LIMIT = 56 * 1024 * 1024
CAST_BLOCK_BYTES = 4 * 1024 * 1024


def _resident(a):
    return pl.BlockSpec(a.shape, lambda *_: (0,) * a.ndim, pipeline_mode=pl.Buffered(1))


def _resident_layer(a, layer):
    return pl.BlockSpec((None,) + a.shape[1:], lambda *_: (layer,) + (0,) * (a.ndim - 1),
                        pipeline_mode=pl.Buffered(1))


def _rms(x, g):
    return x * lax.rsqrt(jnp.mean(x * x, axis=-1, keepdims=True) + EPS) * g


def _head_mean_sq(x, avg):
    sq = x * x
    hi = sq.astype(BF16)
    lo = (sq - hi.astype(F32)).astype(BF16)
    return (jnp.dot(hi, avg, preferred_element_type=F32)
            + jnp.dot(lo, avg, preferred_element_type=F32))


def _cast_kernel(x_ref, o_ref):
    o_ref[...] = x_ref[...].astype(o_ref.dtype)


def _to_bf16(w):
    n, r, c = w.shape
    br = min(r, CAST_BLOCK_BYTES // (4 * c))
    spec = pl.BlockSpec((1, br, c), lambda i, j: (i, j, 0))
    return pl.pallas_call(
        _cast_kernel, grid=(n, r // br), in_specs=[spec], out_specs=spec,
        out_shape=jax.ShapeDtypeStruct(w.shape, BF16),
        compiler_params=pltpu.CompilerParams(dimension_semantics=("parallel", "parallel")),
        name="to_bf16",
    )(w)


def _discretize_kernel(a_re_ref, a_im_ref, log_dt_ref, b_re_ref, b_im_ref,
                       ab_re_ref, ab_im_ref, bb_re_ref, bb_im_ref):
    dt = jnp.exp(log_dt_ref[...])
    a_re = a_re_ref[...]
    a_im = a_im_ref[...]
    mag = jnp.exp(dt * a_re)
    ab_re = mag * jnp.cos(dt * a_im)
    ab_im = mag * jnp.sin(dt * a_im)
    den = a_re * a_re + a_im * a_im
    nr = ab_re - 1.0
    ni = ab_im
    f_re = (nr * a_re + ni * a_im) / den
    f_im = (ni * a_re - nr * a_im) / den
    ab_re_ref[...] = ab_re
    ab_im_ref[...] = ab_im
    b_re = b_re_ref[...]
    b_im = b_im_ref[...]
    fr = f_re[:, None, :]
    fi = f_im[:, None, :]
    bb_re_ref[...] = fr * b_re - fi * b_im
    bb_im_ref[...] = fr * b_im + fi * b_re


def _s5_discretize(a_re, a_im, log_dt, b_re, b_im):
    g, n = a_re.shape
    c = b_re.shape[-1]
    out = pl.pallas_call(
        _discretize_kernel,
        out_shape=(jax.ShapeDtypeStruct((g, n), F32), jax.ShapeDtypeStruct((g, n), F32),
                   jax.ShapeDtypeStruct((g, c, n), F32), jax.ShapeDtypeStruct((g, c, n), F32)),
        name="s5_discretize",
    )(a_re, a_im, log_dt.reshape(g, 1), jnp.swapaxes(b_re, 1, 2), jnp.swapaxes(b_im, 1, 2))
    return out


def _block_diag_weights(bb_re, bb_im, c_re, c_im):
    nb, gb = N_LANE_BLOCKS, GROUPS_PER_BLOCK
    eye = jnp.eye(gb, dtype=F32)
    bb = jnp.stack([bb_re, bb_im], axis=0).reshape(2, nb, gb, GROUP_SIZE, STATE_N)
    b_blk = jnp.einsum("rjgcn,gh->jgcrhn", bb, eye).reshape(nb, LANES, 2 * STATES_PER_BLOCK)
    cc = jnp.stack([c_re, -c_im], axis=0).reshape(2, nb, gb, GROUP_SIZE, STATE_N)
    c_blk = jnp.einsum("rjgcn,gh->jrgnhc", cc, eye).reshape(nb, 2 * STATES_PER_BLOCK, LANES)
    return b_blk.astype(BF16), c_blk.astype(BF16)


def _s5_layer_kernel(x_hbm, h0r_ref, h0i_ref, g_ref, d_ref, abr_ref, abi_ref, bblk_ref, cblk_ref,
                     wglu_ref, o_hbm, hro_ref, hio_ref, xbuf, obuf, in_sem, out_sem, hr_sc, hi_sc,
                     bur, bui, xsr, xsi, hn_sc, *, state_lanes_batch):
    _, tc, bb, d = xbuf.shape
    rows = tc * bb
    spb = STATES_PER_BLOCK
    step_idx = pl.program_id(0)
    n_steps = pl.num_programs(0)
    slot = step_idx % 2

    def in_copy(step, sl, b):
        return pltpu.make_async_copy(x_hbm.at[b, pl.ds(step * tc, tc), :], xbuf.at[sl, :, b, :],
                                     in_sem.at[sl])

    def out_copy(step, sl, b):
        return pltpu.make_async_copy(obuf.at[sl, :, b, :], o_hbm.at[b, pl.ds(step * tc, tc), :],
                                     out_sem.at[sl])

    @pl.when(step_idx == 0)
    def _():
        for b in range(bb):
            in_copy(0, 0, b).start()
        if state_lanes_batch:
            hr_sc[...] = h0r_ref[...].T
            hi_sc[...] = h0i_ref[...].T
        else:
            hr_sc[...] = h0r_ref[...]
            hi_sc[...] = h0i_ref[...]

    @pl.when(step_idx + 1 < n_steps)
    def _():
        for b in range(bb):
            in_copy(step_idx + 1, 1 - slot, b).start()

    for b in range(bb):
        in_copy(step_idx, slot, b).wait()

    hn = _rms(xbuf[slot].reshape(rows, d), g_ref[...])
    hn_sc[...] = hn
    hnb = hn.astype(BF16)

    def input_map(j):
        bu = jnp.dot(hnb[:, j * LANES:(j + 1) * LANES], bblk_ref[j], preferred_element_type=F32)
        bur[j] = bu[:, :spb]
        bui[j] = bu[:, spb:]

    def recurrence(j):
        for half in range(spb // SCAN_LANES):
            cs = slice(half * SCAN_LANES, (half + 1) * SCAN_LANES)
            gs = slice(j * spb + half * SCAN_LANES, j * spb + (half + 1) * SCAN_LANES)
            ar = jnp.broadcast_to(abr_ref[:, gs], (SUB_BATCH, SCAN_LANES))
            ai = jnp.broadcast_to(abi_ref[:, gs], (SUB_BATCH, SCAN_LANES))
            for sb in range(bb // SUB_BATCH):
                seqs = slice(sb * SUB_BATCH, (sb + 1) * SUB_BATCH)
                h_re = hr_sc[seqs, gs]
                h_im = hi_sc[seqs, gs]
                for t in range(tc):
                    rs = slice(t * bb + sb * SUB_BATCH, t * bb + (sb + 1) * SUB_BATCH)
                    n_re = ar * h_re - ai * h_im + bur[j, rs, cs]
                    n_im = ar * h_im + ai * h_re + bui[j, rs, cs]
                    xsr[j, rs, cs] = n_re.astype(BF16)
                    xsi[j, rs, cs] = n_im.astype(BF16)
                    h_re, h_im = n_re, n_im
                hr_sc[seqs, gs] = h_re
                hi_sc[seqs, gs] = h_im

    def output_map(j):
        ls = slice(j * LANES, (j + 1) * LANES)
        y = (jnp.dot(xsr[j], cblk_ref[j, :spb, :], preferred_element_type=F32)
             + jnp.dot(xsi[j], cblk_ref[j, spb:, :], preferred_element_type=F32))
        y = y + d_ref[:, ls] * hn_sc[:, ls]
        return jax.nn.gelu(y).astype(BF16)

    gelu_cols = []
    input_map(0)
    for j in range(N_LANE_BLOCKS):
        if j + 1 < N_LANE_BLOCKS:
            input_map(j + 1)
        recurrence(j)
        if j > 0:
            gelu_cols.append(output_map(j - 1))
    gelu_cols.append(output_map(N_LANE_BLOCKS - 1))

    z = jnp.dot(jnp.concatenate(gelu_cols, axis=1), wglu_ref[...], preferred_element_type=F32)
    mix = z[:, :d] * jax.nn.sigmoid(z[:, d:])

    @pl.when(step_idx >= 2)
    def _():
        for b in range(bb):
            out_copy(step_idx - 2, slot, b).wait()

    obuf[slot] = (xbuf[slot].reshape(rows, d) + mix).reshape(tc, bb, d)
    for b in range(bb):
        out_copy(step_idx, slot, b).start()

    @pl.when(step_idx == n_steps - 1)
    def _():
        @pl.when(n_steps > 1)
        def _():
            for b in range(bb):
                out_copy(step_idx - 1, 1 - slot, b).wait()
        for b in range(bb):
            out_copy(step_idx, slot, b).wait()
        if state_lanes_batch:
            hro_ref[...] = hr_sc[...].T
            hio_ref[...] = hi_sc[...].T
        else:
            hro_ref[...] = hr_sc[...]
            hio_ref[...] = hi_sc[...]


def _s5_layer(x, h0r, h0i, g_mix, d_skip, ab_re, ab_im, b_blk, c_blk, w_glu, *, tc, state_lanes_batch):
    bsz, length, d = x.shape
    rows = tc * bsz
    weights = (g_mix, d_skip, ab_re, ab_im, b_blk, c_blk, w_glu)
    state_shape = jax.ShapeDtypeStruct(h0r.shape, F32)
    block_shape = (N_LANE_BLOCKS, rows, STATES_PER_BLOCK)
    return pl.pallas_call(
        functools.partial(_s5_layer_kernel, state_lanes_batch=state_lanes_batch),
        grid=(length // tc,),
        in_specs=[pl.BlockSpec(memory_space=pl.ANY), _resident(h0r), _resident(h0i)]
        + [_resident(w) for w in weights],
        out_specs=(pl.BlockSpec(memory_space=pl.ANY), pl.BlockSpec(h0r.shape, lambda i: (0, 0)),
                   pl.BlockSpec(h0i.shape, lambda i: (0, 0))),
        out_shape=(jax.ShapeDtypeStruct((bsz, length, d), F32), state_shape, state_shape),
        scratch_shapes=[pltpu.VMEM((2, tc, bsz, d), F32), pltpu.VMEM((2, tc, bsz, d), F32),
                        pltpu.SemaphoreType.DMA((2,)), pltpu.SemaphoreType.DMA((2,)),
                        pltpu.VMEM((bsz, D_STATE), F32), pltpu.VMEM((bsz, D_STATE), F32),
                        pltpu.VMEM(block_shape, F32), pltpu.VMEM(block_shape, F32),
                        pltpu.VMEM(block_shape, BF16), pltpu.VMEM(block_shape, BF16),
                        pltpu.VMEM((rows, d), F32)],
        compiler_params=pltpu.CompilerParams(
            dimension_semantics=("arbitrary",), vmem_limit_bytes=VMEM_LIMIT),
        name="s5_layer",
    )(x, h0r, h0i, *weights)


def _mlp_chunk(hn, w_up_ref, w_down_ref, c):
    cs = slice(c * D_MODEL, (c + 1) * D_MODEL)
    u = jnp.dot(hn, w_up_ref[:, cs], preferred_element_type=F32)
    u = jnp.square(jnp.maximum(u, 0.0)).astype(BF16)
    return jnp.dot(u, w_down_ref[cs, :], preferred_element_type=F32)


def _mlp(x, g, w_up_ref, w_down_ref):
    hn = _rms(x, g).astype(BF16)
    acc = x
    for c in range(w_up_ref.shape[1] // D_MODEL):
        acc = acc + _mlp_chunk(hn, w_up_ref, w_down_ref, c)
    return acc


def _mlp_proj_kernel(x_ref, gffn_ref, wup_ref, wdown_ref, gkv_ref, wkv_ref, gk_ref, gq1_ref, wq_ref,
                     gq_ref, avg_ref, x2_ref, q_ref, kd_ref, vd_ref, k_ref, v_ref):
    x2 = _mlp(x_ref[...], gffn_ref[...], wup_ref, wdown_ref)
    x2_ref[...] = x2
    normed = x2 * lax.rsqrt(jnp.mean(x2 * x2, axis=-1, keepdims=True) + EPS)
    avg = avg_ref[...]

    kv = jnp.dot((normed * gkv_ref[...]).astype(BF16), wkv_ref[...], preferred_element_type=F32)
    k = kv[:, :D_KV]
    k_ref[...] = k * lax.rsqrt(_head_mean_sq(k, avg) + EPS) * gk_ref[...]
    v_ref[...] = kv[:, D_KV:2 * D_KV]
    for c in range(2):
        cs = slice(2 * D_KV + c * D_KV, 2 * D_KV + (c + 1) * D_KV)
        kd = kv[:, cs]
        kd_ref[:, c * D_KV:(c + 1) * D_KV] = (
            kd * lax.rsqrt(_head_mean_sq(kd, avg) + EPS) * gk_ref[...]).astype(BF16)
    vd_ref[...] = kv[:, 4 * D_KV:].astype(BF16)

    hq = (normed * gq1_ref[...]).astype(BF16)
    for c in range(D_MODEL // D_KV):
        cs = slice(c * D_KV, (c + 1) * D_KV)
        q = jnp.dot(hq, wq_ref[:, cs], preferred_element_type=F32)
        q = q * lax.rsqrt(_head_mean_sq(q, avg) + EPS) * gq_ref[...]
        q_ref[:, cs] = (q * (HEAD_DIM ** -0.5)).astype(BF16)


def _mlp_proj(x2d, tm, g_ffn, w_up, w_down, g_kv, w_kv_ext, g_k, g_q1, w_q, g_q, avg):
    rows = x2d.shape[0]
    row = lambda width: pl.BlockSpec((tm, width), lambda i: (i, 0))
    tail = (g_kv, w_kv_ext, g_k, g_q1, w_q, g_q, avg)
    return pl.pallas_call(
        _mlp_proj_kernel,
        grid=(rows // tm,),
        in_specs=[row(D_MODEL), _resident(g_ffn), _resident_layer(w_up, 0), _resident_layer(w_down, 0)]
        + [_resident(w) for w in tail],
        out_specs=(row(D_MODEL), row(D_MODEL), row(2 * D_KV), row(2 * D_KV), row(D_KV), row(D_KV)),
        out_shape=(jax.ShapeDtypeStruct((rows, D_MODEL), F32),
                   jax.ShapeDtypeStruct((rows, D_MODEL), BF16),
                   jax.ShapeDtypeStruct((rows, 2 * D_KV), BF16),
                   jax.ShapeDtypeStruct((rows, 2 * D_KV), BF16),
                   jax.ShapeDtypeStruct((rows, D_KV), F32),
                   jax.ShapeDtypeStruct((rows, D_KV), F32)),
        compiler_params=pltpu.CompilerParams(
            dimension_semantics=("parallel",), vmem_limit_bytes=VMEM_LIMIT),
        name="mlp_proj",
    )(x2d, g_ffn, w_up, w_down, *tail)


def _prompt_attn_kernel(q_ref, kd_ref, vd_ref, kdp_ref, vdp_ref, x_ref, bias_ref, sink_ref, lmask_ref,
                        wo_ref, gffn_ref, wup_ref, wdown_ref, y_ref, o_sc, *, tiles_per_seq, n_tiles):
    tq = q_ref.shape[0]
    step = pl.program_id(0)
    slot = step % 2

    @pl.when(step == 0)
    def _():
        o_sc[1] = jnp.zeros(o_sc.shape[1:], o_sc.dtype)

    tile = jnp.minimum(step, n_tiles - 1)
    first = (tile % tiles_per_seq == 0).astype(jnp.int32)
    m_lo = lmask_ref[0:1, :]
    m_hi = lmask_ref[1:2, :]
    ones_lo = jnp.broadcast_to(m_lo, (2 * WINDOW, LANES))
    ones_hi = jnp.broadcast_to(m_hi, (2 * WINDOW, LANES))
    nt = (((1,), (1,)), ((), ()))
    lane = lax.broadcasted_iota(jnp.int32, (WINDOW, LANES), 1)
    pairs_per_kv = Q_PER_KV // 2
    windows = {}

    def window(r, kh):
        if (r, kh) not in windows:
            ls = slice(kh * LANES, (kh + 1) * LANES)
            if r == 0:
                kwin = jnp.concatenate([kdp_ref[:, ls], kd_ref[0:WINDOW, ls]], axis=0)
                vwin = jnp.concatenate([vdp_ref[:, ls], vd_ref[0:WINDOW, ls]], axis=0)
            else:
                kwin = kd_ref[(r - 1) * WINDOW:(r + 1) * WINDOW, ls]
                vwin = vd_ref[(r - 1) * WINDOW:(r + 1) * WINDOW, ls]
            windows[(r, kh)] = (kwin * m_lo, kwin * m_hi,
                                jnp.concatenate([vwin * m_lo, ones_lo], axis=1),
                                jnp.concatenate([vwin * m_hi, ones_hi], axis=1))
        return windows[(r, kh)]

    def scores(p):
        r, blk = divmod(p, N_LANE_BLOCKS)
        k_lo, k_hi, _, _ = window(r, blk // pairs_per_kv)
        sel = first if r == 0 else 0
        qp = q_ref[r * WINDOW:(r + 1) * WINDOW, blk * LANES:(blk + 1) * LANES]
        s0 = lax.dot_general(qp, k_lo, nt, preferred_element_type=F32) + bias_ref[sel, 2 * blk]
        s1 = lax.dot_general(qp, k_hi, nt, preferred_element_type=F32) + bias_ref[sel, 2 * blk + 1]
        return s0, s1

    def finish(p, s0, s1):
        r, blk = divmod(p, N_LANE_BLOCKS)
        _, _, v_lo, v_hi = window(r, blk // pairs_per_kv)
        h0 = 2 * blk
        m0 = jnp.maximum(jnp.max(s0, axis=-1, keepdims=True), sink_ref[:, h0:h0 + 1])
        m1 = jnp.maximum(jnp.max(s1, axis=-1, keepdims=True), sink_ref[:, h0 + 1:h0 + 2])
        p0 = jnp.exp(s0 - m0).astype(BF16)
        p1 = jnp.exp(s1 - m1).astype(BF16)
        ext = (jnp.dot(p0, v_lo, preferred_element_type=F32)
               + jnp.dot(p1, v_hi, preferred_element_type=F32))
        e0 = jnp.exp(sink_ref[:, h0:h0 + 1] - m0)
        e1 = jnp.exp(sink_ref[:, h0 + 1:h0 + 2] - m1)
        den = ext[:, LANES:] + jnp.where(lane < HEAD_DIM, e0, e1)
        o_sc[slot, r * WINDOW:(r + 1) * WINDOW, blk * LANES:(blk + 1) * LANES] = (
            (ext[:, :LANES] / den).astype(BF16))

    n_col = D_MODEL // MLP_PIECE
    n_chunks = wup_ref.shape[1] // D_MODEL
    o_prev = o_sc[1 - slot]
    state = {}

    def piece(k):
        if k < n_col:
            cs = slice(k * MLP_PIECE, (k + 1) * MLP_PIECE)
            state.setdefault("x3", []).append(
                x_ref[:, cs] + jnp.dot(o_prev, wo_ref[:, cs], preferred_element_type=F32))
            if k == n_col - 1:
                x3 = jnp.concatenate(state["x3"], axis=1)
                state["hn"] = _rms(x3, gffn_ref[...]).astype(BF16)
                state["acc"] = state["x3"]
            return
        c, j = divmod(k - n_col, 2 * n_col)
        if j < n_col:
            cs = slice(c * D_MODEL + j * MLP_PIECE, c * D_MODEL + (j + 1) * MLP_PIECE)
            u = jnp.dot(state["hn"], wup_ref[:, cs], preferred_element_type=F32)
            state.setdefault("u", []).append(jnp.square(jnp.maximum(u, 0.0)).astype(BF16))
            if j == n_col - 1:
                state["uc"] = jnp.concatenate(state.pop("u"), axis=1)
        else:
            n = j - n_col
            cs = slice(n * MLP_PIECE, (n + 1) * MLP_PIECE)
            state["acc"][n] = state["acc"][n] + jnp.dot(
                state["uc"], wdown_ref[c * D_MODEL:(c + 1) * D_MODEL, cs], preferred_element_type=F32)

    n_pieces = n_col + n_chunks * 2 * n_col
    n_pairs = (tq // WINDOW) * N_LANE_BLOCKS
    pending = scores(0)
    for k in range(max(n_pieces, n_pairs)):
        nxt = scores(k + 1) if k + 1 < n_pairs else None
        if k < n_pieces:
            piece(k)
        if k < n_pairs:
            finish(k, *pending)
        pending = nxt
    y_ref[...] = jnp.concatenate(state["acc"], axis=1)


def _prompt_attention(q, kd, vd, x2, bias, sinks, lmask, w_o, g_ffn, w_up, w_down, *, bsz, length, tq):
    nt = length // tq
    n_tiles = bsz * nt
    per_win = tq // WINDOW
    att_tile = lambda s: jnp.minimum(s, n_tiles - 1)
    mlp_tile = lambda s: jnp.maximum(s - 1, 0)
    att = lambda width: pl.BlockSpec((tq, width), lambda s: (att_tile(s), 0))
    mlp = pl.BlockSpec((tq, D_MODEL), lambda s: (mlp_tile(s), 0))
    prev = pl.BlockSpec((WINDOW, 2 * D_KV), lambda s: (jnp.maximum(att_tile(s) * per_win - 1, 0), 0))
    return pl.pallas_call(
        functools.partial(_prompt_attn_kernel, tiles_per_seq=nt, n_tiles=n_tiles),
        grid=(n_tiles + 1,),
        in_specs=[att(D_MODEL), att(2 * D_KV), att(2 * D_KV), prev, prev, mlp,
                  _resident(bias), _resident(sinks), _resident(lmask), _resident(w_o), _resident(g_ffn),
                  _resident_layer(w_up, 1), _resident_layer(w_down, 1)],
        out_specs=mlp,
        out_shape=jax.ShapeDtypeStruct((bsz * length, D_MODEL), F32),
        scratch_shapes=[pltpu.VMEM((2, tq, D_MODEL), BF16)],
        compiler_params=pltpu.CompilerParams(
            dimension_semantics=("arbitrary",), vmem_limit_bytes=VMEM_LIMIT),
        name="prompt_attention",
    )(q, kd, vd, kd, vd, x2, bias, sinks, lmask, w_o, g_ffn, w_up, w_down)


def _alibi_slopes():
    return (2.0 ** (-8.0 * np.arange(1, N_HEADS + 1, dtype=np.float32) / N_HEADS)).astype(np.float32)


def _prompt_bias():
    qi = np.arange(WINDOW)[:, None]
    kj = np.arange(2 * WINDOW)[None, :]
    dist = qi - kj + WINDOW
    valid = (dist >= 0) & (dist < WINDOW)
    valid = np.stack([valid, valid & (kj >= WINDOW)], axis=0)[:, None]
    pen = -_alibi_slopes()[:, None, None] * dist.astype(np.float32)[None]
    return np.where(valid, pen[None], np.float32(MASKED)).astype(np.float32)


def _sample_attn_kernel(q_ref, kc_ref, vc_ref, kn_ref, vn_ref, bias_c_ref, bias_n_ref, sink_ref,
                        o_ref, kbuf_ref, vbuf_ref):
    g = kc_ref.shape[0]
    t_new = kn_ref.shape[0] // g
    nt = (((1,), (1,)), ((), ()))
    pad = jnp.zeros((LANES - g * t_new, D_KV), F32)
    kn_t = jnp.concatenate([kn_ref[...], pad], axis=0).T
    vn_t = jnp.concatenate([vn_ref[...], pad], axis=0).T
    lane = lax.broadcasted_iota(jnp.int32, (HEAD_DIM, WINDOW), 1)
    is_new = lane >= WINDOW - t_new

    for kh in range(N_KV_HEADS):
        hs = slice(kh * HEAD_DIM, (kh + 1) * HEAD_DIM)
        q = q_ref[kh]
        k_t = jnp.concatenate([kc_ref[s, kh] for s in range(g)], axis=1)
        v_t = jnp.concatenate([vc_ref[s, kh] for s in range(g)], axis=1)
        s_c = jnp.dot(q, k_t.astype(BF16), preferred_element_type=F32) + bias_c_ref[kh]
        s_n = jnp.dot(q, kn_t[hs].astype(BF16), preferred_element_type=F32) + bias_n_ref[kh]
        sink = sink_ref[kh]
        m = jnp.maximum(jnp.maximum(jnp.max(s_c, axis=-1, keepdims=True),
                                    jnp.max(s_n, axis=-1, keepdims=True)), sink)
        p_c = jnp.exp(s_c - m)
        p_n = jnp.exp(s_n - m)
        den = (jnp.sum(p_c, axis=-1, keepdims=True) + jnp.sum(p_n, axis=-1, keepdims=True)
               + jnp.exp(sink - m))
        o = (lax.dot_general(p_c.astype(BF16), v_t.astype(BF16), nt, preferred_element_type=F32)
             + lax.dot_general(p_n.astype(BF16), vn_t[hs].astype(BF16), nt, preferred_element_type=F32))
        o_ref[kh] = o / den

        for s in range(g):
            shift_new = (WINDOW - t_new - s * t_new) % WINDOW
            k_new = pltpu.roll(kn_t[hs], shift_new, 1)
            v_new = pltpu.roll(vn_t[hs], shift_new, 1)
            kbuf_ref[s, kh] = jnp.where(is_new, k_new, pltpu.roll(kc_ref[s, kh], WINDOW - t_new, 1))
            vbuf_ref[s, kh] = jnp.where(is_new, v_new, pltpu.roll(vc_ref[s, kh], WINDOW - t_new, 1))


def _sample_attention(q_heads, cache_kt, cache_vt, k_new, v_new, bias_c, bias_n, sink_rows, *, t_new):
    g = SAMPLE_GROUP
    bsz = cache_kt.shape[0]
    rows_q = g * Q_PER_KV * t_new
    cache_spec = pl.BlockSpec((g, N_KV_HEADS, HEAD_DIM, WINDOW), lambda i: (i, 0, 0, 0))
    q_spec = pl.BlockSpec((N_KV_HEADS, rows_q, HEAD_DIM), lambda i: (0, i, 0))
    new_spec = pl.BlockSpec((g * t_new, D_KV), lambda i: (i, 0))
    return pl.pallas_call(
        _sample_attn_kernel,
        grid=(bsz // g,),
        in_specs=[q_spec, cache_spec, cache_spec, new_spec, new_spec,
                  _resident(bias_c), _resident(bias_n), _resident(sink_rows)],
        out_specs=(q_spec, cache_spec, cache_spec),
        out_shape=(jax.ShapeDtypeStruct(q_heads.shape, F32),
                   jax.ShapeDtypeStruct(cache_kt.shape, cache_kt.dtype),
                   jax.ShapeDtypeStruct(cache_vt.shape, cache_vt.dtype)),
        compiler_params=pltpu.CompilerParams(
            dimension_semantics=("parallel",), vmem_limit_bytes=VMEM_LIMIT),
        name="sample_attention",
    )(q_heads, cache_kt, cache_vt, k_new, v_new, bias_c, bias_n, sink_rows)


def _sample_bias(t_new):
    g = SAMPLE_GROUP
    slope = _alibi_slopes().reshape(N_KV_HEADS, 1, Q_PER_KV, 1, 1, 1)
    seq_q = np.arange(g).reshape(1, g, 1, 1, 1, 1)
    t = np.arange(t_new).reshape(1, 1, 1, t_new, 1, 1)
    seq_k = np.arange(g).reshape(1, 1, 1, 1, g, 1)
    j = np.arange(WINDOW).reshape(1, 1, 1, 1, 1, WINDOW)
    dist = t + WINDOW - j
    ok = (seq_q == seq_k) & (dist >= 0) & (dist < WINDOW)
    rows = g * Q_PER_KV * t_new
    bias_c = np.where(ok, -slope * dist.astype(np.float32), np.float32(MASKED))
    bias_c = bias_c.reshape(N_KV_HEADS, rows, g * WINDOW).astype(np.float32)
    tn = np.arange(t_new).reshape(1, 1, 1, 1, 1, t_new)
    dist_n = t - tn
    ok_n = (seq_q == seq_k) & (dist_n >= 0) & (dist_n < WINDOW)
    bias_n = np.where(ok_n, -slope * dist_n.astype(np.float32), np.float32(MASKED))
    bias_n = bias_n.reshape(N_KV_HEADS, rows, g * t_new)
    bias_n = np.pad(bias_n, ((0, 0), (0, 0), (0, LANES - g * t_new)), constant_values=MASKED)
    return bias_c, bias_n.astype(np.float32)


def _out_mlp_kernel(o_ref, x_ref, wo_ref, gffn_ref, wup_ref, wdown_ref, y_ref):
    x3 = x_ref[...] + jnp.dot(o_ref[...], wo_ref[...], preferred_element_type=F32)
    y_ref[...] = _mlp(x3, gffn_ref[...], wup_ref, wdown_ref)


def _out_mlp(o, x2, w_o, g_ffn, w_up, w_down, tm):
    rows = x2.shape[0]
    row = pl.BlockSpec((tm, D_MODEL), lambda i: (i, 0))
    return pl.pallas_call(
        _out_mlp_kernel,
        grid=(rows // tm,),
        in_specs=[row, row, _resident(w_o), _resident(g_ffn), _resident_layer(w_up, 1),
                  _resident_layer(w_down, 1)],
        out_specs=row,
        out_shape=jax.ShapeDtypeStruct((rows, D_MODEL), F32),
        compiler_params=pltpu.CompilerParams(
            dimension_semantics=("parallel",), vmem_limit_bytes=VMEM_LIMIT),
        name="out_mlp",
    )(o, x2, w_o, g_ffn, w_up, w_down)


def _repeat_heads(w):
    d = w.shape[0]
    return jnp.tile(w.reshape(d, N_KV_HEADS, 1, HEAD_DIM), (1, 1, 2, 1)).reshape(d, 2 * D_KV)


def kernel(x_prompt, x_sample, state_ssm_re, state_ssm_im, cache_k, cache_v, norm_mix, norm_ffn,
           ssm_a_re, ssm_a_im, ssm_log_dt, ssm_b_re, ssm_b_im, ssm_c_re, ssm_c_im, ssm_d, ssm_w_glu,
           kv_norm, w_kv, k_norm, w_q, q_norm, attn_sinks, w_o, w_up, w_down):
    bsz, length, d = x_prompt.shape
    dec_b, dec_t, _ = x_sample.shape

    ab_re, ab_im, bb_re, bb_im = _s5_discretize(ssm_a_re[0], ssm_a_im[0], ssm_log_dt[0],
                                                ssm_b_re[0], ssm_b_im[0])
    b_blk, c_blk = _block_diag_weights(bb_re, bb_im, ssm_c_re[0], ssm_c_im[0])
    ab_re = ab_re.reshape(1, D_STATE)
    ab_im = ab_im.reshape(1, D_STATE)
    row = lambda v: v.reshape(1, -1).astype(F32)
    w_glu = ssm_w_glu[0].astype(BF16)
    w_up_b = _to_bf16(w_up)
    w_down_b = _to_bf16(w_down)
    w_k, w_v = w_kv[:, :D_KV], w_kv[:, D_KV:]
    w_kv_ext = jnp.concatenate([w_k, w_v, _repeat_heads(w_k), _repeat_heads(w_v)], axis=1).astype(BF16)
    w_q0 = w_q[0].astype(BF16)
    w_o0 = w_o[0].astype(BF16)
    g_k = row(jnp.tile(k_norm, D_KV // HEAD_DIM))
    g_q = row(jnp.tile(q_norm[0], D_KV // HEAD_DIM))
    avg = jnp.asarray(np.kron(np.eye(D_KV // HEAD_DIM, dtype=np.float32),
                              np.full((HEAD_DIM, HEAD_DIM), 1.0 / HEAD_DIM, np.float32)), BF16)
    lane = np.arange(LANES)
    lmask = jnp.asarray(np.stack([lane < HEAD_DIM, lane >= HEAD_DIM]), BF16)
    sinks = row(attn_sinks[0])

    def layer0(x, h0r, h0i, tc, state_lanes_batch):
        return _s5_layer(x, h0r, h0i, row(norm_mix[0]), row(ssm_d[0]), ab_re, ab_im, b_blk, c_blk,
                         w_glu, tc=tc, state_lanes_batch=state_lanes_batch)

    def mlp_proj(x2d, tm):
        return _mlp_proj(x2d, tm, row(norm_ffn[0]), w_up_b, w_down_b, row(kv_norm), w_kv_ext, g_k,
                         row(norm_mix[1]), w_q0, g_q, avg)

    zeros = jnp.zeros((bsz, D_STATE), F32)
    x1, hr_p, hi_p = layer0(x_prompt, zeros, zeros, 16, False)
    x2, q, kd, vd, k, v = mlp_proj(x1.reshape(bsz * length, d), 512)
    y_prompt = _prompt_attention(q, kd, vd, x2, jnp.asarray(_prompt_bias()), sinks, lmask, w_o0,
                                 row(norm_ffn[1]), w_up_b, w_down_b, bsz=bsz, length=length, tq=512)
    y_prompt = y_prompt.reshape(bsz, length, d)
    k_p = k.reshape(bsz, length, D_KV)[:, -WINDOW:].reshape(bsz, WINDOW, N_KV_HEADS, HEAD_DIM)
    v_p = v.reshape(bsz, length, D_KV)[:, -WINDOW:].reshape(bsz, WINDOW, N_KV_HEADS, HEAD_DIM)
    state_shape = (1, bsz, N_GROUPS, STATE_N)
    re_p = hr_p.reshape(state_shape).astype(state_ssm_re.dtype)
    im_p = hi_p.reshape(state_shape).astype(state_ssm_im.dtype)

    h0r_t = jnp.transpose(state_ssm_re[0], (1, 2, 0)).reshape(D_STATE, dec_b).astype(F32)
    h0i_t = jnp.transpose(state_ssm_im[0], (1, 2, 0)).reshape(D_STATE, dec_b).astype(F32)
    x1s, hr_s, hi_s = layer0(x_sample, h0r_t, h0i_t, dec_t, True)
    x2s, qs, _, _, ks, vs = mlp_proj(x1s.reshape(dec_b * dec_t, d), dec_b * dec_t)
    q_heads = qs.reshape(dec_b, dec_t, N_KV_HEADS, Q_PER_KV, HEAD_DIM).transpose(2, 0, 3, 1, 4)
    q_heads = q_heads.reshape(N_KV_HEADS, dec_b * Q_PER_KV * dec_t, HEAD_DIM)
    bias_c, bias_n = _sample_bias(dec_t)
    sink_rows = jnp.broadcast_to(
        attn_sinks[0].astype(F32).reshape(N_KV_HEADS, 1, Q_PER_KV, 1),
        (N_KV_HEADS, SAMPLE_GROUP, Q_PER_KV, dec_t)).reshape(N_KV_HEADS, -1, 1)
    o_heads, k_s_t, v_s_t = _sample_attention(
        q_heads, jnp.transpose(cache_k, (0, 2, 3, 1)), jnp.transpose(cache_v, (0, 2, 3, 1)),
        ks.astype(cache_k.dtype), vs.astype(cache_v.dtype), jnp.asarray(bias_c), jnp.asarray(bias_n),
        sink_rows, t_new=dec_t)
    o_s = o_heads.reshape(N_KV_HEADS, dec_b, Q_PER_KV, dec_t, HEAD_DIM).transpose(1, 3, 0, 2, 4)
    o_s = o_s.reshape(dec_b * dec_t, d).astype(BF16)
    y_sample = _out_mlp(o_s, x2s, w_o0, row(norm_ffn[1]), w_up_b, w_down_b, dec_b * dec_t)
    y_sample = y_sample.reshape(dec_b, dec_t, d)
    k_s = jnp.transpose(k_s_t, (0, 3, 1, 2))
    v_s = jnp.transpose(v_s_t, (0, 3, 1, 2))
    re_s = jnp.transpose(hr_s.reshape(N_GROUPS, STATE_N, dec_b), (2, 0, 1))[None].astype(state_ssm_re.dtype)
    im_s = jnp.transpose(hi_s.reshape(N_GROUPS, STATE_N, dec_b), (2, 0, 1))[None].astype(state_ssm_im.dtype)

    return (y_prompt, y_sample, re_p, im_p, k_p, v_p, re_s, im_s, k_s, v_s)
```

```python
import functools

import jax
import jax.numpy as jnp
import numpy as np
from jax import lax
from jax.experimental import pallas as pl
from jax.experimental.pallas import tpu as pltpu

F32 = jnp.float32
BF16 = jnp.bfloat16

D_MODEL = 1024
GROUP_SIZE = 16
N_GROUPS = D_MODEL // GROUP_SIZE
STATE_N = 64
D_STATE = N_GROUPS * STATE_N
HEAD_DIM = 64
N_HEADS = D_MODEL // HEAD_DIM
N_KV_HEADS = 4
Q_PER_KV = N_HEADS // N_KV_HEADS
D_KV = N_KV_HEADS * HEAD_DIM
WINDOW = 128
D_FF = 4 * D_MODEL
EPS = 1e-6
MASKED = -1e30

LANES = 128
GROUPS_PER_BLOCK = LANES // GROUP_SIZE
N_LANE_BLOCKS = D_MODEL // LANES
STATES_PER_BLOCK = GROUPS_PER_BLOCK * STATE_N
SUB_BATCH = 16
SCAN_LANES = 256
SAMPLE_GROUP = 8
MLP_PIECE = 256
VMEM_LIMIT = 56 * 1024 * 1024
CAST_BLOCK_BYTES = 4 * 1024 * 1024


def _resident(a):
    return pl.BlockSpec(a.shape, lambda *_: (0,) * a.ndim, pipeline_mode=pl.Buffered(1))


def _resident_layer(a, layer):
    return pl.BlockSpec((None,) + a.shape[1:], lambda *_: (layer,) + (0,) * (a.ndim - 1),
                        pipeline_mode=pl.Buffered(1))


def _rms(x, g):
    return x * lax.rsqrt(jnp.mean(x * x, axis=-1, keepdims=True) + EPS) * g


def _head_mean_sq(x, avg):
    sq = x * x
    hi = sq.astype(BF16)
    lo = (sq - hi.astype(F32)).astype(BF16)
    return (jnp.dot(hi, avg, preferred_element_type=F32)
            + jnp.dot(lo, avg, preferred_element_type=F32))


def _cast_kernel(x_ref, o_ref):
    o_ref[...] = x_ref[...].astype(o_ref.dtype)


def _to_bf16(w):
    n, r, c = w.shape
    br = min(r, CAST_BLOCK_BYTES // (4 * c))
    spec = pl.BlockSpec((1, br, c), lambda i, j: (i, j, 0))
    return pl.pallas_call(
        _cast_kernel, grid=(n, r // br), in_specs=[spec], out_specs=spec,
        out_shape=jax.ShapeDtypeStruct(w.shape, BF16),
        compiler_params=pltpu.CompilerParams(dimension_semantics=("parallel", "parallel")),
        name="to_bf16",
    )(w)


def _discretize_kernel(a_re_ref, a_im_ref, log_dt_ref, b_re_ref, b_im_ref,
                       ab_re_ref, ab_im_ref, bb_re_ref, bb_im_ref):
    dt = jnp.exp(log_dt_ref[...])
    a_re = a_re_ref[...]
    a_im = a_im_ref[...]
    mag = jnp.exp(dt * a_re)
    ab_re = mag * jnp.cos(dt * a_im)
    ab_im = mag * jnp.sin(dt * a_im)
    den = a_re * a_re + a_im * a_im
    nr = ab_re - 1.0
    ni = ab_im
    f_re = (nr * a_re + ni * a_im) / den
    f_im = (ni * a_re - nr * a_im) / den
    ab_re_ref[...] = ab_re
    ab_im_ref[...] = ab_im
    b_re = b_re_ref[...]
    b_im = b_im_ref[...]
    fr = f_re[:, None, :]
    fi = f_im[:, None, :]
    bb_re_ref[...] = fr * b_re - fi * b_im
    bb_im_ref[...] = fr * b_im + fi * b_re


def _s5_discretize(a_re, a_im, log_dt, b_re, b_im):
    g, n = a_re.shape
    c = b_re.shape[-1]
    out = pl.pallas_call(
        _discretize_kernel,
        out_shape=(jax.ShapeDtypeStruct((g, n), F32), jax.ShapeDtypeStruct((g, n), F32),
                   jax.ShapeDtypeStruct((g, c, n), F32), jax.ShapeDtypeStruct((g, c, n), F32)),
        name="s5_discretize",
    )(a_re, a_im, log_dt.reshape(g, 1), jnp.swapaxes(b_re, 1, 2), jnp.swapaxes(b_im, 1, 2))
    return out


def _block_diag_weights(bb_re, bb_im, c_re, c_im):
    nb, gb = N_LANE_BLOCKS, GROUPS_PER_BLOCK
    eye = jnp.eye(gb, dtype=F32)
    bb = jnp.stack([bb_re, bb_im], axis=0).reshape(2, nb, gb, GROUP_SIZE, STATE_N)
    b_blk = jnp.einsum("rjgcn,gh->jgcrhn", bb, eye).reshape(nb, LANES, 2 * STATES_PER_BLOCK)
    cc = jnp.stack([c_re, -c_im], axis=0).reshape(2, nb, gb, GROUP_SIZE, STATE_N)
    c_blk = jnp.einsum("rjgcn,gh->jrgnhc", cc, eye).reshape(nb, 2 * STATES_PER_BLOCK, LANES)
    return b_blk.astype(BF16), c_blk.astype(BF16)


def _s5_layer_kernel(x_hbm, h0r_ref, h0i_ref, g_ref, d_ref, abr_ref, abi_ref, bblk_ref, cblk_ref,
                     wglu_ref, o_hbm, hro_ref, hio_ref, xbuf, obuf, in_sem, out_sem, hr_sc, hi_sc,
                     bur, bui, xsr, xsi, hn_sc, *, state_lanes_batch):
    _, tc, bb, d = xbuf.shape
    rows = tc * bb
    spb = STATES_PER_BLOCK
    step_idx = pl.program_id(0)
    n_steps = pl.num_programs(0)
    slot = step_idx % 2

    def in_copy(step, sl, b):
        return pltpu.make_async_copy(x_hbm.at[b, pl.ds(step * tc, tc), :], xbuf.at[sl, :, b, :],
                                     in_sem.at[sl])

    def out_copy(step, sl, b):
        return pltpu.make_async_copy(obuf.at[sl, :, b, :], o_hbm.at[b, pl.ds(step * tc, tc), :],
                                     out_sem.at[sl])

    @pl.when(step_idx == 0)
    def _():
        for b in range(bb):
            in_copy(0, 0, b).start()
        if state_lanes_batch:
            hr_sc[...] = h0r_ref[...].T
            hi_sc[...] = h0i_ref[...].T
        else:
            hr_sc[...] = h0r_ref[...]
            hi_sc[...] = h0i_ref[...]

    @pl.when(step_idx + 1 < n_steps)
    def _():
        for b in range(bb):
            in_copy(step_idx + 1, 1 - slot, b).start()

    for b in range(bb):
        in_copy(step_idx, slot, b).wait()

    hn = _rms(xbuf[slot].reshape(rows, d), g_ref[...])
    hn_sc[...] = hn
    hnb = hn.astype(BF16)

    def input_map(j):
        bu = jnp.dot(hnb[:, j * LANES:(j + 1) * LANES], bblk_ref[j], preferred_element_type=F32)
        bur[j] = bu[:, :spb]
        bui[j] = bu[:, spb:]

    def recurrence(j):
        for half in range(spb // SCAN_LANES):
            cs = slice(half * SCAN_LANES, (half + 1) * SCAN_LANES)
            gs = slice(j * spb + half * SCAN_LANES, j * spb + (half + 1) * SCAN_LANES)
            ar = jnp.broadcast_to(abr_ref[:, gs], (SUB_BATCH, SCAN_LANES))
            ai = jnp.broadcast_to(abi_ref[:, gs], (SUB_BATCH, SCAN_LANES))
            for sb in range(bb // SUB_BATCH):
                seqs = slice(sb * SUB_BATCH, (sb + 1) * SUB_BATCH)
                h_re = hr_sc[seqs, gs]
                h_im = hi_sc[seqs, gs]
                for t in range(tc):
                    rs = slice(t * bb + sb * SUB_BATCH, t * bb + (sb + 1) * SUB_BATCH)
                    n_re = ar * h_re - ai * h_im + bur[j, rs, cs]
                    n_im = ar * h_im + ai * h_re + bui[j, rs, cs]
                    xsr[j, rs, cs] = n_re.astype(BF16)
                    xsi[j, rs, cs] = n_im.astype(BF16)
                    h_re, h_im = n_re, n_im
                hr_sc[seqs, gs] = h_re
                hi_sc[seqs, gs] = h_im

    def output_map(j):
        ls = slice(j * LANES, (j + 1) * LANES)
        y = (jnp.dot(xsr[j], cblk_ref[j, :spb, :], preferred_element_type=F32)
             + jnp.dot(xsi[j], cblk_ref[j, spb:, :], preferred_element_type=F32))
        y = y + d_ref[:, ls] * hn_sc[:, ls]
        return jax.nn.gelu(y).astype(BF16)

    gelu_cols = []
    input_map(0)
    for j in range(N_LANE_BLOCKS):
        if j + 1 < N_LANE_BLOCKS:
            input_map(j + 1)
        recurrence(j)
        if j > 0:
            gelu_cols.append(output_map(j - 1))
    gelu_cols.append(output_map(N_LANE_BLOCKS - 1))

    z = jnp.dot(jnp.concatenate(gelu_cols, axis=1), wglu_ref[...], preferred_element_type=F32)
    mix = z[:, :d] * jax.nn.sigmoid(z[:, d:])

    @pl.when(step_idx >= 2)
    def _():
        for b in range(bb):
            out_copy(step_idx - 2, slot, b).wait()

    obuf[slot] = (xbuf[slot].reshape(rows, d) + mix).reshape(tc, bb, d)
    for b in range(bb):
        out_copy(step_idx, slot, b).start()

    @pl.when(step_idx == n_steps - 1)
    def _():
        @pl.when(n_steps > 1)
        def _():
            for b in range(bb):
                out_copy(step_idx - 1, 1 - slot, b).wait()
        for b in range(bb):
            out_copy(step_idx, slot, b).wait()
        if state_lanes_batch:
            hro_ref[...] = hr_sc[...].T
            hio_ref[...] = hi_sc[...].T
        else:
            hro_ref[...] = hr_sc[...]
            hio_ref[...] = hi_sc[...]


def _s5_layer(x, h0r, h0i, g_mix, d_skip, ab_re, ab_im, b_blk, c_blk, w_glu, *, tc, state_lanes_batch):
    bsz, length, d = x.shape
    rows = tc * bsz
    weights = (g_mix, d_skip, ab_re, ab_im, b_blk, c_blk, w_glu)
    state_shape = jax.ShapeDtypeStruct(h0r.shape, F32)
    block_shape = (N_LANE_BLOCKS, rows, STATES_PER_BLOCK)
    return pl.pallas_call(
        functools.partial(_s5_layer_kernel, state_lanes_batch=state_lanes_batch),
        grid=(length // tc,),
        in_specs=[pl.BlockSpec(memory_space=pl.ANY), _resident(h0r), _resident(h0i)]
        + [_resident(w) for w in weights],
        out_specs=(pl.BlockSpec(memory_space=pl.ANY), pl.BlockSpec(h0r.shape, lambda i: (0, 0)),
                   pl.BlockSpec(h0i.shape, lambda i: (0, 0))),
        out_shape=(jax.ShapeDtypeStruct((bsz, length, d), F32), state_shape, state_shape),
        scratch_shapes=[pltpu.VMEM((2, tc, bsz, d), F32), pltpu.VMEM((2, tc, bsz, d), F32),
                        pltpu.SemaphoreType.DMA((2,)), pltpu.SemaphoreType.DMA((2,)),
                        pltpu.VMEM((bsz, D_STATE), F32), pltpu.VMEM((bsz, D_STATE), F32),
                        pltpu.VMEM(block_shape, F32), pltpu.VMEM(block_shape, F32),
                        pltpu.VMEM(block_shape, BF16), pltpu.VMEM(block_shape, BF16),
                        pltpu.VMEM((rows, d), F32)],
        compiler_params=pltpu.CompilerParams(
            dimension_semantics=("arbitrary",), vmem_limit_bytes=VMEM_LIMIT),
        name="s5_layer",
    )(x, h0r, h0i, *weights)


def _mlp_chunk(hn, w_up_ref, w_down_ref, c):
    cs = slice(c * D_MODEL, (c + 1) * D_MODEL)
    u = jnp.dot(hn, w_up_ref[:, cs], preferred_element_type=F32)
    u = jnp.square(jnp.maximum(u, 0.0)).astype(BF16)
    return jnp.dot(u, w_down_ref[cs, :], preferred_element_type=F32)


def _mlp(x, g, w_up_ref, w_down_ref):
    hn = _rms(x, g).astype(BF16)
    acc = x
    for c in range(w_up_ref.shape[1] // D_MODEL):
        acc = acc + _mlp_chunk(hn, w_up_ref, w_down_ref, c)
    return acc


def _mlp_proj_kernel(x_ref, gffn_ref, wup_ref, wdown_ref, gkv_ref, wkv_ref, gk_ref, gq1_ref, wq_ref,
                     gq_ref, avg_ref, x2_ref, q_ref, kd_ref, vd_ref, k_ref, v_ref):
    x2 = _mlp(x_ref[...], gffn_ref[...], wup_ref, wdown_ref)
    x2_ref[...] = x2
    normed = x2 * lax.rsqrt(jnp.mean(x2 * x2, axis=-1, keepdims=True) + EPS)
    avg = avg_ref[...]

    kv = jnp.dot((normed * gkv_ref[...]).astype(BF16), wkv_ref[...], preferred_element_type=F32)
    k = kv[:, :D_KV]
    k = k * lax.rsqrt(_head_mean_sq(k, avg) + EPS) * gk_ref[...]
    v = kv[:, D_KV:]
    k_ref[...] = k
    v_ref[...] = v
    lane = lax.broadcasted_iota(jnp.int32, (x2.shape[0], LANES), 1)
    for src, dst_ref in ((k, kd_ref), (v, vd_ref)):
        for blk in range(D_KV // LANES):
            a = src[:, blk * LANES:(blk + 1) * LANES]
            swapped = pltpu.roll(a, HEAD_DIM, 1)
            dst_ref[:, 2 * blk * LANES:(2 * blk + 1) * LANES] = (
                jnp.where(lane < HEAD_DIM, a, swapped).astype(BF16))
            dst_ref[:, (2 * blk + 1) * LANES:(2 * blk + 2) * LANES] = (
                jnp.where(lane < HEAD_DIM, swapped, a).astype(BF16))

    hq = (normed * gq1_ref[...]).astype(BF16)
    for c in range(D_MODEL // D_KV):
        cs = slice(c * D_KV, (c + 1) * D_KV)
        q = jnp.dot(hq, wq_ref[:, cs], preferred_element_type=F32)
        q = q * lax.rsqrt(_head_mean_sq(q, avg) + EPS) * gq_ref[...]
        q_ref[:, cs] = (q * (HEAD_DIM ** -0.5)).astype(BF16)


def _mlp_proj(x2d, tm, g_ffn, w_up, w_down, g_kv, w_kv, g_k, g_q1, w_q, g_q, avg):
    rows = x2d.shape[0]
    row = lambda width: pl.BlockSpec((tm, width), lambda i: (i, 0))
    tail = (g_kv, w_kv, g_k, g_q1, w_q, g_q, avg)
    return pl.pallas_call(
        _mlp_proj_kernel,
        grid=(rows // tm,),
        in_specs=[row(D_MODEL), _resident(g_ffn), _resident_layer(w_up, 0), _resident_layer(w_down, 0)]
        + [_resident(w) for w in tail],
        out_specs=(row(D_MODEL), row(D_MODEL), row(2 * D_KV), row(2 * D_KV), row(D_KV), row(D_KV)),
        out_shape=(jax.ShapeDtypeStruct((rows, D_MODEL), F32),
                   jax.ShapeDtypeStruct((rows, D_MODEL), BF16),
                   jax.ShapeDtypeStruct((rows, 2 * D_KV), BF16),
                   jax.ShapeDtypeStruct((rows, 2 * D_KV), BF16),
                   jax.ShapeDtypeStruct((rows, D_KV), F32),
                   jax.ShapeDtypeStruct((rows, D_KV), F32)),
        compiler_params=pltpu.CompilerParams(
            dimension_semantics=("parallel",), vmem_limit_bytes=VMEM_LIMIT),
        name="mlp_proj",
    )(x2d, g_ffn, w_up, w_down, *tail)


def _prompt_attn_kernel(q_ref, kd_ref, vd_ref, kdp_ref, vdp_ref, x_ref, bias_ref, sink_ref, lmask_ref,
                        wo_ref, gffn_ref, wup_ref, wdown_ref, y_ref, o_sc, *, tiles_per_seq, n_tiles):
    tq = q_ref.shape[0]
    step = pl.program_id(0)
    slot = step % 2

    @pl.when(step == 0)
    def _():
        o_sc[1] = jnp.zeros(o_sc.shape[1:], o_sc.dtype)

    tile = jnp.minimum(step, n_tiles - 1)
    first = (tile % tiles_per_seq == 0).astype(jnp.int32)
    m_lo = lmask_ref[0:1, :]
    m_hi = lmask_ref[1:2, :]
    ones_lo = jnp.broadcast_to(m_lo, (2 * WINDOW, LANES))
    ones_hi = jnp.broadcast_to(m_hi, (2 * WINDOW, LANES))
    nt = (((1,), (1,)), ((), ()))
    lane = lax.broadcasted_iota(jnp.int32, (WINDOW, LANES), 1)
    pairs_per_kv = Q_PER_KV // 2
    windows = {}

    def window(r, kh):
        if (r, kh) not in windows:
            ls = slice(kh * LANES, (kh + 1) * LANES)
            if r == 0:
                kwin = jnp.concatenate([kdp_ref[:, ls], kd_ref[0:WINDOW, ls]], axis=0)
                vwin = jnp.concatenate([vdp_ref[:, ls], vd_ref[0:WINDOW, ls]], axis=0)
            else:
                kwin = kd_ref[(r - 1) * WINDOW:(r + 1) * WINDOW, ls]
                vwin = vd_ref[(r - 1) * WINDOW:(r + 1) * WINDOW, ls]
            windows[(r, kh)] = (kwin * m_lo, kwin * m_hi,
                                jnp.concatenate([vwin * m_lo, ones_lo], axis=1),
                                jnp.concatenate([vwin * m_hi, ones_hi], axis=1))
        return windows[(r, kh)]

    def scores(p):
        r, blk = divmod(p, N_LANE_BLOCKS)
        k_lo, k_hi, _, _ = window(r, blk // pairs_per_kv)
        sel = first if r == 0 else 0
        qp = q_ref[r * WINDOW:(r + 1) * WINDOW, blk * LANES:(blk + 1) * LANES]
        s0 = lax.dot_general(qp, k_lo, nt, preferred_element_type=F32) + bias_ref[sel, 2 * blk]
        s1 = lax.dot_general(qp, k_hi, nt, preferred_element_type=F32) + bias_ref[sel, 2 * blk + 1]
        return s0, s1

    def finish(p, s0, s1):
        r, blk = divmod(p, N_LANE_BLOCKS)
        _, _, v_lo, v_hi = window(r, blk // pairs_per_kv)
        h0 = 2 * blk
        m0 = jnp.maximum(jnp.max(s0, axis=-1, keepdims=True), sink_ref[:, h0:h0 + 1])
        m1 = jnp.maximum(jnp.max(s1, axis=-1, keepdims=True), sink_ref[:, h0 + 1:h0 + 2])
        p0 = jnp.exp(s0 - m0).astype(BF16)
        p1 = jnp.exp(s1 - m1).astype(BF16)
        ext = (jnp.dot(p0, v_lo, preferred_element_type=F32)
               + jnp.dot(p1, v_hi, preferred_element_type=F32))
        e0 = jnp.exp(sink_ref[:, h0:h0 + 1] - m0)
        e1 = jnp.exp(sink_ref[:, h0 + 1:h0 + 2] - m1)
        den = ext[:, LANES:] + jnp.where(lane < HEAD_DIM, e0, e1)
        o_sc[slot, r * WINDOW:(r + 1) * WINDOW, blk * LANES:(blk + 1) * LANES] = (
            (ext[:, :LANES] / den).astype(BF16))

    n_col = D_MODEL // MLP_PIECE
    n_chunks = wup_ref.shape[1] // D_MODEL
    o_prev = o_sc[1 - slot]
    state = {}

    def piece(k):
        if k < n_col:
            cs = slice(k * MLP_PIECE, (k + 1) * MLP_PIECE)
            state.setdefault("x3", []).append(
                x_ref[:, cs] + jnp.dot(o_prev, wo_ref[:, cs], preferred_element_type=F32))
            if k == n_col - 1:
                x3 = jnp.concatenate(state["x3"], axis=1)
                state["hn"] = _rms(x3, gffn_ref[...]).astype(BF16)
                state["acc"] = state["x3"]
            return
        c, j = divmod(k - n_col, 2 * n_col)
        if j < n_col:
            cs = slice(c * D_MODEL + j * MLP_PIECE, c * D_MODEL + (j + 1) * MLP_PIECE)
            u = jnp.dot(state["hn"], wup_ref[:, cs], preferred_element_type=F32)
            state.setdefault("u", []).append(jnp.square(jnp.maximum(u, 0.0)).astype(BF16))
            if j == n_col - 1:
                state["uc"] = jnp.concatenate(state.pop("u"), axis=1)
        else:
            n = j - n_col
            cs = slice(n * MLP_PIECE, (n + 1) * MLP_PIECE)
            state["acc"][n] = state["acc"][n] + jnp.dot(
                state["uc"], wdown_ref[c * D_MODEL:(c + 1) * D_MODEL, cs], preferred_element_type=F32)

    n_pieces = n_col + n_chunks * 2 * n_col
    n_pairs = (tq // WINDOW) * N_LANE_BLOCKS
    pending = scores(0)
    for k in range(max(n_pieces, n_pairs)):
        nxt = scores(k + 1) if k + 1 < n_pairs else None
        if k < n_pieces:
            piece(k)
        if k < n_pairs:
            finish(k, *pending)
        pending = nxt
    y_ref[...] = jnp.concatenate(state["acc"], axis=1)


def _prompt_attention(q, kd, vd, x2, bias, sinks, lmask, w_o, g_ffn, w_up, w_down, *, bsz, length, tq):
    nt = length // tq
    n_tiles = bsz * nt
    per_win = tq // WINDOW
    att_tile = lambda s: jnp.minimum(s, n_tiles - 1)
    mlp_tile = lambda s: jnp.maximum(s - 1, 0)
    att = lambda width: pl.BlockSpec((tq, width), lambda s: (att_tile(s), 0))
    mlp = pl.BlockSpec((tq, D_MODEL), lambda s: (mlp_tile(s), 0))
    prev = pl.BlockSpec((WINDOW, 2 * D_KV), lambda s: (jnp.maximum(att_tile(s) * per_win - 1, 0), 0))
    return pl.pallas_call(
        functools.partial(_prompt_attn_kernel, tiles_per_seq=nt, n_tiles=n_tiles),
        grid=(n_tiles + 1,),
        in_specs=[att(D_MODEL), att(2 * D_KV), att(2 * D_KV), prev, prev, mlp,
                  _resident(bias), _resident(sinks), _resident(lmask), _resident(w_o), _resident(g_ffn),
                  _resident_layer(w_up, 1), _resident_layer(w_down, 1)],
        out_specs=mlp,
        out_shape=jax.ShapeDtypeStruct((bsz * length, D_MODEL), F32),
        scratch_shapes=[pltpu.VMEM((2, tq, D_MODEL), BF16)],
        compiler_params=pltpu.CompilerParams(
            dimension_semantics=("arbitrary",), vmem_limit_bytes=VMEM_LIMIT),
        name="prompt_attention",
    )(q, kd, vd, kd, vd, x2, bias, sinks, lmask, w_o, g_ffn, w_up, w_down)


def _alibi_slopes():
    return (2.0 ** (-8.0 * np.arange(1, N_HEADS + 1, dtype=np.float32) / N_HEADS)).astype(np.float32)


def _prompt_bias():
    qi = np.arange(WINDOW)[:, None]
    kj = np.arange(2 * WINDOW)[None, :]
    dist = qi - kj + WINDOW
    valid = (dist >= 0) & (dist < WINDOW)
    valid = np.stack([valid, valid & (kj >= WINDOW)], axis=0)[:, None]
    pen = -_alibi_slopes()[:, None, None] * dist.astype(np.float32)[None]
    return np.where(valid, pen[None], np.float32(MASKED)).astype(np.float32)


def _sample_attn_kernel(q_ref, kc_ref, vc_ref, kn_ref, vn_ref, bias_c_ref, bias_n_ref, sink_ref,
                        o_ref, kbuf_ref, vbuf_ref):
    g = kc_ref.shape[0]
    t_new = kn_ref.shape[0] // g
    nt = (((1,), (1,)), ((), ()))
    pad = jnp.zeros((LANES - g * t_new, D_KV), F32)
    kn_t = jnp.concatenate([kn_ref[...], pad], axis=0).T
    vn_t = jnp.concatenate([vn_ref[...], pad], axis=0).T
    lane = lax.broadcasted_iota(jnp.int32, (HEAD_DIM, WINDOW), 1)
    is_new = lane >= WINDOW - t_new

    for kh in range(N_KV_HEADS):
        hs = slice(kh * HEAD_DIM, (kh + 1) * HEAD_DIM)
        q = q_ref[kh]
        k_t = jnp.concatenate([kc_ref[s, kh] for s in range(g)], axis=1)
        v_t = jnp.concatenate([vc_ref[s, kh] for s in range(g)], axis=1)
        s_c = jnp.dot(q, k_t.astype(BF16), preferred_element_type=F32) + bias_c_ref[kh]
        s_n = jnp.dot(q, kn_t[hs].astype(BF16), preferred_element_type=F32) + bias_n_ref[kh]
        sink = sink_ref[kh]
        m = jnp.maximum(jnp.maximum(jnp.max(s_c, axis=-1, keepdims=True),
                                    jnp.max(s_n, axis=-1, keepdims=True)), sink)
        p_c = jnp.exp(s_c - m)
        p_n = jnp.exp(s_n - m)
        den = (jnp.sum(p_c, axis=-1, keepdims=True) + jnp.sum(p_n, axis=-1, keepdims=True)
               + jnp.exp(sink - m))
        o = (lax.dot_general(p_c.astype(BF16), v_t.astype(BF16), nt, preferred_element_type=F32)
             + lax.dot_general(p_n.astype(BF16), vn_t[hs].astype(BF16), nt, preferred_element_type=F32))
        o_ref[kh] = o / den

        for s in range(g):
            shift_new = (WINDOW - t_new - s * t_new) % WINDOW
            k_new = pltpu.roll(kn_t[hs], shift_new, 1)
            v_new = pltpu.roll(vn_t[hs], shift_new, 1)
            kbuf_ref[s, kh] = jnp.where(is_new, k_new, pltpu.roll(kc_ref[s, kh], WINDOW - t_new, 1))
            vbuf_ref[s, kh] = jnp.where(is_new, v_new, pltpu.roll(vc_ref[s, kh], WINDOW - t_new, 1))


def _sample_attention(q_heads, cache_kt, cache_vt, k_new, v_new, bias_c, bias_n, sink_rows, *, t_new):
    g = SAMPLE_GROUP
    bsz = cache_kt.shape[0]
    rows_q = g * Q_PER_KV * t_new
    cache_spec = pl.BlockSpec((g, N_KV_HEADS, HEAD_DIM, WINDOW), lambda i: (i, 0, 0, 0))
    q_spec = pl.BlockSpec((N_KV_HEADS, rows_q, HEAD_DIM), lambda i: (0, i, 0))
    new_spec = pl.BlockSpec((g * t_new, D_KV), lambda i: (i, 0))
    return pl.pallas_call(
        _sample_attn_kernel,
        grid=(bsz // g,),
        in_specs=[q_spec, cache_spec, cache_spec, new_spec, new_spec,
                  _resident(bias_c), _resident(bias_n), _resident(sink_rows)],
        out_specs=(q_spec, cache_spec, cache_spec),
        out_shape=(jax.ShapeDtypeStruct(q_heads.shape, F32),
                   jax.ShapeDtypeStruct(cache_kt.shape, cache_kt.dtype),
                   jax.ShapeDtypeStruct(cache_vt.shape, cache_vt.dtype)),
        compiler_params=pltpu.CompilerParams(
            dimension_semantics=("parallel",), vmem_limit_bytes=VMEM_LIMIT),
        name="sample_attention",
    )(q_heads, cache_kt, cache_vt, k_new, v_new, bias_c, bias_n, sink_rows)


def _sample_bias(t_new):
    g = SAMPLE_GROUP
    slope = _alibi_slopes().reshape(N_KV_HEADS, 1, Q_PER_KV, 1, 1, 1)
    seq_q = np.arange(g).reshape(1, g, 1, 1, 1, 1)
    t = np.arange(t_new).reshape(1, 1, 1, t_new, 1, 1)
    seq_k = np.arange(g).reshape(1, 1, 1, 1, g, 1)
    j = np.arange(WINDOW).reshape(1, 1, 1, 1, 1, WINDOW)
    dist = t + WINDOW - j
    ok = (seq_q == seq_k) & (dist >= 0) & (dist < WINDOW)
    rows = g * Q_PER_KV * t_new
    bias_c = np.where(ok, -slope * dist.astype(np.float32), np.float32(MASKED))
    bias_c = bias_c.reshape(N_KV_HEADS, rows, g * WINDOW).astype(np.float32)
    tn = np.arange(t_new).reshape(1, 1, 1, 1, 1, t_new)
    dist_n = t - tn
    ok_n = (seq_q == seq_k) & (dist_n >= 0) & (dist_n < WINDOW)
    bias_n = np.where(ok_n, -slope * dist_n.astype(np.float32), np.float32(MASKED))
    bias_n = bias_n.reshape(N_KV_HEADS, rows, g * t_new)
    bias_n = np.pad(bias_n, ((0, 0), (0, 0), (0, LANES - g * t_new)), constant_values=MASKED)
    return bias_c, bias_n.astype(np.float32)


def _out_mlp_kernel(o_ref, x_ref, wo_ref, gffn_ref, wup_ref, wdown_ref, y_ref):
    x3 = x_ref[...] + jnp.dot(o_ref[...], wo_ref[...], preferred_element_type=F32)
    y_ref[...] = _mlp(x3, gffn_ref[...], wup_ref, wdown_ref)


def _out_mlp(o, x2, w_o, g_ffn, w_up, w_down, tm):
    rows = x2.shape[0]
    row = pl.BlockSpec((tm, D_MODEL), lambda i: (i, 0))
    return pl.pallas_call(
        _out_mlp_kernel,
        grid=(rows // tm,),
        in_specs=[row, row, _resident(w_o), _resident(g_ffn), _resident_layer(w_up, 1),
                  _resident_layer(w_down, 1)],
        out_specs=row,
        out_shape=jax.ShapeDtypeStruct((rows, D_MODEL), F32),
        compiler_params=pltpu.CompilerParams(
            dimension_semantics=("parallel",), vmem_limit_bytes=VMEM_LIMIT),
        name="out_mlp",
    )(o, x2, w_o, g_ffn, w_up, w_down)


def kernel(x_prompt, x_sample, state_ssm_re, state_ssm_im, cache_k, cache_v, norm_mix, norm_ffn,
           ssm_a_re, ssm_a_im, ssm_log_dt, ssm_b_re, ssm_b_im, ssm_c_re, ssm_c_im, ssm_d, ssm_w_glu,
           kv_norm, w_kv, k_norm, w_q, q_norm, attn_sinks, w_o, w_up, w_down):
    bsz, length, d = x_prompt.shape
    dec_b, dec_t, _ = x_sample.shape

    ab_re, ab_im, bb_re, bb_im = _s5_discretize(ssm_a_re[0], ssm_a_im[0], ssm_log_dt[0],
                                                ssm_b_re[0], ssm_b_im[0])
    b_blk, c_blk = _block_diag_weights(bb_re, bb_im, ssm_c_re[0], ssm_c_im[0])
    ab_re = ab_re.reshape(1, D_STATE)
    ab_im = ab_im.reshape(1, D_STATE)
    row = lambda v: v.reshape(1, -1).astype(F32)
    w_glu = ssm_w_glu[0].astype(BF16)
    w_up_b = _to_bf16(w_up)
    w_down_b = _to_bf16(w_down)
    w_kv_b = w_kv.astype(BF16)
    w_q0 = w_q[0].astype(BF16)
    w_o0 = w_o[0].astype(BF16)
    g_k = row(jnp.tile(k_norm, D_KV // HEAD_DIM))
    g_q = row(jnp.tile(q_norm[0], D_KV // HEAD_DIM))
    avg = jnp.asarray(np.kron(np.eye(D_KV // HEAD_DIM, dtype=np.float32),
                              np.full((HEAD_DIM, HEAD_DIM), 1.0 / HEAD_DIM, np.float32)), BF16)
    lane = np.arange(LANES)
    lmask = jnp.asarray(np.stack([lane < HEAD_DIM, lane >= HEAD_DIM]), BF16)
    sinks = row(attn_sinks[0])

    def layer0(x, h0r, h0i, tc, state_lanes_batch):
        return _s5_layer(x, h0r, h0i, row(norm_mix[0]), row(ssm_d[0]), ab_re, ab_im, b_blk, c_blk,
                         w_glu, tc=tc, state_lanes_batch=state_lanes_batch)

    def mlp_proj(x2d, tm):
        return _mlp_proj(x2d, tm, row(norm_ffn[0]), w_up_b, w_down_b, row(kv_norm), w_kv_b, g_k,
                         row(norm_mix[1]), w_q0, g_q, avg)

    zeros = jnp.zeros((bsz, D_STATE), F32)
    x1, hr_p, hi_p = layer0(x_prompt, zeros, zeros, 16, False)
    x2, q, kd, vd, k, v = mlp_proj(x1.reshape(bsz * length, d), 512)
    y_prompt = _prompt_attention(q, kd, vd, x2, jnp.asarray(_prompt_bias()), sinks, lmask, w_o0,
                                 row(norm_ffn[1]), w_up_b, w_down_b, bsz=bsz, length=length, tq=512)
    y_prompt = y_prompt.reshape(bsz, length, d)
    k_p = k.reshape(bsz, length, D_KV)[:, -WINDOW:].reshape(bsz, WINDOW, N_KV_HEADS, HEAD_DIM)
    v_p = v.reshape(bsz, length, D_KV)[:, -WINDOW:].reshape(bsz, WINDOW, N_KV_HEADS, HEAD_DIM)
    state_shape = (1, bsz, N_GROUPS, STATE_N)
    re_p = hr_p.reshape(state_shape).astype(state_ssm_re.dtype)
    im_p = hi_p.reshape(state_shape).astype(state_ssm_im.dtype)

    h0r_t = jnp.transpose(state_ssm_re[0], (1, 2, 0)).reshape(D_STATE, dec_b).astype(F32)
    h0i_t = jnp.transpose(state_ssm_im[0], (1, 2, 0)).reshape(D_STATE, dec_b).astype(F32)
    x1s, hr_s, hi_s = layer0(x_sample, h0r_t, h0i_t, dec_t, True)
    x2s, qs, _, _, ks, vs = mlp_proj(x1s.reshape(dec_b * dec_t, d), dec_b * dec_t)
    q_heads = qs.reshape(dec_b, dec_t, N_KV_HEADS, Q_PER_KV, HEAD_DIM).transpose(2, 0, 3, 1, 4)
    q_heads = q_heads.reshape(N_KV_HEADS, dec_b * Q_PER_KV * dec_t, HEAD_DIM)
    bias_c, bias_n = _sample_bias(dec_t)
    sink_rows = jnp.broadcast_to(
        attn_sinks[0].astype(F32).reshape(N_KV_HEADS, 1, Q_PER_KV, 1),
        (N_KV_HEADS, SAMPLE_GROUP, Q_PER_KV, dec_t)).reshape(N_KV_HEADS, -1, 1)
    o_heads, k_s_t, v_s_t = _sample_attention(
        q_heads, jnp.transpose(cache_k, (0, 2, 3, 1)), jnp.transpose(cache_v, (0, 2, 3, 1)),
        ks.astype(cache_k.dtype), vs.astype(cache_v.dtype), jnp.asarray(bias_c), jnp.asarray(bias_n),
        sink_rows, t_new=dec_t)
    o_s = o_heads.reshape(N_KV_HEADS, dec_b, Q_PER_KV, dec_t, HEAD_DIM).transpose(1, 3, 0, 2, 4)
    o_s = o_s.reshape(dec_b * dec_t, d).astype(BF16)
    y_sample = _out_mlp(o_s, x2s, w_o0, row(norm_ffn[1]), w_up_b, w_down_b, dec_b * dec_t)
    y_sample = y_sample.reshape(dec_b, dec_t, d)
    k_s = jnp.transpose(k_s_t, (0, 3, 1, 2))
    v_s = jnp.transpose(v_s_t, (0, 3, 1, 2))
    re_s = jnp.transpose(hr_s.reshape(N_GROUPS, STATE_N, dec_b), (2, 0, 1))[None].astype(state_ssm_re.dtype)
    im_s = jnp.transpose(hi_s.reshape(N_GROUPS, STATE_N, dec_b), (2, 0, 1))[None].astype(state_ssm_im.dtype)

    return (y_prompt, y_sample, re_p, im_p, k_p, v_p, re_s, im_s, k_s, v_s)
```

```python
import functools

import jax
import jax.numpy as jnp
import numpy as np
from jax import lax
from jax.experimental import pallas as pl
from jax.experimental.pallas import tpu as pltpu

F32 = jnp.float32
BF16 = jnp.bfloat16

D_MODEL = 1024
GROUP_SIZE = 16
N_GROUPS = D_MODEL // GROUP_SIZE
STATE_N = 64
D_STATE = N_GROUPS * STATE_N
HEAD_DIM = 64
N_HEADS = D_MODEL // HEAD_DIM
N_KV_HEADS = 4
Q_PER_KV = N_HEADS // N_KV_HEADS
D_KV = N_KV_HEADS * HEAD_DIM
WINDOW = 128
D_FF = 4 * D_MODEL
EPS = 1e-6
MASKED = -1e30

LANES = 128
GROUPS_PER_BLOCK = LANES // GROUP_SIZE
N_LANE_BLOCKS = D_MODEL // LANES
STATES_PER_BLOCK = GROUPS_PER_BLOCK * STATE_N
SUB_BATCH = 16
SCAN_LANES = 256
SAMPLE_GROUP = 8
MLP_PIECE = 256
VMEM_LIMIT = 56 * 1024 * 1024
CAST_BLOCK_BYTES = 4 * 1024 * 1024


def _resident(a):
    return pl.BlockSpec(a.shape, lambda *_: (0,) * a.ndim, pipeline_mode=pl.Buffered(1))


def _resident_layer(a, layer):
    return pl.BlockSpec((None,) + a.shape[1:], lambda *_: (layer,) + (0,) * (a.ndim - 1),
                        pipeline_mode=pl.Buffered(1))


def _rms(x, g):
    return x * lax.rsqrt(jnp.mean(x * x, axis=-1, keepdims=True) + EPS) * g


def _head_mean_sq(x, avg):
    return jnp.dot((x * x).astype(BF16), avg, preferred_element_type=F32)


def _cast_kernel(x_ref, o_ref):
    o_ref[...] = x_ref[...].astype(o_ref.dtype)


def _to_bf16(w):
    n, r, c = w.shape
    br = min(r, CAST_BLOCK_BYTES // (4 * c))
    spec = pl.BlockSpec((1, br, c), lambda i, j: (i, j, 0))
    return pl.pallas_call(
        _cast_kernel, grid=(n, r // br), in_specs=[spec], out_specs=spec,
        out_shape=jax.ShapeDtypeStruct(w.shape, BF16),
        compiler_params=pltpu.CompilerParams(dimension_semantics=("parallel", "parallel")),
        name="to_bf16",
    )(w)


def _discretize_kernel(a_re_ref, a_im_ref, log_dt_ref, b_re_ref, b_im_ref,
                       ab_re_ref, ab_im_ref, bb_re_ref, bb_im_ref):
    dt = jnp.exp(log_dt_ref[...])
    a_re = a_re_ref[...]
    a_im = a_im_ref[...]
    mag = jnp.exp(dt * a_re)
    ab_re = mag * jnp.cos(dt * a_im)
    ab_im = mag * jnp.sin(dt * a_im)
    den = a_re * a_re + a_im * a_im
    nr = ab_re - 1.0
    ni = ab_im
    f_re = (nr * a_re + ni * a_im) / den
    f_im = (ni * a_re - nr * a_im) / den
    ab_re_ref[...] = ab_re
    ab_im_ref[...] = ab_im
    b_re = b_re_ref[...]
    b_im = b_im_ref[...]
    fr = f_re[:, None, :]
    fi = f_im[:, None, :]
    bb_re_ref[...] = fr * b_re - fi * b_im
    bb_im_ref[...] = fr * b_im + fi * b_re


def _s5_discretize(a_re, a_im, log_dt, b_re, b_im):
    g, n = a_re.shape
    c = b_re.shape[-1]
    out = pl.pallas_call(
        _discretize_kernel,
        out_shape=(jax.ShapeDtypeStruct((g, n), F32), jax.ShapeDtypeStruct((g, n), F32),
                   jax.ShapeDtypeStruct((g, c, n), F32), jax.ShapeDtypeStruct((g, c, n), F32)),
        name="s5_discretize",
    )(a_re, a_im, log_dt.reshape(g, 1), jnp.swapaxes(b_re, 1, 2), jnp.swapaxes(b_im, 1, 2))
    return out


def _block_diag_weights(bb_re, bb_im, c_re, c_im):
    nb, gb = N_LANE_BLOCKS, GROUPS_PER_BLOCK
    eye = jnp.eye(gb, dtype=F32)
    bb = jnp.stack([bb_re, bb_im], axis=0).reshape(2, nb, gb, GROUP_SIZE, STATE_N)
    b_blk = jnp.einsum("rjgcn,gh->jgcrhn", bb, eye).reshape(nb, LANES, 2 * STATES_PER_BLOCK)
    cc = jnp.stack([c_re, -c_im], axis=0).reshape(2, nb, gb, GROUP_SIZE, STATE_N)
    c_blk = jnp.einsum("rjgcn,gh->jrgnhc", cc, eye).reshape(nb, 2 * STATES_PER_BLOCK, LANES)
    return b_blk.astype(BF16), c_blk.astype(BF16)


def _s5_layer_kernel(x_hbm, h0r_ref, h0i_ref, g_ref, d_ref, abr_ref, abi_ref, bblk_ref, cblk_ref,
                     wglu_ref, o_hbm, hro_ref, hio_ref, xbuf, obuf, in_sem, out_sem, hr_sc, hi_sc,
                     bur, bui, xsr, xsi, hn_sc, *, state_lanes_batch):
    _, tc, bb, d = xbuf.shape
    rows = tc * bb
    spb = STATES_PER_BLOCK
    step_idx = pl.program_id(0)
    n_steps = pl.num_programs(0)
    slot = step_idx % 2

    def in_copy(step, sl, b):
        return pltpu.make_async_copy(x_hbm.at[b, pl.ds(step * tc, tc), :], xbuf.at[sl, :, b, :],
                                     in_sem.at[sl])

    def out_copy(step, sl, b):
        return pltpu.make_async_copy(obuf.at[sl, :, b, :], o_hbm.at[b, pl.ds(step * tc, tc), :],
                                     out_sem.at[sl])

    @pl.when(step_idx == 0)
    def _():
        for b in range(bb):
            in_copy(0, 0, b).start()
        if state_lanes_batch:
            hr_sc[...] = h0r_ref[...].T
            hi_sc[...] = h0i_ref[...].T
        else:
            hr_sc[...] = h0r_ref[...]
            hi_sc[...] = h0i_ref[...]

    @pl.when(step_idx + 1 < n_steps)
    def _():
        for b in range(bb):
            in_copy(step_idx + 1, 1 - slot, b).start()

    for b in range(bb):
        in_copy(step_idx, slot, b).wait()

    hn = _rms(xbuf[slot].reshape(rows, d), g_ref[...])
    hn_sc[...] = hn
    hnb = hn.astype(BF16)

    def input_map(j):
        bu = jnp.dot(hnb[:, j * LANES:(j + 1) * LANES], bblk_ref[j], preferred_element_type=F32)
        bur[j] = bu[:, :spb]
        bui[j] = bu[:, spb:]

    def recurrence(j):
        for half in range(spb // SCAN_LANES):
            cs = slice(half * SCAN_LANES, (half + 1) * SCAN_LANES)
            gs = slice(j * spb + half * SCAN_LANES, j * spb + (half + 1) * SCAN_LANES)
            ar = jnp.broadcast_to(abr_ref[:, gs], (SUB_BATCH, SCAN_LANES))
            ai = jnp.broadcast_to(abi_ref[:, gs], (SUB_BATCH, SCAN_LANES))
            for sb in range(bb // SUB_BATCH):
                seqs = slice(sb * SUB_BATCH, (sb + 1) * SUB_BATCH)
                h_re = hr_sc[seqs, gs]
                h_im = hi_sc[seqs, gs]
                for t in range(tc):
                    rs = slice(t * bb + sb * SUB_BATCH, t * bb + (sb + 1) * SUB_BATCH)
                    n_re = ar * h_re - ai * h_im + bur[j, rs, cs]
                    n_im = ar * h_im + ai * h_re + bui[j, rs, cs]
                    xsr[j, rs, cs] = n_re.astype(BF16)
                    xsi[j, rs, cs] = n_im.astype(BF16)
                    h_re, h_im = n_re, n_im
                hr_sc[seqs, gs] = h_re
                hi_sc[seqs, gs] = h_im

    def output_map(j):
        ls = slice(j * LANES, (j + 1) * LANES)
        y = (jnp.dot(xsr[j], cblk_ref[j, :spb, :], preferred_element_type=F32)
             + jnp.dot(xsi[j], cblk_ref[j, spb:, :], preferred_element_type=F32))
        y = y + d_ref[:, ls] * hn_sc[:, ls]
        return jax.nn.gelu(y).astype(BF16)

    gelu_cols = []
    input_map(0)
    for j in range(N_LANE_BLOCKS):
        if j + 1 < N_LANE_BLOCKS:
            input_map(j + 1)
        recurrence(j)
        if j > 0:
            gelu_cols.append(output_map(j - 1))
    gelu_cols.append(output_map(N_LANE_BLOCKS - 1))

    z = jnp.dot(jnp.concatenate(gelu_cols, axis=1), wglu_ref[...], preferred_element_type=F32)
    mix = z[:, :d] * jax.nn.sigmoid(z[:, d:])

    @pl.when(step_idx >= 2)
    def _():
        for b in range(bb):
            out_copy(step_idx - 2, slot, b).wait()

    obuf[slot] = (xbuf[slot].reshape(rows, d) + mix).reshape(tc, bb, d)
    for b in range(bb):
        out_copy(step_idx, slot, b).start()

    @pl.when(step_idx == n_steps - 1)
    def _():
        @pl.when(n_steps > 1)
        def _():
            for b in range(bb):
                out_copy(step_idx - 1, 1 - slot, b).wait()
        for b in range(bb):
            out_copy(step_idx, slot, b).wait()
        if state_lanes_batch:
            hro_ref[...] = hr_sc[...].T
            hio_ref[...] = hi_sc[...].T
        else:
            hro_ref[...] = hr_sc[...]
            hio_ref[...] = hi_sc[...]


def _s5_layer(x, h0r, h0i, g_mix, d_skip, ab_re, ab_im, b_blk, c_blk, w_glu, *, tc, state_lanes_batch):
    bsz, length, d = x.shape
    rows = tc * bsz
    weights = (g_mix, d_skip, ab_re, ab_im, b_blk, c_blk, w_glu)
    state_shape = jax.ShapeDtypeStruct(h0r.shape, F32)
    block_shape = (N_LANE_BLOCKS, rows, STATES_PER_BLOCK)
    return pl.pallas_call(
        functools.partial(_s5_layer_kernel, state_lanes_batch=state_lanes_batch),
        grid=(length // tc,),
        in_specs=[pl.BlockSpec(memory_space=pl.ANY), _resident(h0r), _resident(h0i)]
        + [_resident(w) for w in weights],
        out_specs=(pl.BlockSpec(memory_space=pl.ANY), pl.BlockSpec(h0r.shape, lambda i: (0, 0)),
                   pl.BlockSpec(h0i.shape, lambda i: (0, 0))),
        out_shape=(jax.ShapeDtypeStruct((bsz, length, d), F32), state_shape, state_shape),
        scratch_shapes=[pltpu.VMEM((2, tc, bsz, d), F32), pltpu.VMEM((2, tc, bsz, d), F32),
                        pltpu.SemaphoreType.DMA((2,)), pltpu.SemaphoreType.DMA((2,)),
                        pltpu.VMEM((bsz, D_STATE), F32), pltpu.VMEM((bsz, D_STATE), F32),
                        pltpu.VMEM(block_shape, F32), pltpu.VMEM(block_shape, F32),
                        pltpu.VMEM(block_shape, BF16), pltpu.VMEM(block_shape, BF16),
                        pltpu.VMEM((rows, d), F32)],
        compiler_params=pltpu.CompilerParams(
            dimension_semantics=("arbitrary",), vmem_limit_bytes=VMEM_LIMIT),
        name="s5_layer",
    )(x, h0r, h0i, *weights)


def _mlp(x, g, w_up_ref, w_down_ref):
    hn = _rms(x, g).astype(BF16)
    n_chunks = w_up_ref.shape[1] // D_MODEL

    def up(c):
        u = jnp.dot(hn, w_up_ref[:, c * D_MODEL:(c + 1) * D_MODEL], preferred_element_type=F32)
        return jnp.square(jnp.maximum(u, 0.0)).astype(BF16)

    acc = x
    u = up(0)
    for c in range(n_chunks):
        u_next = up(c + 1) if c + 1 < n_chunks else None
        acc = acc + jnp.dot(u, w_down_ref[c * D_MODEL:(c + 1) * D_MODEL, :], preferred_element_type=F32)
        u = u_next
    return acc


def _mlp_proj_kernel(x_ref, gffn_ref, wup_ref, wdown_ref, gkv_ref, wkv_ref, gk_ref, gq1_ref, wq_ref,
                     gq_ref, avg_ref, x2_ref, q_ref, kd_ref, vd_ref, k_ref, v_ref):
    x2 = _mlp(x_ref[...], gffn_ref[...], wup_ref, wdown_ref)
    x2_ref[...] = x2
    normed = x2 * lax.rsqrt(jnp.mean(x2 * x2, axis=-1, keepdims=True) + EPS)
    avg = avg_ref[...]

    kv = jnp.dot((normed * gkv_ref[...]).astype(BF16), wkv_ref[...], preferred_element_type=F32)
    k = kv[:, :D_KV]
    k = k * lax.rsqrt(_head_mean_sq(k, avg) + EPS) * gk_ref[...]
    v = kv[:, D_KV:]
    k_ref[...] = k
    v_ref[...] = v
    lane = lax.broadcasted_iota(jnp.int32, (x2.shape[0], LANES), 1)
    for src, dst_ref in ((k, kd_ref), (v, vd_ref)):
        for blk in range(D_KV // LANES):
            a = src[:, blk * LANES:(blk + 1) * LANES]
            swapped = pltpu.roll(a, HEAD_DIM, 1)
            dst_ref[:, 2 * blk * LANES:(2 * blk + 1) * LANES] = (
                jnp.where(lane < HEAD_DIM, a, swapped).astype(BF16))
            dst_ref[:, (2 * blk + 1) * LANES:(2 * blk + 2) * LANES] = (
                jnp.where(lane < HEAD_DIM, swapped, a).astype(BF16))

    hq = (normed * gq1_ref[...]).astype(BF16)
    for c in range(D_MODEL // D_KV):
        cs = slice(c * D_KV, (c + 1) * D_KV)
        q = jnp.dot(hq, wq_ref[:, cs], preferred_element_type=F32)
        q = q * lax.rsqrt(_head_mean_sq(q, avg) + EPS) * gq_ref[...]
        q_ref[:, cs] = (q * (HEAD_DIM ** -0.5)).astype(BF16)


def _mlp_proj(x2d, tm, g_ffn, w_up, w_down, g_kv, w_kv, g_k, g_q1, w_q, g_q, avg):
    rows = x2d.shape[0]
    row = lambda width: pl.BlockSpec((tm, width), lambda i: (i, 0))
    tail = (g_kv, w_kv, g_k, g_q1, w_q, g_q, avg)
    return pl.pallas_call(
        _mlp_proj_kernel,
        grid=(rows // tm,),
        in_specs=[row(D_MODEL), _resident(g_ffn), _resident_layer(w_up, 0), _resident_layer(w_down, 0)]
        + [_resident(w) for w in tail],
        out_specs=(row(D_MODEL), row(D_MODEL), row(2 * D_KV), row(2 * D_KV), row(D_KV), row(D_KV)),
        out_shape=(jax.ShapeDtypeStruct((rows, D_MODEL), F32),
                   jax.ShapeDtypeStruct((rows, D_MODEL), BF16),
                   jax.ShapeDtypeStruct((rows, 2 * D_KV), BF16),
                   jax.ShapeDtypeStruct((rows, 2 * D_KV), BF16),
                   jax.ShapeDtypeStruct((rows, D_KV), F32),
                   jax.ShapeDtypeStruct((rows, D_KV), F32)),
        compiler_params=pltpu.CompilerParams(
            dimension_semantics=("parallel",), vmem_limit_bytes=VMEM_LIMIT),
        name="mlp_proj",
    )(x2d, g_ffn, w_up, w_down, *tail)


def _prompt_attn_kernel(q_ref, kd_ref, vd_ref, kdp_ref, vdp_ref, x_ref, bias_ref, sink_ref, lmask_ref,
                        wo_ref, gffn_ref, wup_ref, wdown_ref, y_ref, o_sc, *, tiles_per_seq, n_tiles):
    tq = q_ref.shape[0]
    step = pl.program_id(0)
    slot = step % 2

    @pl.when(step == 0)
    def _():
        o_sc[1] = jnp.zeros(o_sc.shape[1:], o_sc.dtype)

    tile = jnp.minimum(step, n_tiles - 1)
    first = (tile % tiles_per_seq == 0).astype(jnp.int32)
    m_lo = lmask_ref[0:1, :]
    m_hi = lmask_ref[1:2, :]
    ones_lo = jnp.broadcast_to(m_lo, (2 * WINDOW, LANES))
    ones_hi = jnp.broadcast_to(m_hi, (2 * WINDOW, LANES))
    nt = (((1,), (1,)), ((), ()))
    lane = lax.broadcasted_iota(jnp.int32, (WINDOW, LANES), 1)
    pairs_per_kv = Q_PER_KV // 2
    windows = {}

    def window(r, kh):
        if (r, kh) not in windows:
            ls = slice(kh * LANES, (kh + 1) * LANES)
            if r == 0:
                kwin = jnp.concatenate([kdp_ref[:, ls], kd_ref[0:WINDOW, ls]], axis=0)
                vwin = jnp.concatenate([vdp_ref[:, ls], vd_ref[0:WINDOW, ls]], axis=0)
            else:
                kwin = kd_ref[(r - 1) * WINDOW:(r + 1) * WINDOW, ls]
                vwin = vd_ref[(r - 1) * WINDOW:(r + 1) * WINDOW, ls]
            windows[(r, kh)] = (kwin * m_lo, kwin * m_hi,
                                jnp.concatenate([vwin * m_lo, ones_lo], axis=1),
                                jnp.concatenate([vwin * m_hi, ones_hi], axis=1))
        return windows[(r, kh)]

    def scores(p):
        r, blk = divmod(p, N_LANE_BLOCKS)
        k_lo, k_hi, _, _ = window(r, blk // pairs_per_kv)
        sel = first if r == 0 else 0
        qp = q_ref[r * WINDOW:(r + 1) * WINDOW, blk * LANES:(blk + 1) * LANES]
        s0 = lax.dot_general(qp, k_lo, nt, preferred_element_type=F32) + bias_ref[sel, 2 * blk]
        s1 = lax.dot_general(qp, k_hi, nt, preferred_element_type=F32) + bias_ref[sel, 2 * blk + 1]
        return s0, s1

    def finish(p, s0, s1):
        r, blk = divmod(p, N_LANE_BLOCKS)
        _, _, v_lo, v_hi = window(r, blk // pairs_per_kv)
        h0 = 2 * blk
        m0 = jnp.maximum(jnp.max(s0, axis=-1, keepdims=True), sink_ref[:, h0:h0 + 1])
        m1 = jnp.maximum(jnp.max(s1, axis=-1, keepdims=True), sink_ref[:, h0 + 1:h0 + 2])
        p0 = jnp.exp(s0 - m0).astype(BF16)
        p1 = jnp.exp(s1 - m1).astype(BF16)
        ext = (jnp.dot(p0, v_lo, preferred_element_type=F32)
               + jnp.dot(p1, v_hi, preferred_element_type=F32))
        e0 = jnp.exp(sink_ref[:, h0:h0 + 1] - m0)
        e1 = jnp.exp(sink_ref[:, h0 + 1:h0 + 2] - m1)
        den = ext[:, LANES:] + jnp.where(lane < HEAD_DIM, e0, e1)
        o_sc[slot, r * WINDOW:(r + 1) * WINDOW, blk * LANES:(blk + 1) * LANES] = (
            (ext[:, :LANES] / den).astype(BF16))

    n_col = D_MODEL // MLP_PIECE
    n_chunks = wup_ref.shape[1] // D_MODEL
    o_prev = o_sc[1 - slot]
    state = {}

    def piece(kind, c, n):
        cs = slice(n * MLP_PIECE, (n + 1) * MLP_PIECE)
        if kind == "wo":
            state.setdefault("x3", []).append(
                x_ref[:, cs] + jnp.dot(o_prev, wo_ref[:, cs], preferred_element_type=F32))
            if n == n_col - 1:
                x3 = jnp.concatenate(state["x3"], axis=1)
                state["hn"] = _rms(x3, gffn_ref[...]).astype(BF16)
                state["acc"] = state["x3"]
        elif kind == "up":
            hs = slice(c * D_MODEL + n * MLP_PIECE, c * D_MODEL + (n + 1) * MLP_PIECE)
            u = jnp.dot(state["hn"], wup_ref[:, hs], preferred_element_type=F32)
            state.setdefault(("u", c), []).append(jnp.square(jnp.maximum(u, 0.0)).astype(BF16))
        else:
            if n == 0:
                state[("uc", c)] = jnp.concatenate(state.pop(("u", c)), axis=1)
            state["acc"][n] = state["acc"][n] + jnp.dot(
                state[("uc", c)], wdown_ref[c * D_MODEL:(c + 1) * D_MODEL, cs],
                preferred_element_type=F32)

    schedule = [("wo", 0, n) for n in range(n_col)] + [("up", 0, n) for n in range(n_col)]
    for c in range(n_chunks):
        if c + 1 < n_chunks:
            schedule += [("up", c + 1, n) for n in range(n_col)]
        schedule += [("down", c, n) for n in range(n_col)]

    n_pairs = (tq // WINDOW) * N_LANE_BLOCKS
    pending = scores(0)
    for k in range(max(len(schedule), n_pairs)):
        nxt = scores(k + 1) if k + 1 < n_pairs else None
        if k < len(schedule):
            piece(*schedule[k])
        if k < n_pairs:
            finish(k, *pending)
        pending = nxt
    y_ref[...] = jnp.concatenate(state["acc"], axis=1)


def _prompt_attention(q, kd, vd, x2, bias, sinks, lmask, w_o, g_ffn, w_up, w_down, *, bsz, length, tq):
    nt = length // tq
    n_tiles = bsz * nt
    per_win = tq // WINDOW
    att_tile = lambda s: jnp.minimum(s, n_tiles - 1)
    mlp_tile = lambda s: jnp.maximum(s - 1, 0)
    att = lambda width: pl.BlockSpec((tq, width), lambda s: (att_tile(s), 0))
    mlp = pl.BlockSpec((tq, D_MODEL), lambda s: (mlp_tile(s), 0))
    prev = pl.BlockSpec((WINDOW, 2 * D_KV), lambda s: (jnp.maximum(att_tile(s) * per_win - 1, 0), 0))
    return pl.pallas_call(
        functools.partial(_prompt_attn_kernel, tiles_per_seq=nt, n_tiles=n_tiles),
        grid=(n_tiles + 1,),
        in_specs=[att(D_MODEL), att(2 * D_KV), att(2 * D_KV), prev, prev, mlp,
                  _resident(bias), _resident(sinks), _resident(lmask), _resident(w_o), _resident(g_ffn),
                  _resident_layer(w_up, 1), _resident_layer(w_down, 1)],
        out_specs=mlp,
        out_shape=jax.ShapeDtypeStruct((bsz * length, D_MODEL), F32),
        scratch_shapes=[pltpu.VMEM((2, tq, D_MODEL), BF16)],
        compiler_params=pltpu.CompilerParams(
            dimension_semantics=("arbitrary",), vmem_limit_bytes=VMEM_LIMIT),
        name="prompt_attention",
    )(q, kd, vd, kd, vd, x2, bias, sinks, lmask, w_o, g_ffn, w_up, w_down)


def _alibi_slopes():
    return (2.0 ** (-8.0 * np.arange(1, N_HEADS + 1, dtype=np.float32) / N_HEADS)).astype(np.float32)


def _prompt_bias():
    qi = np.arange(WINDOW)[:, None]
    kj = np.arange(2 * WINDOW)[None, :]
    dist = qi - kj + WINDOW
    valid = (dist >= 0) & (dist < WINDOW)
    valid = np.stack([valid, valid & (kj >= WINDOW)], axis=0)[:, None]
    pen = -_alibi_slopes()[:, None, None] * dist.astype(np.float32)[None]
    return np.where(valid, pen[None], np.float32(MASKED)).astype(np.float32)


def _sample_attn_kernel(q_ref, kc_ref, vc_ref, kn_ref, vn_ref, bias_c_ref, bias_n_ref, sink_ref,
                        o_ref, kbuf_ref, vbuf_ref):
    g = kc_ref.shape[0]
    t_new = kn_ref.shape[0] // g
    nt = (((1,), (1,)), ((), ()))
    pad = jnp.zeros((LANES - g * t_new, D_KV), F32)
    kn_t = jnp.concatenate([kn_ref[...], pad], axis=0).T
    vn_t = jnp.concatenate([vn_ref[...], pad], axis=0).T
    lane = lax.broadcasted_iota(jnp.int32, (HEAD_DIM, WINDOW), 1)
    is_new = lane >= WINDOW - t_new

    for kh in range(N_KV_HEADS):
        hs = slice(kh * HEAD_DIM, (kh + 1) * HEAD_DIM)
        q = q_ref[kh]
        k_t = jnp.concatenate([kc_ref[s, kh] for s in range(g)], axis=1)
        v_t = jnp.concatenate([vc_ref[s, kh] for s in range(g)], axis=1)
        s_c = jnp.dot(q, k_t.astype(BF16), preferred_element_type=F32) + bias_c_ref[kh]
        s_n = jnp.dot(q, kn_t[hs].astype(BF16), preferred_element_type=F32) + bias_n_ref[kh]
        sink = sink_ref[kh]
        m = jnp.maximum(jnp.maximum(jnp.max(s_c, axis=-1, keepdims=True),
                                    jnp.max(s_n, axis=-1, keepdims=True)), sink)
        p_c = jnp.exp(s_c - m)
        p_n = jnp.exp(s_n - m)
        den = (jnp.sum(p_c, axis=-1, keepdims=True) + jnp.sum(p_n, axis=-1, keepdims=True)
               + jnp.exp(sink - m))
        o = (lax.dot_general(p_c.astype(BF16), v_t.astype(BF16), nt, preferred_element_type=F32)
             + lax.dot_general(p_n.astype(BF16), vn_t[hs].astype(BF16), nt, preferred_element_type=F32))
        o_ref[kh] = o / den

        for s in range(g):
            shift_new = (WINDOW - t_new - s * t_new) % WINDOW
            k_new = pltpu.roll(kn_t[hs], shift_new, 1)
            v_new = pltpu.roll(vn_t[hs], shift_new, 1)
            kbuf_ref[s, kh] = jnp.where(is_new, k_new, pltpu.roll(kc_ref[s, kh], WINDOW - t_new, 1))
            vbuf_ref[s, kh] = jnp.where(is_new, v_new, pltpu.roll(vc_ref[s, kh], WINDOW - t_new, 1))


def _sample_attention(q_heads, cache_kt, cache_vt, k_new, v_new, bias_c, bias_n, sink_rows, *, t_new):
    g = SAMPLE_GROUP
    bsz = cache_kt.shape[0]
    rows_q = g * Q_PER_KV * t_new
    cache_spec = pl.BlockSpec((g, N_KV_HEADS, HEAD_DIM, WINDOW), lambda i: (i, 0, 0, 0))
    q_spec = pl.BlockSpec((N_KV_HEADS, rows_q, HEAD_DIM), lambda i: (0, i, 0))
    new_spec = pl.BlockSpec((g * t_new, D_KV), lambda i: (i, 0))
    return pl.pallas_call(
        _sample_attn_kernel,
        grid=(bsz // g,),
        in_specs=[q_spec, cache_spec, cache_spec, new_spec, new_spec,
                  _resident(bias_c), _resident(bias_n), _resident(sink_rows)],
        out_specs=(q_spec, cache_spec, cache_spec),
        out_shape=(jax.ShapeDtypeStruct(q_heads.shape, F32),
                   jax.ShapeDtypeStruct(cache_kt.shape, cache_kt.dtype),
                   jax.ShapeDtypeStruct(cache_vt.shape, cache_vt.dtype)),
        compiler_params=pltpu.CompilerParams(
            dimension_semantics=("parallel",), vmem_limit_bytes=VMEM_LIMIT),
        name="sample_attention",
    )(q_heads, cache_kt, cache_vt, k_new, v_new, bias_c, bias_n, sink_rows)


def _sample_bias(t_new):
    g = SAMPLE_GROUP
    slope = _alibi_slopes().reshape(N_KV_HEADS, 1, Q_PER_KV, 1, 1, 1)
    seq_q = np.arange(g).reshape(1, g, 1, 1, 1, 1)
    t = np.arange(t_new).reshape(1, 1, 1, t_new, 1, 1)
    seq_k = np.arange(g).reshape(1, 1, 1, 1, g, 1)
    j = np.arange(WINDOW).reshape(1, 1, 1, 1, 1, WINDOW)
    dist = t + WINDOW - j
    ok = (seq_q == seq_k) & (dist >= 0) & (dist < WINDOW)
    rows = g * Q_PER_KV * t_new
    bias_c = np.where(ok, -slope * dist.astype(np.float32), np.float32(MASKED))
    bias_c = bias_c.reshape(N_KV_HEADS, rows, g * WINDOW).astype(np.float32)
    tn = np.arange(t_new).reshape(1, 1, 1, 1, 1, t_new)
    dist_n = t - tn
    ok_n = (seq_q == seq_k) & (dist_n >= 0) & (dist_n < WINDOW)
    bias_n = np.where(ok_n, -slope * dist_n.astype(np.float32), np.float32(MASKED))
    bias_n = bias_n.reshape(N_KV_HEADS, rows, g * t_new)
    bias_n = np.pad(bias_n, ((0, 0), (0, 0), (0, LANES - g * t_new)), constant_values=MASKED)
    return bias_c, bias_n.astype(np.float32)


def _out_mlp_kernel(o_ref, x_ref, wo_ref, gffn_ref, wup_ref, wdown_ref, y_ref):
    x3 = x_ref[...] + jnp.dot(o_ref[...], wo_ref[...], preferred_element_type=F32)
    y_ref[...] = _mlp(x3, gffn_ref[...], wup_ref, wdown_ref)


def _out_mlp(o, x2, w_o, g_ffn, w_up, w_down, tm):
    rows = x2.shape[0]
    row = pl.BlockSpec((tm, D_MODEL), lambda i: (i, 0))
    return pl.pallas_call(
        _out_mlp_kernel,
        grid=(rows // tm,),
        in_specs=[row, row, _resident(w_o), _resident(g_ffn), _resident_layer(w_up, 1),
                  _resident_layer(w_down, 1)],
        out_specs=row,
        out_shape=jax.ShapeDtypeStruct((rows, D_MODEL), F32),
        compiler_params=pltpu.CompilerParams(
            dimension_semantics=("parallel",), vmem_limit_bytes=VMEM_LIMIT),
        name="out_mlp",
    )(o, x2, w_o, g_ffn, w_up, w_down)


def kernel(x_prompt, x_sample, state_ssm_re, state_ssm_im, cache_k, cache_v, norm_mix, norm_ffn,
           ssm_a_re, ssm_a_im, ssm_log_dt, ssm_b_re, ssm_b_im, ssm_c_re, ssm_c_im, ssm_d, ssm_w_glu,
           kv_norm, w_kv, k_norm, w_q, q_norm, attn_sinks, w_o, w_up, w_down):
    bsz, length, d = x_prompt.shape
    dec_b, dec_t, _ = x_sample.shape

    ab_re, ab_im, bb_re, bb_im = _s5_discretize(ssm_a_re[0], ssm_a_im[0], ssm_log_dt[0],
                                                ssm_b_re[0], ssm_b_im[0])
    b_blk, c_blk = _block_diag_weights(bb_re, bb_im, ssm_c_re[0], ssm_c_im[0])
    ab_re = ab_re.reshape(1, D_STATE)
    ab_im = ab_im.reshape(1, D_STATE)
    row = lambda v: v.reshape(1, -1).astype(F32)
    w_glu = ssm_w_glu[0].astype(BF16)
    w_up_b = _to_bf16(w_up)
    w_down_b = _to_bf16(w_down)
    w_kv_b = w_kv.astype(BF16)
    w_q0 = w_q[0].astype(BF16)
    w_o0 = w_o[0].astype(BF16)
    g_k = row(jnp.tile(k_norm, D_KV // HEAD_DIM))
    g_q = row(jnp.tile(q_norm[0], D_KV // HEAD_DIM))
    avg = jnp.asarray(np.kron(np.eye(D_KV // HEAD_DIM, dtype=np.float32),
                              np.full((HEAD_DIM, HEAD_DIM), 1.0 / HEAD_DIM, np.float32)), BF16)
    lane = np.arange(LANES)
    lmask = jnp.asarray(np.stack([lane < HEAD_DIM, lane >= HEAD_DIM]), BF16)
    sinks = row(attn_sinks[0])

    def layer0(x, h0r, h0i, tc, state_lanes_batch):
        return _s5_layer(x, h0r, h0i, row(norm_mix[0]), row(ssm_d[0]), ab_re, ab_im, b_blk, c_blk,
                         w_glu, tc=tc, state_lanes_batch=state_lanes_batch)

    def mlp_proj(x2d, tm):
        return _mlp_proj(x2d, tm, row(norm_ffn[0]), w_up_b, w_down_b, row(kv_norm), w_kv_b, g_k,
                         row(norm_mix[1]), w_q0, g_q, avg)

    zeros = jnp.zeros((bsz, D_STATE), F32)
    x1, hr_p, hi_p = layer0(x_prompt, zeros, zeros, 32, False)
    x2, q, kd, vd, k, v = mlp_proj(x1.reshape(bsz * length, d), 512)
    y_prompt = _prompt_attention(q, kd, vd, x2, jnp.asarray(_prompt_bias()), sinks, lmask, w_o0,
                                 row(norm_ffn[1]), w_up_b, w_down_b, bsz=bsz, length=length, tq=512)
    y_prompt = y_prompt.reshape(bsz, length, d)
    k_p = k.reshape(bsz, length, D_KV)[:, -WINDOW:].reshape(bsz, WINDOW, N_KV_HEADS, HEAD_DIM)
    v_p = v.reshape(bsz, length, D_KV)[:, -WINDOW:].reshape(bsz, WINDOW, N_KV_HEADS, HEAD_DIM)
    state_shape = (1, bsz, N_GROUPS, STATE_N)
    re_p = hr_p.reshape(state_shape).astype(state_ssm_re.dtype)
    im_p = hi_p.reshape(state_shape).astype(state_ssm_im.dtype)

    h0r_t = jnp.transpose(state_ssm_re[0], (1, 2, 0)).reshape(D_STATE, dec_b).astype(F32)
    h0i_t = jnp.transpose(state_ssm_im[0], (1, 2, 0)).reshape(D_STATE, dec_b).astype(F32)
    x1s, hr_s, hi_s = layer0(x_sample, h0r_t, h0i_t, dec_t, True)
    x2s, qs, _, _, ks, vs = mlp_proj(x1s.reshape(dec_b * dec_t, d), dec_b * dec_t)
    q_heads = qs.reshape(dec_b, dec_t, N_KV_HEADS, Q_PER_KV, HEAD_DIM).transpose(2, 0, 3, 1, 4)
    q_heads = q_heads.reshape(N_KV_HEADS, dec_b * Q_PER_KV * dec_t, HEAD_DIM)
    bias_c, bias_n = _sample_bias(dec_t)
    sink_rows = jnp.broadcast_to(
        attn_sinks[0].astype(F32).reshape(N_KV_HEADS, 1, Q_PER_KV, 1),
        (N_KV_HEADS, SAMPLE_GROUP, Q_PER_KV, dec_t)).reshape(N_KV_HEADS, -1, 1)
    o_heads, k_s_t, v_s_t = _sample_attention(
        q_heads, jnp.transpose(cache_k, (0, 2, 3, 1)), jnp.transpose(cache_v, (0, 2, 3, 1)),
        ks.astype(cache_k.dtype), vs.astype(cache_v.dtype), jnp.asarray(bias_c), jnp.asarray(bias_n),
        sink_rows, t_new=dec_t)
    o_s = o_heads.reshape(N_KV_HEADS, dec_b, Q_PER_KV, dec_t, HEAD_DIM).transpose(1, 3, 0, 2, 4)
    o_s = o_s.reshape(dec_b * dec_t, d).astype(BF16)
    y_sample = _out_mlp(o_s, x2s, w_o0, row(norm_ffn[1]), w_up_b, w_down_b, dec_b * dec_t)
    y_sample = y_sample.reshape(dec_b, dec_t, d)
    k_s = jnp.transpose(k_s_t, (0, 3, 1, 2))
    v_s = jnp.transpose(v_s_t, (0, 3, 1, 2))
    re_s = jnp.transpose(hr_s.reshape(N_GROUPS, STATE_N, dec_b), (2, 0, 1))[None].astype(state_ssm_re.dtype)
    im_s = jnp.transpose(hi_s.reshape(N_GROUPS, STATE_N, dec_b), (2, 0, 1))[None].astype(state_ssm_im.dtype)

    return (y_prompt, y_sample, re_p, im_p, k_p, v_p, re_s, im_s, k_s, v_s)
```

```python
import functools

import jax
import jax.numpy as jnp
import numpy as np
from jax import lax
from jax.experimental import pallas as pl
from jax.experimental.pallas import tpu as pltpu

F32 = jnp.float32
BF16 = jnp.bfloat16

D_MODEL = 1024
GROUP_SIZE = 16
N_GROUPS = D_MODEL // GROUP_SIZE
STATE_N = 64
D_STATE = N_GROUPS * STATE_N
HEAD_DIM = 64
N_HEADS = D_MODEL // HEAD_DIM
N_KV_HEADS = 4
Q_PER_KV = N_HEADS // N_KV_HEADS
D_KV = N_KV_HEADS * HEAD_DIM
WINDOW = 128
D_FF = 4 * D_MODEL
EPS = 1e-6
MASKED = -1e30

LANES = 128
GROUPS_PER_BLOCK = LANES // GROUP_SIZE
N_LANE_BLOCKS = D_MODEL // LANES
STATES_PER_BLOCK = GROUPS_PER_BLOCK * STATE_N
SUB_BATCH = 16
SCAN_LANES = 256
SAMPLE_GROUP = 8
MLP_PIECE = 256
VMEM_LIMIT = 56 * 1024 * 1024
CAST_BLOCK_BYTES = 4 * 1024 * 1024


def _resident(a):
    return pl.BlockSpec(a.shape, lambda *_: (0,) * a.ndim, pipeline_mode=pl.Buffered(1))


def _resident_layer(a, layer):
    return pl.BlockSpec((None,) + a.shape[1:], lambda *_: (layer,) + (0,) * (a.ndim - 1),
                        pipeline_mode=pl.Buffered(1))


def _rms(x, g):
    return x * lax.rsqrt(jnp.mean(x * x, axis=-1, keepdims=True) + EPS) * g


def _head_mean_sq(x, avg):
    return jnp.dot((x * x).astype(BF16), avg, preferred_element_type=F32)


def _cast_kernel(x_ref, o_ref):
    o_ref[...] = x_ref[...].astype(o_ref.dtype)


def _to_bf16(w):
    n, r, c = w.shape
    br = min(r, CAST_BLOCK_BYTES // (4 * c))
    spec = pl.BlockSpec((1, br, c), lambda i, j: (i, j, 0))
    return pl.pallas_call(
        _cast_kernel, grid=(n, r // br), in_specs=[spec], out_specs=spec,
        out_shape=jax.ShapeDtypeStruct(w.shape, BF16),
        compiler_params=pltpu.CompilerParams(dimension_semantics=("parallel", "parallel")),
        name="to_bf16",
    )(w)


def _discretize_kernel(a_re_ref, a_im_ref, log_dt_ref, b_re_ref, b_im_ref,
                       ab_re_ref, ab_im_ref, bb_re_ref, bb_im_ref):
    dt = jnp.exp(log_dt_ref[...])
    a_re = a_re_ref[...]
    a_im = a_im_ref[...]
    mag = jnp.exp(dt * a_re)
    ab_re = mag * jnp.cos(dt * a_im)
    ab_im = mag * jnp.sin(dt * a_im)
    den = a_re * a_re + a_im * a_im
    nr = ab_re - 1.0
    ni = ab_im
    f_re = (nr * a_re + ni * a_im) / den
    f_im = (ni * a_re - nr * a_im) / den
    ab_re_ref[...] = ab_re
    ab_im_ref[...] = ab_im
    b_re = b_re_ref[...]
    b_im = b_im_ref[...]
    fr = f_re[:, None, :]
    fi = f_im[:, None, :]
    bb_re_ref[...] = fr * b_re - fi * b_im
    bb_im_ref[...] = fr * b_im + fi * b_re


def _s5_discretize(a_re, a_im, log_dt, b_re, b_im):
    g, n = a_re.shape
    c = b_re.shape[-1]
    out = pl.pallas_call(
        _discretize_kernel,
        out_shape=(jax.ShapeDtypeStruct((g, n), F32), jax.ShapeDtypeStruct((g, n), F32),
                   jax.ShapeDtypeStruct((g, c, n), F32), jax.ShapeDtypeStruct((g, c, n), F32)),
        name="s5_discretize",
    )(a_re, a_im, log_dt.reshape(g, 1), jnp.swapaxes(b_re, 1, 2), jnp.swapaxes(b_im, 1, 2))
    return out


def _block_diag_weights(bb_re, bb_im, c_re, c_im):
    nb, gb = N_LANE_BLOCKS, GROUPS_PER_BLOCK
    eye = jnp.eye(gb, dtype=F32)
    bb = jnp.stack([bb_re, bb_im], axis=0).reshape(2, nb, gb, GROUP_SIZE, STATE_N)
    b_blk = jnp.einsum("rjgcn,gh->jgcrhn", bb, eye).reshape(nb, LANES, 2 * STATES_PER_BLOCK)
    cc = jnp.stack([c_re, -c_im], axis=0).reshape(2, nb, gb, GROUP_SIZE, STATE_N)
    c_blk = jnp.einsum("rjgcn,gh->jrgnhc", cc, eye).reshape(nb, 2 * STATES_PER_BLOCK, LANES)
    return b_blk.astype(BF16), c_blk.astype(BF16)


def _s5_layer_kernel(x_hbm, h0r_ref, h0i_ref, g_ref, d_ref, abr_ref, abi_ref, bblk_ref, cblk_ref,
                     wglu_ref, o_hbm, hro_ref, hio_ref, xbuf, obuf, in_sem, out_sem, hr_sc, hi_sc,
                     bur, bui, xsr, xsi, hn_sc, *, state_lanes_batch):
    _, tc, bb, d = xbuf.shape
    rows = tc * bb
    spb = STATES_PER_BLOCK
    step_idx = pl.program_id(0)
    n_steps = pl.num_programs(0)
    slot = step_idx % 2

    def in_copy(step, sl, b):
        return pltpu.make_async_copy(x_hbm.at[b, pl.ds(step * tc, tc), :], xbuf.at[sl, :, b, :],
                                     in_sem.at[sl])

    def out_copy(step, sl, b):
        return pltpu.make_async_copy(obuf.at[sl, :, b, :], o_hbm.at[b, pl.ds(step * tc, tc), :],
                                     out_sem.at[sl])

    @pl.when(step_idx == 0)
    def _():
        for b in range(bb):
            in_copy(0, 0, b).start()
        if state_lanes_batch:
            hr_sc[...] = h0r_ref[...].T
            hi_sc[...] = h0i_ref[...].T
        else:
            hr_sc[...] = h0r_ref[...]
            hi_sc[...] = h0i_ref[...]

    @pl.when(step_idx + 1 < n_steps)
    def _():
        for b in range(bb):
            in_copy(step_idx + 1, 1 - slot, b).start()

    for b in range(bb):
        in_copy(step_idx, slot, b).wait()

    hn = _rms(xbuf[slot].reshape(rows, d), g_ref[...])
    hn_sc[...] = hn
    hnb = hn.astype(BF16)

    def input_map(j):
        bu = jnp.dot(hnb[:, j * LANES:(j + 1) * LANES], bblk_ref[j], preferred_element_type=F32)
        bur[j] = bu[:, :spb]
        bui[j] = bu[:, spb:]

    def recurrence(j):
        for half in range(spb // SCAN_LANES):
            cs = slice(half * SCAN_LANES, (half + 1) * SCAN_LANES)
            gs = slice(j * spb + half * SCAN_LANES, j * spb + (half + 1) * SCAN_LANES)
            ar = jnp.broadcast_to(abr_ref[:, gs], (SUB_BATCH, SCAN_LANES))
            ai = jnp.broadcast_to(abi_ref[:, gs], (SUB_BATCH, SCAN_LANES))
            for sb in range(bb // SUB_BATCH):
                seqs = slice(sb * SUB_BATCH, (sb + 1) * SUB_BATCH)
                h_re = hr_sc[seqs, gs]
                h_im = hi_sc[seqs, gs]
                for t in range(tc):
                    rs = slice(t * bb + sb * SUB_BATCH, t * bb + (sb + 1) * SUB_BATCH)
                    n_re = ar * h_re - ai * h_im + bur[j, rs, cs]
                    n_im = ar * h_im + ai * h_re + bui[j, rs, cs]
                    xsr[j, rs, cs] = n_re.astype(BF16)
                    xsi[j, rs, cs] = n_im.astype(BF16)
                    h_re, h_im = n_re, n_im
                hr_sc[seqs, gs] = h_re
                hi_sc[seqs, gs] = h_im

    def output_map(j):
        ls = slice(j * LANES, (j + 1) * LANES)
        y = (jnp.dot(xsr[j], cblk_ref[j, :spb, :], preferred_element_type=F32)
             + jnp.dot(xsi[j], cblk_ref[j, spb:, :], preferred_element_type=F32))
        y = y + d_ref[:, ls] * hn_sc[:, ls]
        return jax.nn.gelu(y).astype(BF16)

    gelu_cols = []
    input_map(0)
    for j in range(N_LANE_BLOCKS):
        if j + 1 < N_LANE_BLOCKS:
            input_map(j + 1)
        recurrence(j)
        if j > 0:
            gelu_cols.append(output_map(j - 1))
    gelu_cols.append(output_map(N_LANE_BLOCKS - 1))

    z = jnp.dot(jnp.concatenate(gelu_cols, axis=1), wglu_ref[...], preferred_element_type=F32)
    mix = z[:, :d] * jax.nn.sigmoid(z[:, d:])

    @pl.when(step_idx >= 2)
    def _():
        for b in range(bb):
            out_copy(step_idx - 2, slot, b).wait()

    obuf[slot] = (xbuf[slot].reshape(rows, d) + mix).reshape(tc, bb, d)
    for b in range(bb):
        out_copy(step_idx, slot, b).start()

    @pl.when(step_idx == n_steps - 1)
    def _():
        @pl.when(n_steps > 1)
        def _():
            for b in range(bb):
                out_copy(step_idx - 1, 1 - slot, b).wait()
        for b in range(bb):
            out_copy(step_idx, slot, b).wait()
        if state_lanes_batch:
            hro_ref[...] = hr_sc[...].T
            hio_ref[...] = hi_sc[...].T
        else:
            hro_ref[...] = hr_sc[...]
            hio_ref[...] = hi_sc[...]


def _s5_layer(x, h0r, h0i, g_mix, d_skip, ab_re, ab_im, b_blk, c_blk, w_glu, *, tc, state_lanes_batch):
    bsz, length, d = x.shape
    rows = tc * bsz
    weights = (g_mix, d_skip, ab_re, ab_im, b_blk, c_blk, w_glu)
    state_shape = jax.ShapeDtypeStruct(h0r.shape, F32)
    block_shape = (N_LANE_BLOCKS, rows, STATES_PER_BLOCK)
    return pl.pallas_call(
        functools.partial(_s5_layer_kernel, state_lanes_batch=state_lanes_batch),
        grid=(length // tc,),
        in_specs=[pl.BlockSpec(memory_space=pl.ANY), _resident(h0r), _resident(h0i)]
        + [_resident(w) for w in weights],
        out_specs=(pl.BlockSpec(memory_space=pl.ANY), pl.BlockSpec(h0r.shape, lambda i: (0, 0)),
                   pl.BlockSpec(h0i.shape, lambda i: (0, 0))),
        out_shape=(jax.ShapeDtypeStruct((bsz, length, d), F32), state_shape, state_shape),
        scratch_shapes=[pltpu.VMEM((2, tc, bsz, d), F32), pltpu.VMEM((2, tc, bsz, d), F32),
                        pltpu.SemaphoreType.DMA((2,)), pltpu.SemaphoreType.DMA((2,)),
                        pltpu.VMEM((bsz, D_STATE), F32), pltpu.VMEM((bsz, D_STATE), F32),
                        pltpu.VMEM(block_shape, F32), pltpu.VMEM(block_shape, F32),
                        pltpu.VMEM(block_shape, BF16), pltpu.VMEM(block_shape, BF16),
                        pltpu.VMEM((rows, d), F32)],
        compiler_params=pltpu.CompilerParams(
            dimension_semantics=("arbitrary",), vmem_limit_bytes=VMEM_LIMIT),
        name="s5_layer",
    )(x, h0r, h0i, *weights)


def _mlp_schedule(n_chunks, n_col):
    schedule = [("up", 0, n) for n in range(n_col)]
    for c in range(n_chunks):
        if c + 1 < n_chunks:
            schedule += [("up", c + 1, n) for n in range(n_col)]
        schedule += [("down", c, n) for n in range(n_col)]
    return schedule


def _mlp_piece(state, kind, c, n, w_up_ref, w_down_ref):
    if kind == "up":
        hs = slice(c * D_MODEL + n * MLP_PIECE, c * D_MODEL + (n + 1) * MLP_PIECE)
        u = jnp.dot(state["hn"], w_up_ref[:, hs], preferred_element_type=F32)
        state.setdefault(("u", c), []).append(jnp.square(jnp.maximum(u, 0.0)).astype(BF16))
    else:
        if n == 0:
            state[("uc", c)] = jnp.concatenate(state.pop(("u", c)), axis=1)
        cs = slice(n * MLP_PIECE, (n + 1) * MLP_PIECE)
        state["acc"][n] = state["acc"][n] + jnp.dot(
            state[("uc", c)], w_down_ref[c * D_MODEL:(c + 1) * D_MODEL, cs], preferred_element_type=F32)


def _mlp(x, g, w_up_ref, w_down_ref):
    n_col = D_MODEL // MLP_PIECE
    state = {"hn": _rms(x, g).astype(BF16),
             "acc": [x[:, n * MLP_PIECE:(n + 1) * MLP_PIECE] for n in range(n_col)]}
    for piece in _mlp_schedule(w_up_ref.shape[1] // D_MODEL, n_col):
        _mlp_piece(state, *piece, w_up_ref, w_down_ref)
    return jnp.concatenate(state["acc"], axis=1)


def _mlp_proj_kernel(x_ref, gffn_ref, wup_ref, wdown_ref, gkv_ref, wkv_ref, gk_ref, gq1_ref, wq_ref,
                     gq_ref, avg_ref, x2_ref, q_ref, kd_ref, vd_ref, k_ref, v_ref, x2_sc):
    tm = x_ref.shape[0]
    step = pl.program_id(0)
    slot = step % 2

    @pl.when(step == 0)
    def _():
        x2_sc[1] = jnp.zeros(x2_sc.shape[1:], x2_sc.dtype)

    n_col = D_MODEL // MLP_PIECE
    x = x_ref[...]
    mlp = {"hn": _rms(x, gffn_ref[...]).astype(BF16),
           "acc": [x[:, n * MLP_PIECE:(n + 1) * MLP_PIECE] for n in range(n_col)]}
    schedule = _mlp_schedule(wup_ref.shape[1] // D_MODEL, n_col)

    prev = x2_sc[1 - slot]
    avg = avg_ref[...]
    lane = lax.broadcasted_iota(jnp.int32, (tm, LANES), 1)
    proj = {}

    def kv_matmul():
        normed = prev * lax.rsqrt(jnp.mean(prev * prev, axis=-1, keepdims=True) + EPS)
        proj["hq"] = (normed * gq1_ref[...]).astype(BF16)
        proj["kv"] = jnp.dot((normed * gkv_ref[...]).astype(BF16), wkv_ref[...],
                             preferred_element_type=F32)

    def kv_finish():
        k = proj["kv"][:, :D_KV]
        k = k * lax.rsqrt(_head_mean_sq(k, avg) + EPS) * gk_ref[...]
        v = proj["kv"][:, D_KV:]
        k_ref[...] = k
        v_ref[...] = v
        for src, dst_ref in ((k, kd_ref), (v, vd_ref)):
            for blk in range(D_KV // LANES):
                a = src[:, blk * LANES:(blk + 1) * LANES]
                swapped = pltpu.roll(a, HEAD_DIM, 1)
                dst_ref[:, 2 * blk * LANES:(2 * blk + 1) * LANES] = (
                    jnp.where(lane < HEAD_DIM, a, swapped).astype(BF16))
                dst_ref[:, (2 * blk + 1) * LANES:(2 * blk + 2) * LANES] = (
                    jnp.where(lane < HEAD_DIM, swapped, a).astype(BF16))

    def q_matmul(c):
        proj[("q", c)] = jnp.dot(proj["hq"], wq_ref[:, c * D_KV:(c + 1) * D_KV],
                                 preferred_element_type=F32)

    def q_finish(c):
        q = proj.pop(("q", c))
        q = q * lax.rsqrt(_head_mean_sq(q, avg) + EPS) * gq_ref[...]
        q_ref[:, c * D_KV:(c + 1) * D_KV] = (q * (HEAD_DIM ** -0.5)).astype(BF16)

    n_q = D_MODEL // D_KV
    epilogue = [kv_matmul, kv_finish, functools.partial(q_matmul, 0)]
    for c in range(n_q):
        if c + 1 < n_q:
            epilogue.append(functools.partial(q_matmul, c + 1))
        epilogue.append(functools.partial(q_finish, c))

    every = len(schedule) // len(epilogue)
    for i, piece in enumerate(schedule):
        _mlp_piece(mlp, *piece, wup_ref, wdown_ref)
        if i % every == every - 1 and epilogue:
            epilogue.pop(0)()
    for stage in epilogue:
        stage()

    x2 = jnp.concatenate(mlp["acc"], axis=1)
    x2_ref[...] = x2
    x2_sc[slot] = x2


def _mlp_proj(x2d, tm, g_ffn, w_up, w_down, g_kv, w_kv, g_k, g_q1, w_q, g_q, avg):
    rows = x2d.shape[0]
    n_tiles = rows // tm
    cur = lambda width: pl.BlockSpec((tm, width), lambda s: (jnp.minimum(s, n_tiles - 1), 0))
    prev = lambda width: pl.BlockSpec((tm, width), lambda s: (jnp.maximum(s - 1, 0), 0))
    tail = (g_kv, w_kv, g_k, g_q1, w_q, g_q, avg)
    return pl.pallas_call(
        _mlp_proj_kernel,
        grid=(n_tiles + 1,),
        in_specs=[cur(D_MODEL), _resident(g_ffn), _resident_layer(w_up, 0), _resident_layer(w_down, 0)]
        + [_resident(w) for w in tail],
        out_specs=(cur(D_MODEL), prev(D_MODEL), prev(2 * D_KV), prev(2 * D_KV), prev(D_KV), prev(D_KV)),
        out_shape=(jax.ShapeDtypeStruct((rows, D_MODEL), F32),
                   jax.ShapeDtypeStruct((rows, D_MODEL), BF16),
                   jax.ShapeDtypeStruct((rows, 2 * D_KV), BF16),
                   jax.ShapeDtypeStruct((rows, 2 * D_KV), BF16),
                   jax.ShapeDtypeStruct((rows, D_KV), F32),
                   jax.ShapeDtypeStruct((rows, D_KV), F32)),
        scratch_shapes=[pltpu.VMEM((2, tm, D_MODEL), F32)],
        compiler_params=pltpu.CompilerParams(
            dimension_semantics=("arbitrary",), vmem_limit_bytes=VMEM_LIMIT),
        name="mlp_proj",
    )(x2d, g_ffn, w_up, w_down, *tail)


def _prompt_attn_kernel(q_ref, kd_ref, vd_ref, kdp_ref, vdp_ref, x_ref, bias_ref, sink_ref, lmask_ref,
                        wo_ref, gffn_ref, wup_ref, wdown_ref, y_ref, o_sc, *, tiles_per_seq, n_tiles):
    tq = q_ref.shape[0]
    step = pl.program_id(0)
    slot = step % 2

    @pl.when(step == 0)
    def _():
        o_sc[1] = jnp.zeros(o_sc.shape[1:], o_sc.dtype)

    tile = jnp.minimum(step, n_tiles - 1)
    first = (tile % tiles_per_seq == 0).astype(jnp.int32)
    m_lo = lmask_ref[0:1, :]
    m_hi = lmask_ref[1:2, :]
    ones_lo = jnp.broadcast_to(m_lo, (2 * WINDOW, LANES))
    ones_hi = jnp.broadcast_to(m_hi, (2 * WINDOW, LANES))
    nt = (((1,), (1,)), ((), ()))
    lane = lax.broadcasted_iota(jnp.int32, (WINDOW, LANES), 1)
    pairs_per_kv = Q_PER_KV // 2
    windows = {}

    def window(r, kh):
        if (r, kh) not in windows:
            ls = slice(kh * LANES, (kh + 1) * LANES)
            if r == 0:
                kwin = jnp.concatenate([kdp_ref[:, ls], kd_ref[0:WINDOW, ls]], axis=0)
                vwin = jnp.concatenate([vdp_ref[:, ls], vd_ref[0:WINDOW, ls]], axis=0)
            else:
                kwin = kd_ref[(r - 1) * WINDOW:(r + 1) * WINDOW, ls]
                vwin = vd_ref[(r - 1) * WINDOW:(r + 1) * WINDOW, ls]
            windows[(r, kh)] = (kwin * m_lo, kwin * m_hi,
                                jnp.concatenate([vwin * m_lo, ones_lo], axis=1),
                                jnp.concatenate([vwin * m_hi, ones_hi], axis=1))
        return windows[(r, kh)]

    def scores(p):
        r, blk = divmod(p, N_LANE_BLOCKS)
        k_lo, k_hi, _, _ = window(r, blk // pairs_per_kv)
        sel = first if r == 0 else 0
        qp = q_ref[r * WINDOW:(r + 1) * WINDOW, blk * LANES:(blk + 1) * LANES]
        s0 = lax.dot_general(qp, k_lo, nt, preferred_element_type=F32) + bias_ref[sel, 2 * blk]
        s1 = lax.dot_general(qp, k_hi, nt, preferred_element_type=F32) + bias_ref[sel, 2 * blk + 1]
        return s0, s1

    def finish(p, s0, s1):
        r, blk = divmod(p, N_LANE_BLOCKS)
        _, _, v_lo, v_hi = window(r, blk // pairs_per_kv)
        h0 = 2 * blk
        m0 = jnp.maximum(jnp.max(s0, axis=-1, keepdims=True), sink_ref[:, h0:h0 + 1])
        m1 = jnp.maximum(jnp.max(s1, axis=-1, keepdims=True), sink_ref[:, h0 + 1:h0 + 2])
        p0 = jnp.exp(s0 - m0).astype(BF16)
        p1 = jnp.exp(s1 - m1).astype(BF16)
        ext = (jnp.dot(p0, v_lo, preferred_element_type=F32)
               + jnp.dot(p1, v_hi, preferred_element_type=F32))
        e0 = jnp.exp(sink_ref[:, h0:h0 + 1] - m0)
        e1 = jnp.exp(sink_ref[:, h0 + 1:h0 + 2] - m1)
        den = ext[:, LANES:] + jnp.where(lane < HEAD_DIM, e0, e1)
        o_sc[slot, r * WINDOW:(r + 1) * WINDOW, blk * LANES:(blk + 1) * LANES] = (
            (ext[:, :LANES] / den).astype(BF16))

    n_col = D_MODEL // MLP_PIECE
    n_chunks = wup_ref.shape[1] // D_MODEL
    o_prev = o_sc[1 - slot]
    state = {}

    def wo_piece(n):
        cs = slice(n * MLP_PIECE, (n + 1) * MLP_PIECE)
        state.setdefault("acc", []).append(
            x_ref[:, cs] + jnp.dot(o_prev, wo_ref[:, cs], preferred_element_type=F32))
        if n == n_col - 1:
            state["hn"] = _rms(jnp.concatenate(state["acc"], axis=1), gffn_ref[...]).astype(BF16)

    schedule = ([functools.partial(wo_piece, n) for n in range(n_col)]
                + [functools.partial(_mlp_piece, state, *p, wup_ref, wdown_ref)
                   for p in _mlp_schedule(n_chunks, n_col)])

    n_pairs = (tq // WINDOW) * N_LANE_BLOCKS
    pending = scores(0)
    for k in range(max(len(schedule), n_pairs)):
        nxt = scores(k + 1) if k + 1 < n_pairs else None
        if k < len(schedule):
            schedule[k]()
        if k < n_pairs:
            finish(k, *pending)
        pending = nxt
    y_ref[...] = jnp.concatenate(state["acc"], axis=1)


def _prompt_attention(q, kd, vd, x2, bias, sinks, lmask, w_o, g_ffn, w_up, w_down, *, bsz, length, tq):
    nt = length // tq
    n_tiles = bsz * nt
    per_win = tq // WINDOW
    att_tile = lambda s: jnp.minimum(s, n_tiles - 1)
    mlp_tile = lambda s: jnp.maximum(s - 1, 0)
    att = lambda width: pl.BlockSpec((tq, width), lambda s: (att_tile(s), 0))
    mlp = pl.BlockSpec((tq, D_MODEL), lambda s: (mlp_tile(s), 0))
    prev = pl.BlockSpec((WINDOW, 2 * D_KV), lambda s: (jnp.maximum(att_tile(s) * per_win - 1, 0), 0))
    return pl.pallas_call(
        functools.partial(_prompt_attn_kernel, tiles_per_seq=nt, n_tiles=n_tiles),
        grid=(n_tiles + 1,),
        in_specs=[att(D_MODEL), att(2 * D_KV), att(2 * D_KV), prev, prev, mlp,
                  _resident(bias), _resident(sinks), _resident(lmask), _resident(w_o), _resident(g_ffn),
                  _resident_layer(w_up, 1), _resident_layer(w_down, 1)],
        out_specs=mlp,
        out_shape=jax.ShapeDtypeStruct((bsz * length, D_MODEL), F32),
        scratch_shapes=[pltpu.VMEM((2, tq, D_MODEL), BF16)],
        compiler_params=pltpu.CompilerParams(
            dimension_semantics=("arbitrary",), vmem_limit_bytes=VMEM_LIMIT),
        name="prompt_attention",
    )(q, kd, vd, kd, vd, x2, bias, sinks, lmask, w_o, g_ffn, w_up, w_down)


def _alibi_slopes():
    return (2.0 ** (-8.0 * np.arange(1, N_HEADS + 1, dtype=np.float32) / N_HEADS)).astype(np.float32)


def _prompt_bias():
    qi = np.arange(WINDOW)[:, None]
    kj = np.arange(2 * WINDOW)[None, :]
    dist = qi - kj + WINDOW
    valid = (dist >= 0) & (dist < WINDOW)
    valid = np.stack([valid, valid & (kj >= WINDOW)], axis=0)[:, None]
    pen = -_alibi_slopes()[:, None, None] * dist.astype(np.float32)[None]
    return np.where(valid, pen[None], np.float32(MASKED)).astype(np.float32)


def _sample_attn_kernel(q_ref, kc_ref, vc_ref, kn_ref, vn_ref, bias_c_ref, bias_n_ref, sink_ref,
                        o_ref, kbuf_ref, vbuf_ref):
    g = kc_ref.shape[0]
    t_new = kn_ref.shape[0] // g
    nt = (((1,), (1,)), ((), ()))
    pad = jnp.zeros((LANES - g * t_new, D_KV), F32)
    kn_t = jnp.concatenate([kn_ref[...], pad], axis=0).T
    vn_t = jnp.concatenate([vn_ref[...], pad], axis=0).T
    lane = lax.broadcasted_iota(jnp.int32, (HEAD_DIM, WINDOW), 1)
    is_new = lane >= WINDOW - t_new

    for kh in range(N_KV_HEADS):
        hs = slice(kh * HEAD_DIM, (kh + 1) * HEAD_DIM)
        q = q_ref[kh]
        k_t = jnp.concatenate([kc_ref[s, kh] for s in range(g)], axis=1)
        v_t = jnp.concatenate([vc_ref[s, kh] for s in range(g)], axis=1)
        s_c = jnp.dot(q, k_t.astype(BF16), preferred_element_type=F32) + bias_c_ref[kh]
        s_n = jnp.dot(q, kn_t[hs].astype(BF16), preferred_element_type=F32) + bias_n_ref[kh]
        sink = sink_ref[kh]
        m = jnp.maximum(jnp.maximum(jnp.max(s_c, axis=-1, keepdims=True),
                                    jnp.max(s_n, axis=-1, keepdims=True)), sink)
        p_c = jnp.exp(s_c - m)
        p_n = jnp.exp(s_n - m)
        den = (jnp.sum(p_c, axis=-1, keepdims=True) + jnp.sum(p_n, axis=-1, keepdims=True)
               + jnp.exp(sink - m))
        o = (lax.dot_general(p_c.astype(BF16), v_t.astype(BF16), nt, preferred_element_type=F32)
             + lax.dot_general(p_n.astype(BF16), vn_t[hs].astype(BF16), nt, preferred_element_type=F32))
        o_ref[kh] = o / den

        for s in range(g):
            shift_new = (WINDOW - t_new - s * t_new) % WINDOW
            k_new = pltpu.roll(kn_t[hs], shift_new, 1)
            v_new = pltpu.roll(vn_t[hs], shift_new, 1)
            kbuf_ref[s, kh] = jnp.where(is_new, k_new, pltpu.roll(kc_ref[s, kh], WINDOW - t_new, 1))
            vbuf_ref[s, kh] = jnp.where(is_new, v_new, pltpu.roll(vc_ref[s, kh], WINDOW - t_new, 1))


def _sample_attention(q_heads, cache_kt, cache_vt, k_new, v_new, bias_c, bias_n, sink_rows, *, t_new):
    g = SAMPLE_GROUP
    bsz = cache_kt.shape[0]
    rows_q = g * Q_PER_KV * t_new
    cache_spec = pl.BlockSpec((g, N_KV_HEADS, HEAD_DIM, WINDOW), lambda i: (i, 0, 0, 0))
    q_spec = pl.BlockSpec((N_KV_HEADS, rows_q, HEAD_DIM), lambda i: (0, i, 0))
    new_spec = pl.BlockSpec((g * t_new, D_KV), lambda i: (i, 0))
    return pl.pallas_call(
        _sample_attn_kernel,
        grid=(bsz // g,),
        in_specs=[q_spec, cache_spec, cache_spec, new_spec, new_spec,
                  _resident(bias_c), _resident(bias_n), _resident(sink_rows)],
        out_specs=(q_spec, cache_spec, cache_spec),
        out_shape=(jax.ShapeDtypeStruct(q_heads.shape, F32),
                   jax.ShapeDtypeStruct(cache_kt.shape, cache_kt.dtype),
                   jax.ShapeDtypeStruct(cache_vt.shape, cache_vt.dtype)),
        compiler_params=pltpu.CompilerParams(
            dimension_semantics=("parallel",), vmem_limit_bytes=VMEM_LIMIT),
        name="sample_attention",
    )(q_heads, cache_kt, cache_vt, k_new, v_new, bias_c, bias_n, sink_rows)


def _sample_bias(t_new):
    g = SAMPLE_GROUP
    slope = _alibi_slopes().reshape(N_KV_HEADS, 1, Q_PER_KV, 1, 1, 1)
    seq_q = np.arange(g).reshape(1, g, 1, 1, 1, 1)
    t = np.arange(t_new).reshape(1, 1, 1, t_new, 1, 1)
    seq_k = np.arange(g).reshape(1, 1, 1, 1, g, 1)
    j = np.arange(WINDOW).reshape(1, 1, 1, 1, 1, WINDOW)
    dist = t + WINDOW - j
    ok = (seq_q == seq_k) & (dist >= 0) & (dist < WINDOW)
    rows = g * Q_PER_KV * t_new
    bias_c = np.where(ok, -slope * dist.astype(np.float32), np.float32(MASKED))
    bias_c = bias_c.reshape(N_KV_HEADS, rows, g * WINDOW).astype(np.float32)
    tn = np.arange(t_new).reshape(1, 1, 1, 1, 1, t_new)
    dist_n = t - tn
    ok_n = (seq_q == seq_k) & (dist_n >= 0) & (dist_n < WINDOW)
    bias_n = np.where(ok_n, -slope * dist_n.astype(np.float32), np.float32(MASKED))
    bias_n = bias_n.reshape(N_KV_HEADS, rows, g * t_new)
    bias_n = np.pad(bias_n, ((0, 0), (0, 0), (0, LANES - g * t_new)), constant_values=MASKED)
    return bias_c, bias_n.astype(np.float32)


def _out_mlp_kernel(o_ref, x_ref, wo_ref, gffn_ref, wup_ref, wdown_ref, y_ref):
    x3 = x_ref[...] + jnp.dot(o_ref[...], wo_ref[...], preferred_element_type=F32)
    y_ref[...] = _mlp(x3, gffn_ref[...], wup_ref, wdown_ref)


def _out_mlp(o, x2, w_o, g_ffn, w_up, w_down, tm):
    rows = x2.shape[0]
    row = pl.BlockSpec((tm, D_MODEL), lambda i: (i, 0))
    return pl.pallas_call(
        _out_mlp_kernel,
        grid=(rows // tm,),
        in_specs=[row, row, _resident(w_o), _resident(g_ffn), _resident_layer(w_up, 1),
                  _resident_layer(w_down, 1)],
        out_specs=row,
        out_shape=jax.ShapeDtypeStruct((rows, D_MODEL), F32),
        compiler_params=pltpu.CompilerParams(
            dimension_semantics=("parallel",), vmem_limit_bytes=VMEM_LIMIT),
        name="out_mlp",
    )(o, x2, w_o, g_ffn, w_up, w_down)


def kernel(x_prompt, x_sample, state_ssm_re, state_ssm_im, cache_k, cache_v, norm_mix, norm_ffn,
           ssm_a_re, ssm_a_im, ssm_log_dt, ssm_b_re, ssm_b_im, ssm_c_re, ssm_c_im, ssm_d, ssm_w_glu,
           kv_norm, w_kv, k_norm, w_q, q_norm, attn_sinks, w_o, w_up, w_down):
    bsz, length, d = x_prompt.shape
    dec_b, dec_t, _ = x_sample.shape

    ab_re, ab_im, bb_re, bb_im = _s5_discretize(ssm_a_re[0], ssm_a_im[0], ssm_log_dt[0],
                                                ssm_b_re[0], ssm_b_im[0])
    b_blk, c_blk = _block_diag_weights(bb_re, bb_im, ssm_c_re[0], ssm_c_im[0])
    ab_re = ab_re.reshape(1, D_STATE)
    ab_im = ab_im.reshape(1, D_STATE)
    row = lambda v: v.reshape(1, -1).astype(F32)
    w_glu = ssm_w_glu[0].astype(BF16)
    w_up_b = _to_bf16(w_up)
    w_down_b = _to_bf16(w_down)
    w_kv_b = w_kv.astype(BF16)
    w_q0 = w_q[0].astype(BF16)
    w_o0 = w_o[0].astype(BF16)
    g_k = row(jnp.tile(k_norm, D_KV // HEAD_DIM))
    g_q = row(jnp.tile(q_norm[0], D_KV // HEAD_DIM))
    avg = jnp.asarray(np.kron(np.eye(D_KV // HEAD_DIM, dtype=np.float32),
                              np.full((HEAD_DIM, HEAD_DIM), 1.0 / HEAD_DIM, np.float32)), BF16)
    lane = np.arange(LANES)
    lmask = jnp.asarray(np.stack([lane < HEAD_DIM, lane >= HEAD_DIM]), BF16)
    sinks = row(attn_sinks[0])

    def layer0(x, h0r, h0i, tc, state_lanes_batch):
        return _s5_layer(x, h0r, h0i, row(norm_mix[0]), row(ssm_d[0]), ab_re, ab_im, b_blk, c_blk,
                         w_glu, tc=tc, state_lanes_batch=state_lanes_batch)

    def mlp_proj(x2d, tm):
        return _mlp_proj(x2d, tm, row(norm_ffn[0]), w_up_b, w_down_b, row(kv_norm), w_kv_b, g_k,
                         row(norm_mix[1]), w_q0, g_q, avg)

    zeros = jnp.zeros((bsz, D_STATE), F32)
    x1, hr_p, hi_p = layer0(x_prompt, zeros, zeros, 32, False)
    x2, q, kd, vd, k, v = mlp_proj(x1.reshape(bsz * length, d), 512)
    y_prompt = _prompt_attention(q, kd, vd, x2, jnp.asarray(_prompt_bias()), sinks, lmask, w_o0,
                                 row(norm_ffn[1]), w_up_b, w_down_b, bsz=bsz, length=length, tq=512)
    y_prompt = y_prompt.reshape(bsz, length, d)
    k_p = k.reshape(bsz, length, D_KV)[:, -WINDOW:].reshape(bsz, WINDOW, N_KV_HEADS, HEAD_DIM)
    v_p = v.reshape(bsz, length, D_KV)[:, -WINDOW:].reshape(bsz, WINDOW, N_KV_HEADS, HEAD_DIM)
    state_shape = (1, bsz, N_GROUPS, STATE_N)
    re_p = hr_p.reshape(state_shape).astype(state_ssm_re.dtype)
    im_p = hi_p.reshape(state_shape).astype(state_ssm_im.dtype)

    h0r_t = jnp.transpose(state_ssm_re[0], (1, 2, 0)).reshape(D_STATE, dec_b).astype(F32)
    h0i_t = jnp.transpose(state_ssm_im[0], (1, 2, 0)).reshape(D_STATE, dec_b).astype(F32)
    x1s, hr_s, hi_s = layer0(x_sample, h0r_t, h0i_t, dec_t, True)
    x2s, qs, _, _, ks, vs = mlp_proj(x1s.reshape(dec_b * dec_t, d), dec_b * dec_t)
    q_heads = qs.reshape(dec_b, dec_t, N_KV_HEADS, Q_PER_KV, HEAD_DIM).transpose(2, 0, 3, 1, 4)
    q_heads = q_heads.reshape(N_KV_HEADS, dec_b * Q_PER_KV * dec_t, HEAD_DIM)
    bias_c, bias_n = _sample_bias(dec_t)
    sink_rows = jnp.broadcast_to(
        attn_sinks[0].astype(F32).reshape(N_KV_HEADS, 1, Q_PER_KV, 1),
        (N_KV_HEADS, SAMPLE_GROUP, Q_PER_KV, dec_t)).reshape(N_KV_HEADS, -1, 1)
    o_heads, k_s_t, v_s_t = _sample_attention(
        q_heads, jnp.transpose(cache_k, (0, 2, 3, 1)), jnp.transpose(cache_v, (0, 2, 3, 1)),
        ks.astype(cache_k.dtype), vs.astype(cache_v.dtype), jnp.asarray(bias_c), jnp.asarray(bias_n),
        sink_rows, t_new=dec_t)
    o_s = o_heads.reshape(N_KV_HEADS, dec_b, Q_PER_KV, dec_t, HEAD_DIM).transpose(1, 3, 0, 2, 4)
    o_s = o_s.reshape(dec_b * dec_t, d).astype(BF16)
    y_sample = _out_mlp(o_s, x2s, w_o0, row(norm_ffn[1]), w_up_b, w_down_b, dec_b * dec_t)
    y_sample = y_sample.reshape(dec_b, dec_t, d)
    k_s = jnp.transpose(k_s_t, (0, 3, 1, 2))
    v_s = jnp.transpose(v_s_t, (0, 3, 1, 2))
    re_s = jnp.transpose(hr_s.reshape(N_GROUPS, STATE_N, dec_b), (2, 0, 1))[None].astype(state_ssm_re.dtype)
    im_s = jnp.transpose(hi_s.reshape(N_GROUPS, STATE_N, dec_b), (2, 0, 1))[None].astype(state_ssm_im.dtype)

    return (y_prompt, y_sample, re_p, im_p, k_p, v_p, re_s, im_s, k_s, v_s)
```

```python
import functools

import jax
import jax.numpy as jnp
import numpy as np
from jax import lax
from jax.experimental import pallas as pl
from jax.experimental.pallas import tpu as pltpu

F32 = jnp.float32
BF16 = jnp.bfloat16

D_MODEL = 1024
GROUP_SIZE = 16
N_GROUPS = D_MODEL // GROUP_SIZE
STATE_N = 64
D_STATE = N_GROUPS * STATE_N
HEAD_DIM = 64
N_HEADS = D_MODEL // HEAD_DIM
N_KV_HEADS = 4
Q_PER_KV = N_HEADS // N_KV_HEADS
D_KV = N_KV_HEADS * HEAD_DIM
WINDOW = 128
D_FF = 4 * D_MODEL
EPS = 1e-6
MASKED = -1e30

LANES = 128
GROUPS_PER_BLOCK = LANES // GROUP_SIZE
N_LANE_BLOCKS = D_MODEL // LANES
STATES_PER_BLOCK = GROUPS_PER_BLOCK * STATE_N
SUB_BATCH = 16
SCAN_LANES = 256
SAMPLE_GROUP = 8
MLP_PIECE = 256
VMEM_LIMIT = 56 * 1024 * 1024
CAST_BLOCK_BYTES = 4 * 1024 * 1024


def _resident(a):
    return pl.BlockSpec(a.shape, lambda *_: (0,) * a.ndim, pipeline_mode=pl.Buffered(1))


def _resident_layer(a, layer):
    return pl.BlockSpec((None,) + a.shape[1:], lambda *_: (layer,) + (0,) * (a.ndim - 1),
                        pipeline_mode=pl.Buffered(1))


def _rms(x, g):
    return x * lax.rsqrt(jnp.mean(x * x, axis=-1, keepdims=True) + EPS) * g


def _head_mean_sq(x, avg):
    return jnp.dot((x * x).astype(BF16), avg, preferred_element_type=F32)


def _cast_kernel(x_ref, o_ref):
    o_ref[...] = x_ref[...].astype(o_ref.dtype)


def _to_bf16(w):
    n, r, c = w.shape
    br = min(r, CAST_BLOCK_BYTES // (4 * c))
    spec = pl.BlockSpec((1, br, c), lambda i, j: (i, j, 0))
    return pl.pallas_call(
        _cast_kernel, grid=(n, r // br), in_specs=[spec], out_specs=spec,
        out_shape=jax.ShapeDtypeStruct(w.shape, BF16),
        compiler_params=pltpu.CompilerParams(dimension_semantics=("parallel", "parallel")),
        name="to_bf16",
    )(w)


def _discretize_kernel(a_re_ref, a_im_ref, log_dt_ref, b_re_ref, b_im_ref,
                       ab_re_ref, ab_im_ref, bb_re_ref, bb_im_ref):
    dt = jnp.exp(log_dt_ref[...])
    a_re = a_re_ref[...]
    a_im = a_im_ref[...]
    mag = jnp.exp(dt * a_re)
    ab_re = mag * jnp.cos(dt * a_im)
    ab_im = mag * jnp.sin(dt * a_im)
    den = a_re * a_re + a_im * a_im
    nr = ab_re - 1.0
    ni = ab_im
    f_re = (nr * a_re + ni * a_im) / den
    f_im = (ni * a_re - nr * a_im) / den
    ab_re_ref[...] = ab_re
    ab_im_ref[...] = ab_im
    b_re = b_re_ref[...]
    b_im = b_im_ref[...]
    fr = f_re[:, None, :]
    fi = f_im[:, None, :]
    bb_re_ref[...] = fr * b_re - fi * b_im
    bb_im_ref[...] = fr * b_im + fi * b_re


def _s5_discretize(a_re, a_im, log_dt, b_re, b_im):
    g, n = a_re.shape
    c = b_re.shape[-1]
    out = pl.pallas_call(
        _discretize_kernel,
        out_shape=(jax.ShapeDtypeStruct((g, n), F32), jax.ShapeDtypeStruct((g, n), F32),
                   jax.ShapeDtypeStruct((g, c, n), F32), jax.ShapeDtypeStruct((g, c, n), F32)),
        name="s5_discretize",
    )(a_re, a_im, log_dt.reshape(g, 1), jnp.swapaxes(b_re, 1, 2), jnp.swapaxes(b_im, 1, 2))
    return out


def _block_diag_weights(bb_re, bb_im, c_re, c_im):
    nb, gb = N_LANE_BLOCKS, GROUPS_PER_BLOCK
    eye = jnp.eye(gb, dtype=F32)
    bb = jnp.stack([bb_re, bb_im], axis=0).reshape(2, nb, gb, GROUP_SIZE, STATE_N)
    b_blk = jnp.einsum("rjgcn,gh->jgcrhn", bb, eye).reshape(nb, LANES, 2 * STATES_PER_BLOCK)
    cc = jnp.stack([c_re, -c_im], axis=0).reshape(2, nb, gb, GROUP_SIZE, STATE_N)
    c_blk = jnp.einsum("rjgcn,gh->jrgnhc", cc, eye).reshape(nb, 2 * STATES_PER_BLOCK, LANES)
    return b_blk.astype(BF16), c_blk.astype(BF16)


def _s5_layer_kernel(x_hbm, h0r_ref, h0i_ref, g_ref, d_ref, abr_ref, abi_ref, bblk_ref, cblk_ref,
                     wglu_ref, o_hbm, hro_ref, hio_ref, xbuf, obuf, in_sem, out_sem, hr_sc, hi_sc,
                     bur, bui, xsr, xsi, hn_sc, ge_sc, *, state_lanes_batch):
    _, tc, bb, d = xbuf.shape
    rows = tc * bb
    spb = STATES_PER_BLOCK
    step_idx = pl.program_id(0)
    n_chunks = pl.num_programs(0) - 1
    n_xbuf, n_obuf = xbuf.shape[0], obuf.shape[0]

    def in_copy(chunk, b):
        return pltpu.make_async_copy(x_hbm.at[b, pl.ds(chunk * tc, tc), :],
                                     xbuf.at[chunk % n_xbuf, :, b, :], in_sem.at[chunk % n_xbuf])

    def out_copy(chunk, b):
        return pltpu.make_async_copy(obuf.at[chunk % n_obuf, :, b, :],
                                     o_hbm.at[b, pl.ds(chunk * tc, tc), :], out_sem.at[chunk % n_obuf])

    @pl.when(step_idx == 0)
    def _():
        for b in range(bb):
            in_copy(0, b).start()
        ge_sc[1] = jnp.zeros(ge_sc.shape[1:], ge_sc.dtype)
        xbuf[n_xbuf - 1] = jnp.zeros(xbuf.shape[1:], xbuf.dtype)
        if state_lanes_batch:
            hr_sc[...] = h0r_ref[...].T
            hi_sc[...] = h0i_ref[...].T
        else:
            hr_sc[...] = h0r_ref[...]
            hi_sc[...] = h0i_ref[...]

    @pl.when(step_idx + 1 < n_chunks)
    def _():
        for b in range(bb):
            in_copy(step_idx + 1, b).start()

    @pl.when(step_idx >= n_obuf + 1)
    def _():
        for b in range(bb):
            out_copy(step_idx - n_obuf - 1, b).wait()

    prev = step_idx + (n_xbuf * n_obuf - 1)
    ge_prev_slot = prev % 2
    x_prev_slot = prev % n_xbuf
    o_slot = prev % n_obuf
    n_col = d // MLP_PIECE
    gate_first = [n for pair in zip(range(n_col, 2 * n_col), range(n_col)) for n in pair]
    z = {}

    def glu_piece(m):
        n = gate_first[m]
        z[n] = jnp.dot(ge_sc[ge_prev_slot], wglu_ref[:, n * MLP_PIECE:(n + 1) * MLP_PIECE],
                       preferred_element_type=F32)
        if n < n_col:
            cs = slice(n * MLP_PIECE, (n + 1) * MLP_PIECE)
            mix = z.pop(n) * jax.nn.sigmoid(z.pop(n + n_col))
            x_prev = xbuf[x_prev_slot, :, :, cs].reshape(rows, MLP_PIECE)
            obuf[o_slot, :, :, cs] = (x_prev + mix).reshape(tc, bb, MLP_PIECE)

    glu_pieces = [functools.partial(glu_piece, m) for m in range(2 * n_col)]

    def chunk_body():
        x_slot = step_idx % n_xbuf
        ge_slot = step_idx % 2
        for b in range(bb):
            in_copy(step_idx, b).wait()

        hn = _rms(xbuf[x_slot].reshape(rows, d), g_ref[...])
        hn_sc[...] = hn
        hnb = hn.astype(BF16)

        def input_map(j):
            bu = jnp.dot(hnb[:, j * LANES:(j + 1) * LANES], bblk_ref[j], preferred_element_type=F32)
            bur[j] = bu[:, :spb]
            bui[j] = bu[:, spb:]

        def recurrence(j):
            for half in range(spb // SCAN_LANES):
                cs = slice(half * SCAN_LANES, (half + 1) * SCAN_LANES)
                gs = slice(j * spb + half * SCAN_LANES, j * spb + (half + 1) * SCAN_LANES)
                ar = jnp.broadcast_to(abr_ref[:, gs], (SUB_BATCH, SCAN_LANES))
                ai = jnp.broadcast_to(abi_ref[:, gs], (SUB_BATCH, SCAN_LANES))
                for sb in range(bb // SUB_BATCH):
                    seqs = slice(sb * SUB_BATCH, (sb + 1) * SUB_BATCH)
                    h_re = hr_sc[seqs, gs]
                    h_im = hi_sc[seqs, gs]
                    for t in range(tc):
                        rs = slice(t * bb + sb * SUB_BATCH, t * bb + (sb + 1) * SUB_BATCH)
                        n_re = ar * h_re - ai * h_im + bur[j, rs, cs]
                        n_im = ar * h_im + ai * h_re + bui[j, rs, cs]
                        xsr[j, rs, cs] = n_re.astype(BF16)
                        xsi[j, rs, cs] = n_im.astype(BF16)
                        h_re, h_im = n_re, n_im
                    hr_sc[seqs, gs] = h_re
                    hi_sc[seqs, gs] = h_im

        def output_map(j):
            ls = slice(j * LANES, (j + 1) * LANES)
            y = (jnp.dot(xsr[j], cblk_ref[j, :spb, :], preferred_element_type=F32)
                 + jnp.dot(xsi[j], cblk_ref[j, spb:, :], preferred_element_type=F32))
            y = y + d_ref[:, ls] * hn_sc[:, ls]
            ge_sc[ge_slot, :, ls] = jax.nn.gelu(y).astype(BF16)

        pending_glu = list(glu_pieces)
        input_map(0)
        for j in range(N_LANE_BLOCKS):
            if j + 1 < N_LANE_BLOCKS:
                input_map(j + 1)
            recurrence(j)
            if j > 0:
                output_map(j - 1)
            if pending_glu:
                pending_glu.pop(0)()
        output_map(N_LANE_BLOCKS - 1)
        for piece in pending_glu:
            piece()

    @pl.when(step_idx < n_chunks)
    def _():
        chunk_body()

    @pl.when(step_idx == n_chunks)
    def _():
        for piece in glu_pieces:
            piece()

    @pl.when(step_idx >= 1)
    def _():
        for b in range(bb):
            out_copy(step_idx - 1, b).start()

    @pl.when(step_idx == n_chunks)
    def _():
        for back in range(min(n_obuf, 2), 0, -1):
            @pl.when(n_chunks >= back)
            def _(back=back):
                for b in range(bb):
                    out_copy(n_chunks - back, b).wait()
        if state_lanes_batch:
            hro_ref[...] = hr_sc[...].T
            hio_ref[...] = hi_sc[...].T
        else:
            hro_ref[...] = hr_sc[...]
            hio_ref[...] = hi_sc[...]


def _s5_layer(x, h0r, h0i, g_mix, d_skip, ab_re, ab_im, b_blk, c_blk, w_glu, *, tc, state_lanes_batch):
    bsz, length, d = x.shape
    rows = tc * bsz
    weights = (g_mix, d_skip, ab_re, ab_im, b_blk, c_blk, w_glu)
    state_shape = jax.ShapeDtypeStruct(h0r.shape, F32)
    block_shape = (N_LANE_BLOCKS, rows, STATES_PER_BLOCK)
    return pl.pallas_call(
        functools.partial(_s5_layer_kernel, state_lanes_batch=state_lanes_batch),
        grid=(length // tc + 1,),
        in_specs=[pl.BlockSpec(memory_space=pl.ANY), _resident(h0r), _resident(h0i)]
        + [_resident(w) for w in weights],
        out_specs=(pl.BlockSpec(memory_space=pl.ANY), pl.BlockSpec(h0r.shape, lambda i: (0, 0)),
                   pl.BlockSpec(h0i.shape, lambda i: (0, 0))),
        out_shape=(jax.ShapeDtypeStruct((bsz, length, d), F32), state_shape, state_shape),
        scratch_shapes=[pltpu.VMEM((3, tc, bsz, d), F32), pltpu.VMEM((2, tc, bsz, d), F32),
                        pltpu.SemaphoreType.DMA((3,)), pltpu.SemaphoreType.DMA((2,)),
                        pltpu.VMEM((bsz, D_STATE), F32), pltpu.VMEM((bsz, D_STATE), F32),
                        pltpu.VMEM(block_shape, F32), pltpu.VMEM(block_shape, F32),
                        pltpu.VMEM(block_shape, BF16), pltpu.VMEM(block_shape, BF16),
                        pltpu.VMEM((rows, d), F32), pltpu.VMEM((2, rows, d), BF16)],
        compiler_params=pltpu.CompilerParams(
            dimension_semantics=("arbitrary",), vmem_limit_bytes=VMEM_LIMIT),
        name="s5_layer",
    )(x, h0r, h0i, *weights)


def _mlp_schedule(n_chunks, n_col):
    schedule = [("up", 0, n) for n in range(n_col)]
    for c in range(n_chunks):
        if c + 1 < n_chunks:
            schedule += [("up", c + 1, n) for n in range(n_col)]
        schedule += [("down", c, n) for n in range(n_col)]
    return schedule


def _mlp_piece(state, kind, c, n, w_up_ref, w_down_ref):
    if kind == "up":
        hs = slice(c * D_MODEL + n * MLP_PIECE, c * D_MODEL + (n + 1) * MLP_PIECE)
        u = jnp.dot(state["hn"], w_up_ref[:, hs], preferred_element_type=F32)
        state.setdefault(("u", c), []).append(jnp.square(jnp.maximum(u, 0.0)).astype(BF16))
    else:
        if n == 0:
            state[("uc", c)] = jnp.concatenate(state.pop(("u", c)), axis=1)
        cs = slice(n * MLP_PIECE, (n + 1) * MLP_PIECE)
        state["acc"][n] = state["acc"][n] + jnp.dot(
            state[("uc", c)], w_down_ref[c * D_MODEL:(c + 1) * D_MODEL, cs], preferred_element_type=F32)


def _mlp(x, g, w_up_ref, w_down_ref):
    n_col = D_MODEL // MLP_PIECE
    state = {"hn": _rms(x, g).astype(BF16),
             "acc": [x[:, n * MLP_PIECE:(n + 1) * MLP_PIECE] for n in range(n_col)]}
    for piece in _mlp_schedule(w_up_ref.shape[1] // D_MODEL, n_col):
        _mlp_piece(state, *piece, w_up_ref, w_down_ref)
    return jnp.concatenate(state["acc"], axis=1)


def _mlp_proj_kernel(x_ref, gffn_ref, wup_ref, wdown_ref, gkv_ref, wkv_ref, gk_ref, gq1_ref, wq_ref,
                     gq_ref, avg_ref, x2_ref, q_ref, kd_ref, vd_ref, k_ref, v_ref, x2_sc):
    tm = x_ref.shape[0]
    step = pl.program_id(0)
    slot = step % 2

    @pl.when(step == 0)
    def _():
        x2_sc[1] = jnp.zeros(x2_sc.shape[1:], x2_sc.dtype)

    n_col = D_MODEL // MLP_PIECE
    x = x_ref[...]
    mlp = {"hn": _rms(x, gffn_ref[...]).astype(BF16),
           "acc": [x[:, n * MLP_PIECE:(n + 1) * MLP_PIECE] for n in range(n_col)]}
    schedule = _mlp_schedule(wup_ref.shape[1] // D_MODEL, n_col)

    prev = x2_sc[1 - slot]
    avg = avg_ref[...]
    lane = lax.broadcasted_iota(jnp.int32, (tm, LANES), 1)
    proj = {}

    def kv_matmul():
        normed = prev * lax.rsqrt(jnp.mean(prev * prev, axis=-1, keepdims=True) + EPS)
        proj["hq"] = (normed * gq1_ref[...]).astype(BF16)
        proj["kv"] = jnp.dot((normed * gkv_ref[...]).astype(BF16), wkv_ref[...],
                             preferred_element_type=F32)

    def kv_finish():
        k = proj["kv"][:, :D_KV]
        k = k * lax.rsqrt(_head_mean_sq(k, avg) + EPS) * gk_ref[...]
        v = proj["kv"][:, D_KV:]
        k_ref[...] = k
        v_ref[...] = v
        for src, dst_ref in ((k, kd_ref), (v, vd_ref)):
            for blk in range(D_KV // LANES):
                a = src[:, blk * LANES:(blk + 1) * LANES]
                swapped = pltpu.roll(a, HEAD_DIM, 1)
                dst_ref[:, 2 * blk * LANES:(2 * blk + 1) * LANES] = (
                    jnp.where(lane < HEAD_DIM, a, swapped).astype(BF16))
                dst_ref[:, (2 * blk + 1) * LANES:(2 * blk + 2) * LANES] = (
                    jnp.where(lane < HEAD_DIM, swapped, a).astype(BF16))

    def q_matmul(c):
        proj[("q", c)] = jnp.dot(proj["hq"], wq_ref[:, c * D_KV:(c + 1) * D_KV],
                                 preferred_element_type=F32)

    def q_finish(c):
        q = proj.pop(("q", c))
        q = q * lax.rsqrt(_head_mean_sq(q, avg) + EPS) * gq_ref[...]
        q_ref[:, c * D_KV:(c + 1) * D_KV] = (q * (HEAD_DIM ** -0.5)).astype(BF16)

    n_q = D_MODEL // D_KV
    epilogue = [kv_matmul, kv_finish, functools.partial(q_matmul, 0)]
    for c in range(n_q):
        if c + 1 < n_q:
            epilogue.append(functools.partial(q_matmul, c + 1))
        epilogue.append(functools.partial(q_finish, c))

    every = len(schedule) // len(epilogue)
    for i, piece in enumerate(schedule):
        _mlp_piece(mlp, *piece, wup_ref, wdown_ref)
        if i % every == every - 1 and epilogue:
            epilogue.pop(0)()
    for stage in epilogue:
        stage()

    x2 = jnp.concatenate(mlp["acc"], axis=1)
    x2_ref[...] = x2
    x2_sc[slot] = x2


def _mlp_proj(x2d, tm, g_ffn, w_up, w_down, g_kv, w_kv, g_k, g_q1, w_q, g_q, avg):
    rows = x2d.shape[0]
    n_tiles = rows // tm
    cur = lambda width: pl.BlockSpec((tm, width), lambda s: (jnp.minimum(s, n_tiles - 1), 0))
    prev = lambda width: pl.BlockSpec((tm, width), lambda s: (jnp.maximum(s - 1, 0), 0))
    tail = (g_kv, w_kv, g_k, g_q1, w_q, g_q, avg)
    return pl.pallas_call(
        _mlp_proj_kernel,
        grid=(n_tiles + 1,),
        in_specs=[cur(D_MODEL), _resident(g_ffn), _resident_layer(w_up, 0), _resident_layer(w_down, 0)]
        + [_resident(w) for w in tail],
        out_specs=(cur(D_MODEL), prev(D_MODEL), prev(2 * D_KV), prev(2 * D_KV), prev(D_KV), prev(D_KV)),
        out_shape=(jax.ShapeDtypeStruct((rows, D_MODEL), F32),
                   jax.ShapeDtypeStruct((rows, D_MODEL), BF16),
                   jax.ShapeDtypeStruct((rows, 2 * D_KV), BF16),
                   jax.ShapeDtypeStruct((rows, 2 * D_KV), BF16),
                   jax.ShapeDtypeStruct((rows, D_KV), F32),
                   jax.ShapeDtypeStruct((rows, D_KV), F32)),
        scratch_shapes=[pltpu.VMEM((2, tm, D_MODEL), F32)],
        compiler_params=pltpu.CompilerParams(
            dimension_semantics=("arbitrary",), vmem_limit_bytes=VMEM_LIMIT),
        name="mlp_proj",
    )(x2d, g_ffn, w_up, w_down, *tail)


def _prompt_attn_kernel(q_ref, kd_ref, vd_ref, kdp_ref, vdp_ref, x_ref, bias_ref, sink_ref, lmask_ref,
                        wo_ref, gffn_ref, wup_ref, wdown_ref, y_ref, o_sc, *, tiles_per_seq, n_tiles):
    tq = q_ref.shape[0]
    step = pl.program_id(0)
    slot = step % 2

    @pl.when(step == 0)
    def _():
        o_sc[1] = jnp.zeros(o_sc.shape[1:], o_sc.dtype)

    tile = jnp.minimum(step, n_tiles - 1)
    first = (tile % tiles_per_seq == 0).astype(jnp.int32)
    m_lo = lmask_ref[0:1, :]
    m_hi = lmask_ref[1:2, :]
    ones_lo = jnp.broadcast_to(m_lo, (2 * WINDOW, LANES))
    ones_hi = jnp.broadcast_to(m_hi, (2 * WINDOW, LANES))
    nt = (((1,), (1,)), ((), ()))
    lane = lax.broadcasted_iota(jnp.int32, (WINDOW, LANES), 1)
    pairs_per_kv = Q_PER_KV // 2
    windows = {}

    def window(r, kh):
        if (r, kh) not in windows:
            ls = slice(kh * LANES, (kh + 1) * LANES)
            if r == 0:
                kwin = jnp.concatenate([kdp_ref[:, ls], kd_ref[0:WINDOW, ls]], axis=0)
                vwin = jnp.concatenate([vdp_ref[:, ls], vd_ref[0:WINDOW, ls]], axis=0)
            else:
                kwin = kd_ref[(r - 1) * WINDOW:(r + 1) * WINDOW, ls]
                vwin = vd_ref[(r - 1) * WINDOW:(r + 1) * WINDOW, ls]
            windows[(r, kh)] = (kwin * m_lo, kwin * m_hi,
                                jnp.concatenate([vwin * m_lo, ones_lo], axis=1),
                                jnp.concatenate([vwin * m_hi, ones_hi], axis=1))
        return windows[(r, kh)]

    def scores(p):
        r, blk = divmod(p, N_LANE_BLOCKS)
        k_lo, k_hi, _, _ = window(r, blk // pairs_per_kv)
        sel = first if r == 0 else 0
        qp = q_ref[r * WINDOW:(r + 1) * WINDOW, blk * LANES:(blk + 1) * LANES]
        s0 = lax.dot_general(qp, k_lo, nt, preferred_element_type=F32) + bias_ref[sel, 2 * blk]
        s1 = lax.dot_general(qp, k_hi, nt, preferred_element_type=F32) + bias_ref[sel, 2 * blk + 1]
        return s0, s1

    def finish(p, s0, s1):
        r, blk = divmod(p, N_LANE_BLOCKS)
        _, _, v_lo, v_hi = window(r, blk // pairs_per_kv)
        h0 = 2 * blk
        m0 = jnp.maximum(jnp.max(s0, axis=-1, keepdims=True), sink_ref[:, h0:h0 + 1])
        m1 = jnp.maximum(jnp.max(s1, axis=-1, keepdims=True), sink_ref[:, h0 + 1:h0 + 2])
        p0 = jnp.exp(s0 - m0).astype(BF16)
        p1 = jnp.exp(s1 - m1).astype(BF16)
        ext = (jnp.dot(p0, v_lo, preferred_element_type=F32)
               + jnp.dot(p1, v_hi, preferred_element_type=F32))
        e0 = jnp.exp(sink_ref[:, h0:h0 + 1] - m0)
        e1 = jnp.exp(sink_ref[:, h0 + 1:h0 + 2] - m1)
        den = ext[:, LANES:] + jnp.where(lane < HEAD_DIM, e0, e1)
        o_sc[slot, r * WINDOW:(r + 1) * WINDOW, blk * LANES:(blk + 1) * LANES] = (
            (ext[:, :LANES] / den).astype(BF16))

    n_col = D_MODEL // MLP_PIECE
    n_chunks = wup_ref.shape[1] // D_MODEL
    o_prev = o_sc[1 - slot]
    state = {}

    def wo_piece(n):
        cs = slice(n * MLP_PIECE, (n + 1) * MLP_PIECE)
        state.setdefault("acc", []).append(
            x_ref[:, cs] + jnp.dot(o_prev, wo_ref[:, cs], preferred_element_type=F32))
        if n == n_col - 1:
            state["hn"] = _rms(jnp.concatenate(state["acc"], axis=1), gffn_ref[...]).astype(BF16)

    schedule = ([functools.partial(wo_piece, n) for n in range(n_col)]
                + [functools.partial(_mlp_piece, state, *p, wup_ref, wdown_ref)
                   for p in _mlp_schedule(n_chunks, n_col)])

    n_pairs = (tq // WINDOW) * N_LANE_BLOCKS
    pending = scores(0)
    for k in range(max(len(schedule), n_pairs)):
        nxt = scores(k + 1) if k + 1 < n_pairs else None
        if k < len(schedule):
            schedule[k]()
        if k < n_pairs:
            finish(k, *pending)
        pending = nxt
    y_ref[...] = jnp.concatenate(state["acc"], axis=1)


def _prompt_attention(q, kd, vd, x2, bias, sinks, lmask, w_o, g_ffn, w_up, w_down, *, bsz, length, tq):
    nt = length // tq
    n_tiles = bsz * nt
    per_win = tq // WINDOW
    att_tile = lambda s: jnp.minimum(s, n_tiles - 1)
    mlp_tile = lambda s: jnp.maximum(s - 1, 0)
    att = lambda width: pl.BlockSpec((tq, width), lambda s: (att_tile(s), 0))
    mlp = pl.BlockSpec((tq, D_MODEL), lambda s: (mlp_tile(s), 0))
    prev = pl.BlockSpec((WINDOW, 2 * D_KV), lambda s: (jnp.maximum(att_tile(s) * per_win - 1, 0), 0))
    return pl.pallas_call(
        functools.partial(_prompt_attn_kernel, tiles_per_seq=nt, n_tiles=n_tiles),
        grid=(n_tiles + 1,),
        in_specs=[att(D_MODEL), att(2 * D_KV), att(2 * D_KV), prev, prev, mlp,
                  _resident(bias), _resident(sinks), _resident(lmask), _resident(w_o), _resident(g_ffn),
                  _resident_layer(w_up, 1), _resident_layer(w_down, 1)],
        out_specs=mlp,
        out_shape=jax.ShapeDtypeStruct((bsz * length, D_MODEL), F32),
        scratch_shapes=[pltpu.VMEM((2, tq, D_MODEL), BF16)],
        compiler_params=pltpu.CompilerParams(
            dimension_semantics=("arbitrary",), vmem_limit_bytes=VMEM_LIMIT),
        name="prompt_attention",
    )(q, kd, vd, kd, vd, x2, bias, sinks, lmask, w_o, g_ffn, w_up, w_down)


def _alibi_slopes():
    return (2.0 ** (-8.0 * np.arange(1, N_HEADS + 1, dtype=np.float32) / N_HEADS)).astype(np.float32)


def _prompt_bias():
    qi = np.arange(WINDOW)[:, None]
    kj = np.arange(2 * WINDOW)[None, :]
    dist = qi - kj + WINDOW
    valid = (dist >= 0) & (dist < WINDOW)
    valid = np.stack([valid, valid & (kj >= WINDOW)], axis=0)[:, None]
    pen = -_alibi_slopes()[:, None, None] * dist.astype(np.float32)[None]
    return np.where(valid, pen[None], np.float32(MASKED)).astype(np.float32)


def _sample_attn_kernel(q_ref, kc_ref, vc_ref, kn_ref, vn_ref, bias_c_ref, bias_n_ref, sink_ref,
                        o_ref, kbuf_ref, vbuf_ref):
    g = kc_ref.shape[0]
    t_new = kn_ref.shape[0] // g
    nt = (((1,), (1,)), ((), ()))
    pad = jnp.zeros((LANES - g * t_new, D_KV), F32)
    kn_t = jnp.concatenate([kn_ref[...], pad], axis=0).T
    vn_t = jnp.concatenate([vn_ref[...], pad], axis=0).T
    lane = lax.broadcasted_iota(jnp.int32, (HEAD_DIM, WINDOW), 1)
    is_new = lane >= WINDOW - t_new

    for kh in range(N_KV_HEADS):
        hs = slice(kh * HEAD_DIM, (kh + 1) * HEAD_DIM)
        q = q_ref[kh]
        k_t = jnp.concatenate([kc_ref[s, kh] for s in range(g)], axis=1)
        v_t = jnp.concatenate([vc_ref[s, kh] for s in range(g)], axis=1)
        s_c = jnp.dot(q, k_t.astype(BF16), preferred_element_type=F32) + bias_c_ref[kh]
        s_n = jnp.dot(q, kn_t[hs].astype(BF16), preferred_element_type=F32) + bias_n_ref[kh]
        sink = sink_ref[kh]
        m = jnp.maximum(jnp.maximum(jnp.max(s_c, axis=-1, keepdims=True),
                                    jnp.max(s_n, axis=-1, keepdims=True)), sink)
        p_c = jnp.exp(s_c - m)
        p_n = jnp.exp(s_n - m)
        den = (jnp.sum(p_c, axis=-1, keepdims=True) + jnp.sum(p_n, axis=-1, keepdims=True)
               + jnp.exp(sink - m))
        o = (lax.dot_general(p_c.astype(BF16), v_t.astype(BF16), nt, preferred_element_type=F32)
             + lax.dot_general(p_n.astype(BF16), vn_t[hs].astype(BF16), nt, preferred_element_type=F32))
        o_ref[kh] = o / den

        for s in range(g):
            shift_new = (WINDOW - t_new - s * t_new) % WINDOW
            k_new = pltpu.roll(kn_t[hs], shift_new, 1)
            v_new = pltpu.roll(vn_t[hs], shift_new, 1)
            kbuf_ref[s, kh] = jnp.where(is_new, k_new, pltpu.roll(kc_ref[s, kh], WINDOW - t_new, 1))
            vbuf_ref[s, kh] = jnp.where(is_new, v_new, pltpu.roll(vc_ref[s, kh], WINDOW - t_new, 1))


def _sample_attention(q_heads, cache_kt, cache_vt, k_new, v_new, bias_c, bias_n, sink_rows, *, t_new):
    g = SAMPLE_GROUP
    bsz = cache_kt.shape[0]
    rows_q = g * Q_PER_KV * t_new
    cache_spec = pl.BlockSpec((g, N_KV_HEADS, HEAD_DIM, WINDOW), lambda i: (i, 0, 0, 0))
    q_spec = pl.BlockSpec((N_KV_HEADS, rows_q, HEAD_DIM), lambda i: (0, i, 0))
    new_spec = pl.BlockSpec((g * t_new, D_KV), lambda i: (i, 0))
    return pl.pallas_call(
        _sample_attn_kernel,
        grid=(bsz // g,),
        in_specs=[q_spec, cache_spec, cache_spec, new_spec, new_spec,
                  _resident(bias_c), _resident(bias_n), _resident(sink_rows)],
        out_specs=(q_spec, cache_spec, cache_spec),
        out_shape=(jax.ShapeDtypeStruct(q_heads.shape, F32),
                   jax.ShapeDtypeStruct(cache_kt.shape, cache_kt.dtype),
                   jax.ShapeDtypeStruct(cache_vt.shape, cache_vt.dtype)),
        compiler_params=pltpu.CompilerParams(
            dimension_semantics=("parallel",), vmem_limit_bytes=VMEM_LIMIT),
        name="sample_attention",
    )(q_heads, cache_kt, cache_vt, k_new, v_new, bias_c, bias_n, sink_rows)


def _sample_bias(t_new):
    g = SAMPLE_GROUP
    slope = _alibi_slopes().reshape(N_KV_HEADS, 1, Q_PER_KV, 1, 1, 1)
    seq_q = np.arange(g).reshape(1, g, 1, 1, 1, 1)
    t = np.arange(t_new).reshape(1, 1, 1, t_new, 1, 1)
    seq_k = np.arange(g).reshape(1, 1, 1, 1, g, 1)
    j = np.arange(WINDOW).reshape(1, 1, 1, 1, 1, WINDOW)
    dist = t + WINDOW - j
    ok = (seq_q == seq_k) & (dist >= 0) & (dist < WINDOW)
    rows = g * Q_PER_KV * t_new
    bias_c = np.where(ok, -slope * dist.astype(np.float32), np.float32(MASKED))
    bias_c = bias_c.reshape(N_KV_HEADS, rows, g * WINDOW).astype(np.float32)
    tn = np.arange(t_new).reshape(1, 1, 1, 1, 1, t_new)
    dist_n = t - tn
    ok_n = (seq_q == seq_k) & (dist_n >= 0) & (dist_n < WINDOW)
    bias_n = np.where(ok_n, -slope * dist_n.astype(np.float32), np.float32(MASKED))
    bias_n = bias_n.reshape(N_KV_HEADS, rows, g * t_new)
    bias_n = np.pad(bias_n, ((0, 0), (0, 0), (0, LANES - g * t_new)), constant_values=MASKED)
    return bias_c, bias_n.astype(np.float32)


def _out_mlp_kernel(o_ref, x_ref, wo_ref, gffn_ref, wup_ref, wdown_ref, y_ref):
    x3 = x_ref[...] + jnp.dot(o_ref[...], wo_ref[...], preferred_element_type=F32)
    y_ref[...] = _mlp(x3, gffn_ref[...], wup_ref, wdown_ref)


def _out_mlp(o, x2, w_o, g_ffn, w_up, w_down, tm):
    rows = x2.shape[0]
    row = pl.BlockSpec((tm, D_MODEL), lambda i: (i, 0))
    return pl.pallas_call(
        _out_mlp_kernel,
        grid=(rows // tm,),
        in_specs=[row, row, _resident(w_o), _resident(g_ffn), _resident_layer(w_up, 1),
                  _resident_layer(w_down, 1)],
        out_specs=row,
        out_shape=jax.ShapeDtypeStruct((rows, D_MODEL), F32),
        compiler_params=pltpu.CompilerParams(
            dimension_semantics=("parallel",), vmem_limit_bytes=VMEM_LIMIT),
        name="out_mlp",
    )(o, x2, w_o, g_ffn, w_up, w_down)


def kernel(x_prompt, x_sample, state_ssm_re, state_ssm_im, cache_k, cache_v, norm_mix, norm_ffn,
           ssm_a_re, ssm_a_im, ssm_log_dt, ssm_b_re, ssm_b_im, ssm_c_re, ssm_c_im, ssm_d, ssm_w_glu,
           kv_norm, w_kv, k_norm, w_q, q_norm, attn_sinks, w_o, w_up, w_down):
    bsz, length, d = x_prompt.shape
    dec_b, dec_t, _ = x_sample.shape

    ab_re, ab_im, bb_re, bb_im = _s5_discretize(ssm_a_re[0], ssm_a_im[0], ssm_log_dt[0],
                                                ssm_b_re[0], ssm_b_im[0])
    b_blk, c_blk = _block_diag_weights(bb_re, bb_im, ssm_c_re[0], ssm_c_im[0])
    ab_re = ab_re.reshape(1, D_STATE)
    ab_im = ab_im.reshape(1, D_STATE)
    row = lambda v: v.reshape(1, -1).astype(F32)
    w_glu = ssm_w_glu[0].astype(BF16)
    w_up_b = _to_bf16(w_up)
    w_down_b = _to_bf16(w_down)
    w_kv_b = w_kv.astype(BF16)
    w_q0 = w_q[0].astype(BF16)
    w_o0 = w_o[0].astype(BF16)
    g_k = row(jnp.tile(k_norm, D_KV // HEAD_DIM))
    g_q = row(jnp.tile(q_norm[0], D_KV // HEAD_DIM))
    avg = jnp.asarray(np.kron(np.eye(D_KV // HEAD_DIM, dtype=np.float32),
                              np.full((HEAD_DIM, HEAD_DIM), 1.0 / HEAD_DIM, np.float32)), BF16)
    lane = np.arange(LANES)
    lmask = jnp.asarray(np.stack([lane < HEAD_DIM, lane >= HEAD_DIM]), BF16)
    sinks = row(attn_sinks[0])

    def layer0(x, h0r, h0i, tc, state_lanes_batch):
        return _s5_layer(x, h0r, h0i, row(norm_mix[0]), row(ssm_d[0]), ab_re, ab_im, b_blk, c_blk,
                         w_glu, tc=tc, state_lanes_batch=state_lanes_batch)

    def mlp_proj(x2d, tm):
        return _mlp_proj(x2d, tm, row(norm_ffn[0]), w_up_b, w_down_b, row(kv_norm), w_kv_b, g_k,
                         row(norm_mix[1]), w_q0, g_q, avg)

    zeros = jnp.zeros((bsz, D_STATE), F32)
    x1, hr_p, hi_p = layer0(x_prompt, zeros, zeros, 32, False)
    x2, q, kd, vd, k, v = mlp_proj(x1.reshape(bsz * length, d), 512)
    y_prompt = _prompt_attention(q, kd, vd, x2, jnp.asarray(_prompt_bias()), sinks, lmask, w_o0,
                                 row(norm_ffn[1]), w_up_b, w_down_b, bsz=bsz, length=length, tq=512)
    y_prompt = y_prompt.reshape(bsz, length, d)
    k_p = k.reshape(bsz, length, D_KV)[:, -WINDOW:].reshape(bsz, WINDOW, N_KV_HEADS, HEAD_DIM)
    v_p = v.reshape(bsz, length, D_KV)[:, -WINDOW:].reshape(bsz, WINDOW, N_KV_HEADS, HEAD_DIM)
    state_shape = (1, bsz, N_GROUPS, STATE_N)
    re_p = hr_p.reshape(state_shape).astype(state_ssm_re.dtype)
    im_p = hi_p.reshape(state_shape).astype(state_ssm_im.dtype)

    h0r_t = jnp.transpose(state_ssm_re[0], (1, 2, 0)).reshape(D_STATE, dec_b).astype(F32)
    h0i_t = jnp.transpose(state_ssm_im[0], (1, 2, 0)).reshape(D_STATE, dec_b).astype(F32)
    x1s, hr_s, hi_s = layer0(x_sample, h0r_t, h0i_t, dec_t, True)
    x2s, qs, _, _, ks, vs = mlp_proj(x1s.reshape(dec_b * dec_t, d), dec_b * dec_t)
    q_heads = qs.reshape(dec_b, dec_t, N_KV_HEADS, Q_PER_KV, HEAD_DIM).transpose(2, 0, 3, 1, 4)
    q_heads = q_heads.reshape(N_KV_HEADS, dec_b * Q_PER_KV * dec_t, HEAD_DIM)
    bias_c, bias_n = _sample_bias(dec_t)
    sink_rows = jnp.broadcast_to(
        attn_sinks[0].astype(F32).reshape(N_KV_HEADS, 1, Q_PER_KV, 1),
        (N_KV_HEADS, SAMPLE_GROUP, Q_PER_KV, dec_t)).reshape(N_KV_HEADS, -1, 1)
    o_heads, k_s_t, v_s_t = _sample_attention(
        q_heads, jnp.transpose(cache_k, (0, 2, 3, 1)), jnp.transpose(cache_v, (0, 2, 3, 1)),
        ks.astype(cache_k.dtype), vs.astype(cache_v.dtype), jnp.asarray(bias_c), jnp.asarray(bias_n),
        sink_rows, t_new=dec_t)
    o_s = o_heads.reshape(N_KV_HEADS, dec_b, Q_PER_KV, dec_t, HEAD_DIM).transpose(1, 3, 0, 2, 4)
    o_s = o_s.reshape(dec_b * dec_t, d).astype(BF16)
    y_sample = _out_mlp(o_s, x2s, w_o0, row(norm_ffn[1]), w_up_b, w_down_b, dec_b * dec_t)
    y_sample = y_sample.reshape(dec_b, dec_t, d)
    k_s = jnp.transpose(k_s_t, (0, 3, 1, 2))
    v_s = jnp.transpose(v_s_t, (0, 3, 1, 2))
    re_s = jnp.transpose(hr_s.reshape(N_GROUPS, STATE_N, dec_b), (2, 0, 1))[None].astype(state_ssm_re.dtype)
    im_s = jnp.transpose(hi_s.reshape(N_GROUPS, STATE_N, dec_b), (2, 0, 1))[None].astype(state_ssm_im.dtype)

    return (y_prompt, y_sample, re_p, im_p, k_p, v_p, re_s, im_s, k_s, v_s)
```

```python
import functools

import jax
import jax.numpy as jnp
import numpy as np
from jax import lax
from jax.experimental import pallas as pl
from jax.experimental.pallas import tpu as pltpu

F32 = jnp.float32
BF16 = jnp.bfloat16

D_MODEL = 1024
GROUP_SIZE = 16
N_GROUPS = D_MODEL // GROUP_SIZE
STATE_N = 64
D_STATE = N_GROUPS * STATE_N
HEAD_DIM = 64
N_HEADS = D_MODEL // HEAD_DIM
N_KV_HEADS = 4
Q_PER_KV = N_HEADS // N_KV_HEADS
D_KV = N_KV_HEADS * HEAD_DIM
WINDOW = 128
D_FF = 4 * D_MODEL
EPS = 1e-6
MASKED = -1e30

LANES = 128
GROUPS_PER_BLOCK = LANES // GROUP_SIZE
N_LANE_BLOCKS = D_MODEL // LANES
STATES_PER_BLOCK = GROUPS_PER_BLOCK * STATE_N
SUB_BATCH = 16
SCAN_LANES = 256
SAMPLE_GROUP = 8
MLP_PIECE = 256
VMEM_LIMIT = 56 * 1024 * 1024


def _resident(a):
    return pl.BlockSpec(a.shape, lambda *_: (0,) * a.ndim, pipeline_mode=pl.Buffered(1))


def _resident_layer(a, layer):
    return pl.BlockSpec((None,) + a.shape[1:], lambda *_: (layer,) + (0,) * (a.ndim - 1),
                        pipeline_mode=pl.Buffered(1))


def _rms(x, g):
    return x * lax.rsqrt(jnp.mean(x * x, axis=-1, keepdims=True) + EPS) * g


def _head_mean_sq(x, avg):
    return jnp.dot((x * x).astype(BF16), avg, preferred_element_type=F32)


def _discretize_kernel(a_re_ref, a_im_ref, log_dt_ref, b_re_ref, b_im_ref,
                       ab_re_ref, ab_im_ref, bb_re_ref, bb_im_ref):
    dt = jnp.exp(log_dt_ref[...])
    a_re = a_re_ref[...]
    a_im = a_im_ref[...]
    mag = jnp.exp(dt * a_re)
    ab_re = mag * jnp.cos(dt * a_im)
    ab_im = mag * jnp.sin(dt * a_im)
    den = a_re * a_re + a_im * a_im
    nr = ab_re - 1.0
    ni = ab_im
    f_re = (nr * a_re + ni * a_im) / den
    f_im = (ni * a_re - nr * a_im) / den
    ab_re_ref[...] = ab_re
    ab_im_ref[...] = ab_im
    b_re = b_re_ref[...]
    b_im = b_im_ref[...]
    fr = f_re[:, None, :]
    fi = f_im[:, None, :]
    bb_re_ref[...] = fr * b_re - fi * b_im
    bb_im_ref[...] = fr * b_im + fi * b_re


def _s5_discretize(a_re, a_im, log_dt, b_re, b_im):
    g, n = a_re.shape
    c = b_re.shape[-1]
    out = pl.pallas_call(
        _discretize_kernel,
        out_shape=(jax.ShapeDtypeStruct((g, n), F32), jax.ShapeDtypeStruct((g, n), F32),
                   jax.ShapeDtypeStruct((g, c, n), F32), jax.ShapeDtypeStruct((g, c, n), F32)),
        name="s5_discretize",
    )(a_re, a_im, log_dt.reshape(g, 1), jnp.swapaxes(b_re, 1, 2), jnp.swapaxes(b_im, 1, 2))
    return out


def _block_diag_weights(bb_re, bb_im, c_re, c_im):
    nb, gb = N_LANE_BLOCKS, GROUPS_PER_BLOCK
    eye = jnp.eye(gb, dtype=F32)
    bb = jnp.stack([bb_re, bb_im], axis=0).reshape(2, nb, gb, GROUP_SIZE, STATE_N)
    b_blk = jnp.einsum("rjgcn,gh->jgcrhn", bb, eye).reshape(nb, LANES, 2 * STATES_PER_BLOCK)
    cc = jnp.stack([c_re, -c_im], axis=0).reshape(2, nb, gb, GROUP_SIZE, STATE_N)
    c_blk = jnp.einsum("rjgcn,gh->jrgnhc", cc, eye).reshape(nb, 2 * STATES_PER_BLOCK, LANES)
    return b_blk.astype(BF16), c_blk.astype(BF16)


def _s5_layer_kernel(*refs, state_lanes_batch, n_cast):
    n_in = 10 + n_cast
    (x_hbm, h0r_ref, h0i_ref, g_ref, d_ref, abr_ref, abi_ref, bblk_ref, cblk_ref, wglu_ref) = refs[:10]
    o_hbm, hro_ref, hio_ref = refs[n_in:n_in + 3]
    (xbuf, obuf, in_sem, out_sem, hr_sc, hi_sc, bur, bui, xsr, xsi, hn_sc, ge_sc) = refs[n_in + 3 + n_cast:]
    for src, dst in zip(refs[10:n_in], refs[n_in + 3:n_in + 3 + n_cast]):
        dst[...] = src[...].astype(dst.dtype)

    _, tc, bb, d = xbuf.shape
    rows = tc * bb
    spb = STATES_PER_BLOCK
    step_idx = pl.program_id(0)
    n_chunks = pl.num_programs(0) - 1
    n_xbuf, n_obuf = xbuf.shape[0], obuf.shape[0]

    def in_copy(chunk, b):
        return pltpu.make_async_copy(x_hbm.at[b, pl.ds(chunk * tc, tc), :],
                                     xbuf.at[chunk % n_xbuf, :, b, :], in_sem.at[chunk % n_xbuf])

    def out_copy(chunk, b):
        return pltpu.make_async_copy(obuf.at[chunk % n_obuf, :, b, :],
                                     o_hbm.at[b, pl.ds(chunk * tc, tc), :], out_sem.at[chunk % n_obuf])

    @pl.when(step_idx == 0)
    def _():
        for b in range(bb):
            in_copy(0, b).start()
        ge_sc[1] = jnp.zeros(ge_sc.shape[1:], ge_sc.dtype)
        xbuf[n_xbuf - 1] = jnp.zeros(xbuf.shape[1:], xbuf.dtype)
        if state_lanes_batch:
            hr_sc[...] = h0r_ref[...].T
            hi_sc[...] = h0i_ref[...].T
        else:
            hr_sc[...] = h0r_ref[...]
            hi_sc[...] = h0i_ref[...]

    @pl.when(step_idx + 1 < n_chunks)
    def _():
        for b in range(bb):
            in_copy(step_idx + 1, b).start()

    @pl.when(step_idx >= n_obuf + 1)
    def _():
        for b in range(bb):
            out_copy(step_idx - n_obuf - 1, b).wait()

    prev = step_idx + (n_xbuf * n_obuf - 1)
    ge_prev_slot = prev % 2
    x_prev_slot = prev % n_xbuf
    o_slot = prev % n_obuf
    n_col = d // MLP_PIECE
    gate_first = [n for pair in zip(range(n_col, 2 * n_col), range(n_col)) for n in pair]
    z = {}

    def glu_piece(m):
        n = gate_first[m]
        z[n] = jnp.dot(ge_sc[ge_prev_slot], wglu_ref[:, n * MLP_PIECE:(n + 1) * MLP_PIECE],
                       preferred_element_type=F32)
        if n < n_col:
            cs = slice(n * MLP_PIECE, (n + 1) * MLP_PIECE)
            mix = z.pop(n) * jax.nn.sigmoid(z.pop(n + n_col))
            x_prev = xbuf[x_prev_slot, :, :, cs].reshape(rows, MLP_PIECE)
            obuf[o_slot, :, :, cs] = (x_prev + mix).reshape(tc, bb, MLP_PIECE)

    glu_pieces = [functools.partial(glu_piece, m) for m in range(2 * n_col)]

    def chunk_body():
        x_slot = step_idx % n_xbuf
        ge_slot = step_idx % 2
        for b in range(bb):
            in_copy(step_idx, b).wait()

        hn = _rms(xbuf[x_slot].reshape(rows, d), g_ref[...])
        hn_sc[...] = hn
        hnb = hn.astype(BF16)

        def input_map(j):
            bu = jnp.dot(hnb[:, j * LANES:(j + 1) * LANES], bblk_ref[j], preferred_element_type=F32)
            bur[j] = bu[:, :spb]
            bui[j] = bu[:, spb:]

        def recurrence(j):
            for half in range(spb // SCAN_LANES):
                cs = slice(half * SCAN_LANES, (half + 1) * SCAN_LANES)
                gs = slice(j * spb + half * SCAN_LANES, j * spb + (half + 1) * SCAN_LANES)
                ar = jnp.broadcast_to(abr_ref[:, gs], (SUB_BATCH, SCAN_LANES))
                ai = jnp.broadcast_to(abi_ref[:, gs], (SUB_BATCH, SCAN_LANES))
                for sb in range(bb // SUB_BATCH):
                    seqs = slice(sb * SUB_BATCH, (sb + 1) * SUB_BATCH)
                    h_re = hr_sc[seqs, gs]
                    h_im = hi_sc[seqs, gs]
                    for t in range(tc):
                        rs = slice(t * bb + sb * SUB_BATCH, t * bb + (sb + 1) * SUB_BATCH)
                        n_re = ar * h_re - ai * h_im + bur[j, rs, cs]
                        n_im = ar * h_im + ai * h_re + bui[j, rs, cs]
                        xsr[j, rs, cs] = n_re.astype(BF16)
                        xsi[j, rs, cs] = n_im.astype(BF16)
                        h_re, h_im = n_re, n_im
                    hr_sc[seqs, gs] = h_re
                    hi_sc[seqs, gs] = h_im

        def output_map(j):
            ls = slice(j * LANES, (j + 1) * LANES)
            y = (jnp.dot(xsr[j], cblk_ref[j, :spb, :], preferred_element_type=F32)
                 + jnp.dot(xsi[j], cblk_ref[j, spb:, :], preferred_element_type=F32))
            y = y + d_ref[:, ls] * hn_sc[:, ls]
            ge_sc[ge_slot, :, ls] = jax.nn.gelu(y).astype(BF16)

        pending_glu = list(glu_pieces)
        input_map(0)
        for j in range(N_LANE_BLOCKS):
            if j + 1 < N_LANE_BLOCKS:
                input_map(j + 1)
            recurrence(j)
            if j > 0:
                output_map(j - 1)
            if pending_glu:
                pending_glu.pop(0)()
        output_map(N_LANE_BLOCKS - 1)
        for piece in pending_glu:
            piece()

    @pl.when(step_idx < n_chunks)
    def _():
        chunk_body()

    @pl.when(step_idx == n_chunks)
    def _():
        for piece in glu_pieces:
            piece()

    @pl.when(step_idx >= 1)
    def _():
        for b in range(bb):
            out_copy(step_idx - 1, b).start()

    @pl.when(step_idx == n_chunks)
    def _():
        for back in range(min(n_obuf, 2), 0, -1):
            @pl.when(n_chunks >= back)
            def _(back=back):
                for b in range(bb):
                    out_copy(n_chunks - back, b).wait()
        if state_lanes_batch:
            hro_ref[...] = hr_sc[...].T
            hio_ref[...] = hi_sc[...].T
        else:
            hro_ref[...] = hr_sc[...]
            hio_ref[...] = hi_sc[...]


def _s5_layer(x, h0r, h0i, g_mix, d_skip, ab_re, ab_im, b_blk, c_blk, w_glu, *, tc, state_lanes_batch,
              cast=()):
    bsz, length, d = x.shape
    rows = tc * bsz
    n_chunks = length // tc
    weights = (g_mix, d_skip, ab_re, ab_im, b_blk, c_blk, w_glu)
    state_shape = jax.ShapeDtypeStruct(h0r.shape, F32)
    block_shape = (N_LANE_BLOCKS, rows, STATES_PER_BLOCK)
    cast_specs = [pl.BlockSpec((a.shape[0] // n_chunks, a.shape[1]),
                               lambda i: (jnp.minimum(i, n_chunks - 1), 0)) for a in cast]
    return pl.pallas_call(
        functools.partial(_s5_layer_kernel, state_lanes_batch=state_lanes_batch, n_cast=len(cast)),
        grid=(n_chunks + 1,),
        in_specs=[pl.BlockSpec(memory_space=pl.ANY), _resident(h0r), _resident(h0i)]
        + [_resident(w) for w in weights] + cast_specs,
        out_specs=[pl.BlockSpec(memory_space=pl.ANY), pl.BlockSpec(h0r.shape, lambda i: (0, 0)),
                   pl.BlockSpec(h0i.shape, lambda i: (0, 0))] + cast_specs,
        out_shape=[jax.ShapeDtypeStruct((bsz, length, d), F32), state_shape, state_shape]
        + [jax.ShapeDtypeStruct(a.shape, BF16) for a in cast],
        scratch_shapes=[pltpu.VMEM((3, tc, bsz, d), F32), pltpu.VMEM((2, tc, bsz, d), F32),
                        pltpu.SemaphoreType.DMA((3,)), pltpu.SemaphoreType.DMA((2,)),
                        pltpu.VMEM((bsz, D_STATE), F32), pltpu.VMEM((bsz, D_STATE), F32),
                        pltpu.VMEM(block_shape, F32), pltpu.VMEM(block_shape, F32),
                        pltpu.VMEM(block_shape, BF16), pltpu.VMEM(block_shape, BF16),
                        pltpu.VMEM((rows, d), F32), pltpu.VMEM((2, rows, d), BF16)],
        compiler_params=pltpu.CompilerParams(
            dimension_semantics=("arbitrary",), vmem_limit_bytes=VMEM_LIMIT),
        name="s5_layer",
    )(x, h0r, h0i, *weights, *cast)


def _mlp_schedule(n_chunks, n_col):
    schedule = [("up", 0, n) for n in range(n_col)]
    for c in range(n_chunks):
        if c + 1 < n_chunks:
            schedule += [("up", c + 1, n) for n in range(n_col)]
        schedule += [("down", c, n) for n in range(n_col)]
    return schedule


def _mlp_piece(state, kind, c, n, w_up_ref, w_down_ref):
    if kind == "up":
        hs = slice(c * D_MODEL + n * MLP_PIECE, c * D_MODEL + (n + 1) * MLP_PIECE)
        u = jnp.dot(state["hn"], w_up_ref[:, hs], preferred_element_type=F32)
        state.setdefault(("u", c), []).append(jnp.square(jnp.maximum(u, 0.0)).astype(BF16))
    else:
        if n == 0:
            state[("uc", c)] = jnp.concatenate(state.pop(("u", c)), axis=1)
        cs = slice(n * MLP_PIECE, (n + 1) * MLP_PIECE)
        state["acc"][n] = state["acc"][n] + jnp.dot(
            state[("uc", c)], w_down_ref[c * D_MODEL:(c + 1) * D_MODEL, cs], preferred_element_type=F32)


def _mlp(x, g, w_up_ref, w_down_ref):
    n_col = D_MODEL // MLP_PIECE
    state = {"hn": _rms(x, g).astype(BF16),
             "acc": [x[:, n * MLP_PIECE:(n + 1) * MLP_PIECE] for n in range(n_col)]}
    for piece in _mlp_schedule(w_up_ref.shape[1] // D_MODEL, n_col):
        _mlp_piece(state, *piece, w_up_ref, w_down_ref)
    return jnp.concatenate(state["acc"], axis=1)


def _mlp_proj_kernel(x_ref, gffn_ref, wup_ref, wdown_ref, gkv_ref, wkv_ref, gk_ref, gq1_ref, wq_ref,
                     gq_ref, avg_ref, x2_ref, q_ref, kd_ref, vd_ref, k_ref, v_ref, x2_sc):
    tm = x_ref.shape[0]
    step = pl.program_id(0)
    slot = step % 2

    @pl.when(step == 0)
    def _():
        x2_sc[1] = jnp.zeros(x2_sc.shape[1:], x2_sc.dtype)

    n_col = D_MODEL // MLP_PIECE
    x = x_ref[...]
    mlp = {"hn": _rms(x, gffn_ref[...]).astype(BF16),
           "acc": [x[:, n * MLP_PIECE:(n + 1) * MLP_PIECE] for n in range(n_col)]}
    schedule = _mlp_schedule(wup_ref.shape[1] // D_MODEL, n_col)

    prev = x2_sc[1 - slot]
    avg = avg_ref[...]
    lane = lax.broadcasted_iota(jnp.int32, (tm, LANES), 1)
    proj = {}

    def kv_matmul():
        normed = prev * lax.rsqrt(jnp.mean(prev * prev, axis=-1, keepdims=True) + EPS)
        proj["hq"] = (normed * gq1_ref[...]).astype(BF16)
        proj["kv"] = jnp.dot((normed * gkv_ref[...]).astype(BF16), wkv_ref[...],
                             preferred_element_type=F32)

    def kv_finish():
        k = proj["kv"][:, :D_KV]
        k = k * lax.rsqrt(_head_mean_sq(k, avg) + EPS) * gk_ref[...]
        v = proj["kv"][:, D_KV:]
        k_ref[...] = k
        v_ref[...] = v
        for src, dst_ref in ((k, kd_ref), (v, vd_ref)):
            for blk in range(D_KV // LANES):
                a = src[:, blk * LANES:(blk + 1) * LANES]
                swapped = pltpu.roll(a, HEAD_DIM, 1)
                dst_ref[:, 2 * blk * LANES:(2 * blk + 1) * LANES] = (
                    jnp.where(lane < HEAD_DIM, a, swapped).astype(BF16))
                dst_ref[:, (2 * blk + 1) * LANES:(2 * blk + 2) * LANES] = (
                    jnp.where(lane < HEAD_DIM, swapped, a).astype(BF16))

    def q_matmul(c):
        proj[("q", c)] = jnp.dot(proj["hq"], wq_ref[:, c * D_KV:(c + 1) * D_KV],
                                 preferred_element_type=F32)

    def q_finish(c):
        q = proj.pop(("q", c))
        q = q * lax.rsqrt(_head_mean_sq(q, avg) + EPS) * gq_ref[...]
        q_ref[:, c * D_KV:(c + 1) * D_KV] = (q * (HEAD_DIM ** -0.5)).astype(BF16)

    n_q = D_MODEL // D_KV
    epilogue = [kv_matmul, kv_finish, functools.partial(q_matmul, 0)]
    for c in range(n_q):
        if c + 1 < n_q:
            epilogue.append(functools.partial(q_matmul, c + 1))
        epilogue.append(functools.partial(q_finish, c))

    every = len(schedule) // len(epilogue)
    for i, piece in enumerate(schedule):
        _mlp_piece(mlp, *piece, wup_ref, wdown_ref)
        if i % every == every - 1 and epilogue:
            epilogue.pop(0)()
    for stage in epilogue:
        stage()

    x2 = jnp.concatenate(mlp["acc"], axis=1)
    x2_ref[...] = x2
    x2_sc[slot] = x2


def _mlp_proj(x2d, tm, g_ffn, w_up, w_down, g_kv, w_kv, g_k, g_q1, w_q, g_q, avg):
    rows = x2d.shape[0]
    n_tiles = rows // tm
    cur = lambda width: pl.BlockSpec((tm, width), lambda s: (jnp.minimum(s, n_tiles - 1), 0))
    prev = lambda width: pl.BlockSpec((tm, width), lambda s: (jnp.maximum(s - 1, 0), 0))
    tail = (g_kv, w_kv, g_k, g_q1, w_q, g_q, avg)
    return pl.pallas_call(
        _mlp_proj_kernel,
        grid=(n_tiles + 1,),
        in_specs=[cur(D_MODEL), _resident(g_ffn), _resident_layer(w_up, 0), _resident_layer(w_down, 0)]
        + [_resident(w) for w in tail],
        out_specs=(cur(D_MODEL), prev(D_MODEL), prev(2 * D_KV), prev(2 * D_KV), prev(D_KV), prev(D_KV)),
        out_shape=(jax.ShapeDtypeStruct((rows, D_MODEL), F32),
                   jax.ShapeDtypeStruct((rows, D_MODEL), BF16),
                   jax.ShapeDtypeStruct((rows, 2 * D_KV), BF16),
                   jax.ShapeDtypeStruct((rows, 2 * D_KV), BF16),
                   jax.ShapeDtypeStruct((rows, D_KV), F32),
                   jax.ShapeDtypeStruct((rows, D_KV), F32)),
        scratch_shapes=[pltpu.VMEM((2, tm, D_MODEL), F32)],
        compiler_params=pltpu.CompilerParams(
            dimension_semantics=("arbitrary",), vmem_limit_bytes=VMEM_LIMIT),
        name="mlp_proj",
    )(x2d, g_ffn, w_up, w_down, *tail)


def _prompt_attn_kernel(q_ref, kd_ref, vd_ref, kdp_ref, vdp_ref, x_ref, bias_ref, sink_ref, lmask_ref,
                        wo_ref, gffn_ref, wup_ref, wdown_ref, y_ref, o_sc, *, tiles_per_seq, n_tiles):
    tq = q_ref.shape[0]
    step = pl.program_id(0)
    slot = step % 2

    @pl.when(step == 0)
    def _():
        o_sc[1] = jnp.zeros(o_sc.shape[1:], o_sc.dtype)

    tile = jnp.minimum(step, n_tiles - 1)
    first = (tile % tiles_per_seq == 0).astype(jnp.int32)
    m_lo = lmask_ref[0:1, :]
    m_hi = lmask_ref[1:2, :]
    ones_lo = jnp.broadcast_to(m_lo, (2 * WINDOW, LANES))
    ones_hi = jnp.broadcast_to(m_hi, (2 * WINDOW, LANES))
    nt = (((1,), (1,)), ((), ()))
    lane = lax.broadcasted_iota(jnp.int32, (WINDOW, LANES), 1)
    pairs_per_kv = Q_PER_KV // 2
    windows = {}

    def window(r, kh):
        if (r, kh) not in windows:
            ls = slice(kh * LANES, (kh + 1) * LANES)
            if r == 0:
                kwin = jnp.concatenate([kdp_ref[:, ls], kd_ref[0:WINDOW, ls]], axis=0)
                vwin = jnp.concatenate([vdp_ref[:, ls], vd_ref[0:WINDOW, ls]], axis=0)
            else:
                kwin = kd_ref[(r - 1) * WINDOW:(r + 1) * WINDOW, ls]
                vwin = vd_ref[(r - 1) * WINDOW:(r + 1) * WINDOW, ls]
            windows[(r, kh)] = (kwin * m_lo, kwin * m_hi,
                                jnp.concatenate([vwin * m_lo, ones_lo], axis=1),
                                jnp.concatenate([vwin * m_hi, ones_hi], axis=1))
        return windows[(r, kh)]

    def scores(p):
        r, blk = divmod(p, N_LANE_BLOCKS)
        k_lo, k_hi, _, _ = window(r, blk // pairs_per_kv)
        sel = first if r == 0 else 0
        qp = q_ref[r * WINDOW:(r + 1) * WINDOW, blk * LANES:(blk + 1) * LANES]
        s0 = lax.dot_general(qp, k_lo, nt, preferred_element_type=F32) + bias_ref[sel, 2 * blk]
        s1 = lax.dot_general(qp, k_hi, nt, preferred_element_type=F32) + bias_ref[sel, 2 * blk + 1]
        return s0, s1

    def finish(p, s0, s1):
        r, blk = divmod(p, N_LANE_BLOCKS)
        _, _, v_lo, v_hi = window(r, blk // pairs_per_kv)
        h0 = 2 * blk
        m0 = jnp.maximum(jnp.max(s0, axis=-1, keepdims=True), sink_ref[:, h0:h0 + 1])
        m1 = jnp.maximum(jnp.max(s1, axis=-1, keepdims=True), sink_ref[:, h0 + 1:h0 + 2])
        p0 = jnp.exp(s0 - m0).astype(BF16)
        p1 = jnp.exp(s1 - m1).astype(BF16)
        ext = (jnp.dot(p0, v_lo, preferred_element_type=F32)
               + jnp.dot(p1, v_hi, preferred_element_type=F32))
        e0 = jnp.exp(sink_ref[:, h0:h0 + 1] - m0)
        e1 = jnp.exp(sink_ref[:, h0 + 1:h0 + 2] - m1)
        den = ext[:, LANES:] + jnp.where(lane < HEAD_DIM, e0, e1)
        o_sc[slot, r * WINDOW:(r + 1) * WINDOW, blk * LANES:(blk + 1) * LANES] = (
            (ext[:, :LANES] / den).astype(BF16))

    n_col = D_MODEL // MLP_PIECE
    n_chunks = wup_ref.shape[1] // D_MODEL
    o_prev = o_sc[1 - slot]
    state = {}

    def wo_piece(n):
        cs = slice(n * MLP_PIECE, (n + 1) * MLP_PIECE)
        state.setdefault("acc", []).append(
            x_ref[:, cs] + jnp.dot(o_prev, wo_ref[:, cs], preferred_element_type=F32))
        if n == n_col - 1:
            state["hn"] = _rms(jnp.concatenate(state["acc"], axis=1), gffn_ref[...]).astype(BF16)

    schedule = ([functools.partial(wo_piece, n) for n in range(n_col)]
                + [functools.partial(_mlp_piece, state, *p, wup_ref, wdown_ref)
                   for p in _mlp_schedule(n_chunks, n_col)])

    n_pairs = (tq // WINDOW) * N_LANE_BLOCKS
    pending = scores(0)
    for k in range(max(len(schedule), n_pairs)):
        nxt = scores(k + 1) if k + 1 < n_pairs else None
        if k < len(schedule):
            schedule[k]()
        if k < n_pairs:
            finish(k, *pending)
        pending = nxt
    y_ref[...] = jnp.concatenate(state["acc"], axis=1)


def _prompt_attention(q, kd, vd, x2, bias, sinks, lmask, w_o, g_ffn, w_up, w_down, *, bsz, length, tq):
    nt = length // tq
    n_tiles = bsz * nt
    per_win = tq // WINDOW
    att_tile = lambda s: jnp.minimum(s, n_tiles - 1)
    mlp_tile = lambda s: jnp.maximum(s - 1, 0)
    att = lambda width: pl.BlockSpec((tq, width), lambda s: (att_tile(s), 0))
    mlp = pl.BlockSpec((tq, D_MODEL), lambda s: (mlp_tile(s), 0))
    prev = pl.BlockSpec((WINDOW, 2 * D_KV), lambda s: (jnp.maximum(att_tile(s) * per_win - 1, 0), 0))
    return pl.pallas_call(
        functools.partial(_prompt_attn_kernel, tiles_per_seq=nt, n_tiles=n_tiles),
        grid=(n_tiles + 1,),
        in_specs=[att(D_MODEL), att(2 * D_KV), att(2 * D_KV), prev, prev, mlp,
                  _resident(bias), _resident(sinks), _resident(lmask), _resident(w_o), _resident(g_ffn),
                  _resident_layer(w_up, 1), _resident_layer(w_down, 1)],
        out_specs=mlp,
        out_shape=jax.ShapeDtypeStruct((bsz * length, D_MODEL), F32),
        scratch_shapes=[pltpu.VMEM((2, tq, D_MODEL), BF16)],
        compiler_params=pltpu.CompilerParams(
            dimension_semantics=("arbitrary",), vmem_limit_bytes=VMEM_LIMIT),
        name="prompt_attention",
    )(q, kd, vd, kd, vd, x2, bias, sinks, lmask, w_o, g_ffn, w_up, w_down)


def _alibi_slopes():
    return (2.0 ** (-8.0 * np.arange(1, N_HEADS + 1, dtype=np.float32) / N_HEADS)).astype(np.float32)


def _prompt_bias():
    qi = np.arange(WINDOW)[:, None]
    kj = np.arange(2 * WINDOW)[None, :]
    dist = qi - kj + WINDOW
    valid = (dist >= 0) & (dist < WINDOW)
    valid = np.stack([valid, valid & (kj >= WINDOW)], axis=0)[:, None]
    pen = -_alibi_slopes()[:, None, None] * dist.astype(np.float32)[None]
    return np.where(valid, pen[None], np.float32(MASKED)).astype(np.float32)


def _sample_attn_kernel(q_ref, kc_ref, vc_ref, kn_ref, vn_ref, bias_c_ref, bias_n_ref, sink_ref,
                        o_ref, kbuf_ref, vbuf_ref):
    g = kc_ref.shape[0]
    t_new = kn_ref.shape[0] // g
    nt = (((1,), (1,)), ((), ()))
    pad = jnp.zeros((LANES - g * t_new, D_KV), F32)
    kn_t = jnp.concatenate([kn_ref[...], pad], axis=0).T
    vn_t = jnp.concatenate([vn_ref[...], pad], axis=0).T
    lane = lax.broadcasted_iota(jnp.int32, (HEAD_DIM, WINDOW), 1)
    is_new = lane >= WINDOW - t_new

    for kh in range(N_KV_HEADS):
        hs = slice(kh * HEAD_DIM, (kh + 1) * HEAD_DIM)
        q = q_ref[kh]
        k_t = jnp.concatenate([kc_ref[s, kh] for s in range(g)], axis=1)
        v_t = jnp.concatenate([vc_ref[s, kh] for s in range(g)], axis=1)
        s_c = jnp.dot(q, k_t.astype(BF16), preferred_element_type=F32) + bias_c_ref[kh]
        s_n = jnp.dot(q, kn_t[hs].astype(BF16), preferred_element_type=F32) + bias_n_ref[kh]
        sink = sink_ref[kh]
        m = jnp.maximum(jnp.maximum(jnp.max(s_c, axis=-1, keepdims=True),
                                    jnp.max(s_n, axis=-1, keepdims=True)), sink)
        p_c = jnp.exp(s_c - m)
        p_n = jnp.exp(s_n - m)
        den = (jnp.sum(p_c, axis=-1, keepdims=True) + jnp.sum(p_n, axis=-1, keepdims=True)
               + jnp.exp(sink - m))
        o = (lax.dot_general(p_c.astype(BF16), v_t.astype(BF16), nt, preferred_element_type=F32)
             + lax.dot_general(p_n.astype(BF16), vn_t[hs].astype(BF16), nt, preferred_element_type=F32))
        o_ref[kh] = o / den

        for s in range(g):
            shift_new = (WINDOW - t_new - s * t_new) % WINDOW
            k_new = pltpu.roll(kn_t[hs], shift_new, 1)
            v_new = pltpu.roll(vn_t[hs], shift_new, 1)
            kbuf_ref[s, kh] = jnp.where(is_new, k_new, pltpu.roll(kc_ref[s, kh], WINDOW - t_new, 1))
            vbuf_ref[s, kh] = jnp.where(is_new, v_new, pltpu.roll(vc_ref[s, kh], WINDOW - t_new, 1))


def _sample_attention(q_heads, cache_kt, cache_vt, k_new, v_new, bias_c, bias_n, sink_rows, *, t_new):
    g = SAMPLE_GROUP
    bsz = cache_kt.shape[0]
    rows_q = g * Q_PER_KV * t_new
    cache_spec = pl.BlockSpec((g, N_KV_HEADS, HEAD_DIM, WINDOW), lambda i: (i, 0, 0, 0))
    q_spec = pl.BlockSpec((N_KV_HEADS, rows_q, HEAD_DIM), lambda i: (0, i, 0))
    new_spec = pl.BlockSpec((g * t_new, D_KV), lambda i: (i, 0))
    return pl.pallas_call(
        _sample_attn_kernel,
        grid=(bsz // g,),
        in_specs=[q_spec, cache_spec, cache_spec, new_spec, new_spec,
                  _resident(bias_c), _resident(bias_n), _resident(sink_rows)],
        out_specs=(q_spec, cache_spec, cache_spec),
        out_shape=(jax.ShapeDtypeStruct(q_heads.shape, F32),
                   jax.ShapeDtypeStruct(cache_kt.shape, cache_kt.dtype),
                   jax.ShapeDtypeStruct(cache_vt.shape, cache_vt.dtype)),
        compiler_params=pltpu.CompilerParams(
            dimension_semantics=("parallel",), vmem_limit_bytes=VMEM_LIMIT),
        name="sample_attention",
    )(q_heads, cache_kt, cache_vt, k_new, v_new, bias_c, bias_n, sink_rows)


def _sample_bias(t_new):
    g = SAMPLE_GROUP
    slope = _alibi_slopes().reshape(N_KV_HEADS, 1, Q_PER_KV, 1, 1, 1)
    seq_q = np.arange(g).reshape(1, g, 1, 1, 1, 1)
    t = np.arange(t_new).reshape(1, 1, 1, t_new, 1, 1)
    seq_k = np.arange(g).reshape(1, 1, 1, 1, g, 1)
    j = np.arange(WINDOW).reshape(1, 1, 1, 1, 1, WINDOW)
    dist = t + WINDOW - j
    ok = (seq_q == seq_k) & (dist >= 0) & (dist < WINDOW)
    rows = g * Q_PER_KV * t_new
    bias_c = np.where(ok, -slope * dist.astype(np.float32), np.float32(MASKED))
    bias_c = bias_c.reshape(N_KV_HEADS, rows, g * WINDOW).astype(np.float32)
    tn = np.arange(t_new).reshape(1, 1, 1, 1, 1, t_new)
    dist_n = t - tn
    ok_n = (seq_q == seq_k) & (dist_n >= 0) & (dist_n < WINDOW)
    bias_n = np.where(ok_n, -slope * dist_n.astype(np.float32), np.float32(MASKED))
    bias_n = bias_n.reshape(N_KV_HEADS, rows, g * t_new)
    bias_n = np.pad(bias_n, ((0, 0), (0, 0), (0, LANES - g * t_new)), constant_values=MASKED)
    return bias_c, bias_n.astype(np.float32)


def _out_mlp_kernel(o_ref, x_ref, wo_ref, gffn_ref, wup_ref, wdown_ref, y_ref):
    x3 = x_ref[...] + jnp.dot(o_ref[...], wo_ref[...], preferred_element_type=F32)
    y_ref[...] = _mlp(x3, gffn_ref[...], wup_ref, wdown_ref)


def _out_mlp(o, x2, w_o, g_ffn, w_up, w_down, tm):
    rows = x2.shape[0]
    row = pl.BlockSpec((tm, D_MODEL), lambda i: (i, 0))
    return pl.pallas_call(
        _out_mlp_kernel,
        grid=(rows // tm,),
        in_specs=[row, row, _resident(w_o), _resident(g_ffn), _resident_layer(w_up, 1),
                  _resident_layer(w_down, 1)],
        out_specs=row,
        out_shape=jax.ShapeDtypeStruct((rows, D_MODEL), F32),
        compiler_params=pltpu.CompilerParams(
            dimension_semantics=("parallel",), vmem_limit_bytes=VMEM_LIMIT),
        name="out_mlp",
    )(o, x2, w_o, g_ffn, w_up, w_down)


def kernel(x_prompt, x_sample, state_ssm_re, state_ssm_im, cache_k, cache_v, norm_mix, norm_ffn,
           ssm_a_re, ssm_a_im, ssm_log_dt, ssm_b_re, ssm_b_im, ssm_c_re, ssm_c_im, ssm_d, ssm_w_glu,
           kv_norm, w_kv, k_norm, w_q, q_norm, attn_sinks, w_o, w_up, w_down):
    bsz, length, d = x_prompt.shape
    dec_b, dec_t, _ = x_sample.shape

    ab_re, ab_im, bb_re, bb_im = _s5_discretize(ssm_a_re[0], ssm_a_im[0], ssm_log_dt[0],
                                                ssm_b_re[0], ssm_b_im[0])
    b_blk, c_blk = _block_diag_weights(bb_re, bb_im, ssm_c_re[0], ssm_c_im[0])
    ab_re = ab_re.reshape(1, D_STATE)
    ab_im = ab_im.reshape(1, D_STATE)
    row = lambda v: v.reshape(1, -1).astype(F32)
    w_glu = ssm_w_glu[0].astype(BF16)
    w_kv_b = w_kv.astype(BF16)
    w_q0 = w_q[0].astype(BF16)
    w_o0 = w_o[0].astype(BF16)
    g_k = row(jnp.tile(k_norm, D_KV // HEAD_DIM))
    g_q = row(jnp.tile(q_norm[0], D_KV // HEAD_DIM))
    avg = jnp.asarray(np.kron(np.eye(D_KV // HEAD_DIM, dtype=np.float32),
                              np.full((HEAD_DIM, HEAD_DIM), 1.0 / HEAD_DIM, np.float32)), BF16)
    lane = np.arange(LANES)
    lmask = jnp.asarray(np.stack([lane < HEAD_DIM, lane >= HEAD_DIM]), BF16)
    sinks = row(attn_sinks[0])

    def layer0(x, h0r, h0i, tc, state_lanes_batch, cast=()):
        return _s5_layer(x, h0r, h0i, row(norm_mix[0]), row(ssm_d[0]), ab_re, ab_im, b_blk, c_blk,
                         w_glu, tc=tc, state_lanes_batch=state_lanes_batch, cast=cast)

    def mlp_proj(x2d, tm):
        return _mlp_proj(x2d, tm, row(norm_ffn[0]), w_up_b, w_down_b, row(kv_norm), w_kv_b, g_k,
                         row(norm_mix[1]), w_q0, g_q, avg)

    zeros = jnp.zeros((bsz, D_STATE), F32)
    x1, hr_p, hi_p, w_up_b, w_down_b = layer0(
        x_prompt, zeros, zeros, 32, False,
        cast=(w_up.reshape(-1, w_up.shape[-1]), w_down.reshape(-1, w_down.shape[-1])))
    w_up_b = w_up_b.reshape(w_up.shape)
    w_down_b = w_down_b.reshape(w_down.shape)
    x2, q, kd, vd, k, v = mlp_proj(x1.reshape(bsz * length, d), 512)
    y_prompt = _prompt_attention(q, kd, vd, x2, jnp.asarray(_prompt_bias()), sinks, lmask, w_o0,
                                 row(norm_ffn[1]), w_up_b, w_down_b, bsz=bsz, length=length, tq=512)
    y_prompt = y_prompt.reshape(bsz, length, d)
    k_p = k.reshape(bsz, length, D_KV)[:, -WINDOW:].reshape(bsz, WINDOW, N_KV_HEADS, HEAD_DIM)
    v_p = v.reshape(bsz, length, D_KV)[:, -WINDOW:].reshape(bsz, WINDOW, N_KV_HEADS, HEAD_DIM)
    state_shape = (1, bsz, N_GROUPS, STATE_N)
    re_p = hr_p.reshape(state_shape).astype(state_ssm_re.dtype)
    im_p = hi_p.reshape(state_shape).astype(state_ssm_im.dtype)

    h0r_t = jnp.transpose(state_ssm_re[0], (1, 2, 0)).reshape(D_STATE, dec_b).astype(F32)
    h0i_t = jnp.transpose(state_ssm_im[0], (1, 2, 0)).reshape(D_STATE, dec_b).astype(F32)
    x1s, hr_s, hi_s = layer0(x_sample, h0r_t, h0i_t, dec_t, True)
    x2s, qs, _, _, ks, vs = mlp_proj(x1s.reshape(dec_b * dec_t, d), dec_b * dec_t)
    q_heads = qs.reshape(dec_b, dec_t, N_KV_HEADS, Q_PER_KV, HEAD_DIM).transpose(2, 0, 3, 1, 4)
    q_heads = q_heads.reshape(N_KV_HEADS, dec_b * Q_PER_KV * dec_t, HEAD_DIM)
    bias_c, bias_n = _sample_bias(dec_t)
    sink_rows = jnp.broadcast_to(
        attn_sinks[0].astype(F32).reshape(N_KV_HEADS, 1, Q_PER_KV, 1),
        (N_KV_HEADS, SAMPLE_GROUP, Q_PER_KV, dec_t)).reshape(N_KV_HEADS, -1, 1)
    o_heads, k_s_t, v_s_t = _sample_attention(
        q_heads, jnp.transpose(cache_k, (0, 2, 3, 1)), jnp.transpose(cache_v, (0, 2, 3, 1)),
        ks.astype(cache_k.dtype), vs.astype(cache_v.dtype), jnp.asarray(bias_c), jnp.asarray(bias_n),
        sink_rows, t_new=dec_t)
    o_s = o_heads.reshape(N_KV_HEADS, dec_b, Q_PER_KV, dec_t, HEAD_DIM).transpose(1, 3, 0, 2, 4)
    o_s = o_s.reshape(dec_b * dec_t, d).astype(BF16)
    y_sample = _out_mlp(o_s, x2s, w_o0, row(norm_ffn[1]), w_up_b, w_down_b, dec_b * dec_t)
    y_sample = y_sample.reshape(dec_b, dec_t, d)
    k_s = jnp.transpose(k_s_t, (0, 3, 1, 2))
    v_s = jnp.transpose(v_s_t, (0, 3, 1, 2))
    re_s = jnp.transpose(hr_s.reshape(N_GROUPS, STATE_N, dec_b), (2, 0, 1))[None].astype(state_ssm_re.dtype)
    im_s = jnp.transpose(hi_s.reshape(N_GROUPS, STATE_N, dec_b), (2, 0, 1))[None].astype(state_ssm_im.dtype)

    return (y_prompt, y_sample, re_p, im_p, k_p, v_p, re_s, im_s, k_s, v_s)
```

```python
import functools

import jax
import jax.numpy as jnp
import numpy as np
from jax import lax
from jax.experimental import pallas as pl
from jax.experimental.pallas import tpu as pltpu

F32 = jnp.float32
BF16 = jnp.bfloat16

D_MODEL = 1024
GROUP_SIZE = 16
N_GROUPS = D_MODEL // GROUP_SIZE
STATE_N = 64
D_STATE = N_GROUPS * STATE_N
HEAD_DIM = 64
N_HEADS = D_MODEL // HEAD_DIM
N_KV_HEADS = 4
Q_PER_KV = N_HEADS // N_KV_HEADS
D_KV = N_KV_HEADS * HEAD_DIM
WINDOW = 128
D_FF = 4 * D_MODEL
EPS = 1e-6
MASKED = -1e30

LANES = 128
GROUPS_PER_BLOCK = LANES // GROUP_SIZE
N_LANE_BLOCKS = D_MODEL // LANES
STATES_PER_BLOCK = GROUPS_PER_BLOCK * STATE_N
SUB_BATCH = 16
SCAN_LANES = 256
SAMPLE_GROUP = 8
MLP_PIECE = 256
ROW_TILE = 512
S5_TIME_CHUNK = 32
VMEM_LIMIT = 56 * 1024 * 1024


def _resident(a):
    return pl.BlockSpec(a.shape, lambda *_: (0,) * a.ndim, pipeline_mode=pl.Buffered(1))


def _resident_layer(a, layer):
    return pl.BlockSpec((None,) + a.shape[1:], lambda *_: (layer,) + (0,) * (a.ndim - 1),
                        pipeline_mode=pl.Buffered(1))


def _rms(x, g):
    return x * lax.rsqrt(jnp.mean(x * x, axis=-1, keepdims=True) + EPS) * g


def _head_mean_sq(x, avg):
    return jnp.dot((x * x).astype(BF16), avg, preferred_element_type=F32)


def _discretize_kernel(a_re_ref, a_im_ref, log_dt_ref, b_re_ref, b_im_ref,
                       ab_re_ref, ab_im_ref, bb_re_ref, bb_im_ref):
    dt = jnp.exp(log_dt_ref[...])
    a_re = a_re_ref[...]
    a_im = a_im_ref[...]
    mag = jnp.exp(dt * a_re)
    ab_re = mag * jnp.cos(dt * a_im)
    ab_im = mag * jnp.sin(dt * a_im)
    den = a_re * a_re + a_im * a_im
    nr = ab_re - 1.0
    ni = ab_im
    f_re = (nr * a_re + ni * a_im) / den
    f_im = (ni * a_re - nr * a_im) / den
    ab_re_ref[...] = ab_re
    ab_im_ref[...] = ab_im
    b_re = b_re_ref[...]
    b_im = b_im_ref[...]
    fr = f_re[:, None, :]
    fi = f_im[:, None, :]
    bb_re_ref[...] = fr * b_re - fi * b_im
    bb_im_ref[...] = fr * b_im + fi * b_re


def _s5_discretize(a_re, a_im, log_dt, b_re, b_im):
    g, n = a_re.shape
    c = b_re.shape[-1]
    out = pl.pallas_call(
        _discretize_kernel,
        out_shape=(jax.ShapeDtypeStruct((g, n), F32), jax.ShapeDtypeStruct((g, n), F32),
                   jax.ShapeDtypeStruct((g, c, n), F32), jax.ShapeDtypeStruct((g, c, n), F32)),
        name="s5_discretize",
    )(a_re, a_im, log_dt.reshape(g, 1), jnp.swapaxes(b_re, 1, 2), jnp.swapaxes(b_im, 1, 2))
    return out


def _block_diag_weights(bb_re, bb_im, c_re, c_im):
    nb, gb = N_LANE_BLOCKS, GROUPS_PER_BLOCK
    own = jnp.asarray(np.arange(LANES)[:, None] // GROUP_SIZE == np.arange(STATES_PER_BLOCK)[None, :] // STATE_N)

    def b_half(bb):
        return jnp.where(own, jnp.tile(bb.astype(BF16).reshape(nb, LANES, STATE_N), (1, 1, gb)), 0)

    def c_half(cc):
        rows = jnp.swapaxes(cc.astype(BF16), 1, 2).reshape(nb, STATES_PER_BLOCK, GROUP_SIZE)
        return jnp.where(own.T, jnp.tile(rows, (1, 1, gb)), 0)

    b_blk = jnp.concatenate([b_half(bb_re), b_half(bb_im)], axis=2)
    c_blk = jnp.concatenate([c_half(c_re), c_half(-c_im)], axis=1)
    return b_blk, c_blk


def _s5_layer_kernel(*refs, state_lanes_batch, n_cast):
    n_in = 10 + n_cast
    (x_hbm, h0r_ref, h0i_ref, g_ref, d_ref, abr_ref, abi_ref, bblk_ref, cblk_ref, wglu_ref) = refs[:10]
    o_hbm, hro_ref, hio_ref = refs[n_in:n_in + 3]
    (xbuf, obuf, in_sem, out_sem, hr_sc, hi_sc, bur, bui, xsr, xsi, hn_sc, ge_sc) = refs[n_in + 3 + n_cast:]
    for src, dst in zip(refs[10:n_in], refs[n_in + 3:n_in + 3 + n_cast]):
        dst[...] = src[...].astype(dst.dtype)

    _, tc, bb, d = xbuf.shape
    rows = tc * bb
    spb = STATES_PER_BLOCK
    step_idx = pl.program_id(0)
    n_chunks = pl.num_programs(0) - 1
    n_xbuf, n_obuf = xbuf.shape[0], obuf.shape[0]

    def in_copy(chunk, b):
        return pltpu.make_async_copy(x_hbm.at[b, pl.ds(chunk * tc, tc), :],
                                     xbuf.at[chunk % n_xbuf, :, b, :], in_sem.at[chunk % n_xbuf])

    def out_copy(chunk, b):
        return pltpu.make_async_copy(obuf.at[chunk % n_obuf, :, b, :],
                                     o_hbm.at[b, pl.ds(chunk * tc, tc), :], out_sem.at[chunk % n_obuf])

    @pl.when(step_idx == 0)
    def _():
        for b in range(bb):
            in_copy(0, b).start()
        ge_sc[1] = jnp.zeros(ge_sc.shape[1:], ge_sc.dtype)
        xbuf[n_xbuf - 1] = jnp.zeros(xbuf.shape[1:], xbuf.dtype)
        if state_lanes_batch:
            hr_sc[...] = h0r_ref[...].T
            hi_sc[...] = h0i_ref[...].T
        else:
            hr_sc[...] = h0r_ref[...]
            hi_sc[...] = h0i_ref[...]

    @pl.when(step_idx + 1 < n_chunks)
    def _():
        for b in range(bb):
            in_copy(step_idx + 1, b).start()

    @pl.when(step_idx >= n_obuf + 1)
    def _():
        for b in range(bb):
            out_copy(step_idx - n_obuf - 1, b).wait()

    prev = step_idx + (n_xbuf * n_obuf - 1)
    ge_prev_slot = prev % 2
    x_prev_slot = prev % n_xbuf
    o_slot = prev % n_obuf
    n_col = d // MLP_PIECE
    gate_first = [n for pair in zip(range(n_col, 2 * n_col), range(n_col)) for n in pair]
    z = {}

    def glu_piece(m):
        n = gate_first[m]
        z[n] = jnp.dot(ge_sc[ge_prev_slot], wglu_ref[:, n * MLP_PIECE:(n + 1) * MLP_PIECE],
                       preferred_element_type=F32)
        if n < n_col:
            cs = slice(n * MLP_PIECE, (n + 1) * MLP_PIECE)
            mix = z.pop(n) * jax.nn.sigmoid(z.pop(n + n_col))
            x_prev = xbuf[x_prev_slot, :, :, cs].reshape(rows, MLP_PIECE)
            obuf[o_slot, :, :, cs] = (x_prev + mix).reshape(tc, bb, MLP_PIECE)

    glu_pieces = [functools.partial(glu_piece, m) for m in range(2 * n_col)]

    def chunk_body():
        x_slot = step_idx % n_xbuf
        ge_slot = step_idx % 2
        for b in range(bb):
            in_copy(step_idx, b).wait()

        hn = _rms(xbuf[x_slot].reshape(rows, d), g_ref[...])
        hn_sc[...] = hn
        hnb = hn.astype(BF16)

        def input_map(j):
            bu = jnp.dot(hnb[:, j * LANES:(j + 1) * LANES], bblk_ref[j], preferred_element_type=F32)
            bur[j] = bu[:, :spb]
            bui[j] = bu[:, spb:]

        def recurrence(j):
            for half in range(spb // SCAN_LANES):
                cs = slice(half * SCAN_LANES, (half + 1) * SCAN_LANES)
                gs = slice(j * spb + half * SCAN_LANES, j * spb + (half + 1) * SCAN_LANES)
                ar = jnp.broadcast_to(abr_ref[:, gs], (SUB_BATCH, SCAN_LANES))
                ai = jnp.broadcast_to(abi_ref[:, gs], (SUB_BATCH, SCAN_LANES))
                for sb in range(bb // SUB_BATCH):
                    seqs = slice(sb * SUB_BATCH, (sb + 1) * SUB_BATCH)
                    h_re = hr_sc[seqs, gs]
                    h_im = hi_sc[seqs, gs]
                    for t in range(tc):
                        rs = slice(t * bb + sb * SUB_BATCH, t * bb + (sb + 1) * SUB_BATCH)
                        n_re = ar * h_re - ai * h_im + bur[j, rs, cs]
                        n_im = ar * h_im + ai * h_re + bui[j, rs, cs]
                        xsr[j, rs, cs] = n_re.astype(BF16)
                        xsi[j, rs, cs] = n_im.astype(BF16)
                        h_re, h_im = n_re, n_im
                    hr_sc[seqs, gs] = h_re
                    hi_sc[seqs, gs] = h_im

        def output_map(j):
            ls = slice(j * LANES, (j + 1) * LANES)
            y = (jnp.dot(xsr[j], cblk_ref[j, :spb, :], preferred_element_type=F32)
                 + jnp.dot(xsi[j], cblk_ref[j, spb:, :], preferred_element_type=F32))
            y = y + d_ref[:, ls] * hn_sc[:, ls]
            ge_sc[ge_slot, :, ls] = jax.nn.gelu(y).astype(BF16)

        pending_glu = list(glu_pieces)
        input_map(0)
        for j in range(N_LANE_BLOCKS):
            if j + 1 < N_LANE_BLOCKS:
                input_map(j + 1)
            recurrence(j)
            if j > 0:
                output_map(j - 1)
            if pending_glu:
                pending_glu.pop(0)()
        output_map(N_LANE_BLOCKS - 1)
        for piece in pending_glu:
            piece()

    @pl.when(step_idx < n_chunks)
    def _():
        chunk_body()

    @pl.when(step_idx == n_chunks)
    def _():
        for piece in glu_pieces:
            piece()

    @pl.when(step_idx >= 1)
    def _():
        for b in range(bb):
            out_copy(step_idx - 1, b).start()

    @pl.when(step_idx == n_chunks)
    def _():
        for back in range(min(n_obuf, 2), 0, -1):
            @pl.when(n_chunks >= back)
            def _(back=back):
                for b in range(bb):
                    out_copy(n_chunks - back, b).wait()
        if state_lanes_batch:
            hro_ref[...] = hr_sc[...].T
            hio_ref[...] = hi_sc[...].T
        else:
            hro_ref[...] = hr_sc[...]
            hio_ref[...] = hi_sc[...]


def _s5_layer(x, h0r, h0i, g_mix, d_skip, ab_re, ab_im, b_blk, c_blk, w_glu, *, tc, state_lanes_batch,
              cast=()):
    bsz, length, d = x.shape
    rows = tc * bsz
    n_chunks = length // tc
    weights = (g_mix, d_skip, ab_re, ab_im, b_blk, c_blk, w_glu)
    state_shape = jax.ShapeDtypeStruct(h0r.shape, F32)
    block_shape = (N_LANE_BLOCKS, rows, STATES_PER_BLOCK)
    cast_specs = [pl.BlockSpec((a.shape[0] // n_chunks, a.shape[1]),
                               lambda i: (jnp.minimum(i, n_chunks - 1), 0)) for a in cast]
    return pl.pallas_call(
        functools.partial(_s5_layer_kernel, state_lanes_batch=state_lanes_batch, n_cast=len(cast)),
        grid=(n_chunks + 1,),
        in_specs=[pl.BlockSpec(memory_space=pl.ANY), _resident(h0r), _resident(h0i)]
        + [_resident(w) for w in weights] + cast_specs,
        out_specs=[pl.BlockSpec(memory_space=pl.ANY), pl.BlockSpec(h0r.shape, lambda i: (0, 0)),
                   pl.BlockSpec(h0i.shape, lambda i: (0, 0))] + cast_specs,
        out_shape=[jax.ShapeDtypeStruct((bsz, length, d), F32), state_shape, state_shape]
        + [jax.ShapeDtypeStruct(a.shape, BF16) for a in cast],
        scratch_shapes=[pltpu.VMEM((3, tc, bsz, d), F32), pltpu.VMEM((2, tc, bsz, d), F32),
                        pltpu.SemaphoreType.DMA((3,)), pltpu.SemaphoreType.DMA((2,)),
                        pltpu.VMEM((bsz, D_STATE), F32), pltpu.VMEM((bsz, D_STATE), F32),
                        pltpu.VMEM(block_shape, F32), pltpu.VMEM(block_shape, F32),
                        pltpu.VMEM(block_shape, BF16), pltpu.VMEM(block_shape, BF16),
                        pltpu.VMEM((rows, d), F32), pltpu.VMEM((2, rows, d), BF16)],
        compiler_params=pltpu.CompilerParams(
            dimension_semantics=("arbitrary",), vmem_limit_bytes=VMEM_LIMIT),
        name="s5_layer",
    )(x, h0r, h0i, *weights, *cast)


def _mlp_schedule(n_chunks, n_col):
    schedule = [("up", 0, n) for n in range(n_col)]
    for c in range(n_chunks):
        if c + 1 < n_chunks:
            schedule += [("up", c + 1, n) for n in range(n_col)]
        schedule += [("down", c, n) for n in range(n_col)]
    return schedule


def _mlp_piece(state, kind, c, n, w_up_ref, w_down_ref):
    if kind == "up":
        hs = slice(c * D_MODEL + n * MLP_PIECE, c * D_MODEL + (n + 1) * MLP_PIECE)
        u = jnp.dot(state["hn"], w_up_ref[:, hs], preferred_element_type=F32)
        state.setdefault(("u", c), []).append(jnp.square(jnp.maximum(u, 0.0)).astype(BF16))
    else:
        if n == 0:
            state[("uc", c)] = jnp.concatenate(state.pop(("u", c)), axis=1)
        cs = slice(n * MLP_PIECE, (n + 1) * MLP_PIECE)
        state["acc"][n] = state["acc"][n] + jnp.dot(
            state[("uc", c)], w_down_ref[c * D_MODEL:(c + 1) * D_MODEL, cs], preferred_element_type=F32)


def _mlp(x, g, w_up_ref, w_down_ref):
    n_col = D_MODEL // MLP_PIECE
    state = {"hn": _rms(x, g).astype(BF16),
             "acc": [x[:, n * MLP_PIECE:(n + 1) * MLP_PIECE] for n in range(n_col)]}
    for piece in _mlp_schedule(w_up_ref.shape[1] // D_MODEL, n_col):
        _mlp_piece(state, *piece, w_up_ref, w_down_ref)
    return jnp.concatenate(state["acc"], axis=1)


def _mlp_proj_kernel(x_ref, gffn_ref, wup_ref, wdown_ref, gkv_ref, wkv_ref, gk_ref, gq1_ref, wq_ref,
                     gq_ref, avg_ref, x2_ref, q_ref, kd_ref, vd_ref, k_ref, v_ref, x2_sc):
    tm = x_ref.shape[0]
    step = pl.program_id(0)
    slot = step % 2

    @pl.when(step == 0)
    def _():
        x2_sc[1] = jnp.zeros(x2_sc.shape[1:], x2_sc.dtype)

    n_col = D_MODEL // MLP_PIECE
    x = x_ref[...]
    mlp = {"hn": _rms(x, gffn_ref[...]).astype(BF16),
           "acc": [x[:, n * MLP_PIECE:(n + 1) * MLP_PIECE] for n in range(n_col)]}
    schedule = _mlp_schedule(wup_ref.shape[1] // D_MODEL, n_col)

    prev = x2_sc[1 - slot]
    avg = avg_ref[...]
    lane = lax.broadcasted_iota(jnp.int32, (tm, LANES), 1)
    proj = {}

    def kv_matmul():
        normed = prev * lax.rsqrt(jnp.mean(prev * prev, axis=-1, keepdims=True) + EPS)
        proj["hq"] = (normed * gq1_ref[...]).astype(BF16)
        proj["kv"] = jnp.dot((normed * gkv_ref[...]).astype(BF16), wkv_ref[...],
                             preferred_element_type=F32)

    def kv_finish():
        k = proj["kv"][:, :D_KV]
        k = k * lax.rsqrt(_head_mean_sq(k, avg) + EPS) * gk_ref[...]
        v = proj["kv"][:, D_KV:]
        k_ref[...] = k
        v_ref[...] = v
        for src, dst_ref in ((k, kd_ref), (v, vd_ref)):
            for blk in range(D_KV // LANES):
                a = src[:, blk * LANES:(blk + 1) * LANES]
                swapped = pltpu.roll(a, HEAD_DIM, 1)
                dst_ref[:, 2 * blk * LANES:(2 * blk + 1) * LANES] = (
                    jnp.where(lane < HEAD_DIM, a, swapped).astype(BF16))
                dst_ref[:, (2 * blk + 1) * LANES:(2 * blk + 2) * LANES] = (
                    jnp.where(lane < HEAD_DIM, swapped, a).astype(BF16))

    def q_matmul(c):
        proj[("q", c)] = jnp.dot(proj["hq"], wq_ref[:, c * D_KV:(c + 1) * D_KV],
                                 preferred_element_type=F32)

    def q_finish(c):
        q = proj.pop(("q", c))
        q = q * lax.rsqrt(_head_mean_sq(q, avg) + EPS) * gq_ref[...]
        q_ref[:, c * D_KV:(c + 1) * D_KV] = (q * (HEAD_DIM ** -0.5)).astype(BF16)

    n_q = D_MODEL // D_KV
    epilogue = [kv_matmul, kv_finish, functools.partial(q_matmul, 0)]
    for c in range(n_q):
        if c + 1 < n_q:
            epilogue.append(functools.partial(q_matmul, c + 1))
        epilogue.append(functools.partial(q_finish, c))

    every = len(schedule) // len(epilogue)
    for i, piece in enumerate(schedule):
        _mlp_piece(mlp, *piece, wup_ref, wdown_ref)
        if i % every == every - 1 and epilogue:
            epilogue.pop(0)()
    for stage in epilogue:
        stage()

    x2 = jnp.concatenate(mlp["acc"], axis=1)
    x2_ref[...] = x2
    x2_sc[slot] = x2


def _mlp_proj(x2d, tm, g_ffn, w_up, w_down, g_kv, w_kv, g_k, g_q1, w_q, g_q, avg):
    rows = x2d.shape[0]
    n_tiles = rows // tm
    cur = lambda width: pl.BlockSpec((tm, width), lambda s: (jnp.minimum(s, n_tiles - 1), 0))
    prev = lambda width: pl.BlockSpec((tm, width), lambda s: (jnp.maximum(s - 1, 0), 0))
    tail = (g_kv, w_kv, g_k, g_q1, w_q, g_q, avg)
    return pl.pallas_call(
        _mlp_proj_kernel,
        grid=(n_tiles + 1,),
        in_specs=[cur(D_MODEL), _resident(g_ffn), _resident_layer(w_up, 0), _resident_layer(w_down, 0)]
        + [_resident(w) for w in tail],
        out_specs=(cur(D_MODEL), prev(D_MODEL), prev(2 * D_KV), prev(2 * D_KV), prev(D_KV), prev(D_KV)),
        out_shape=(jax.ShapeDtypeStruct((rows, D_MODEL), F32),
                   jax.ShapeDtypeStruct((rows, D_MODEL), BF16),
                   jax.ShapeDtypeStruct((rows, 2 * D_KV), BF16),
                   jax.ShapeDtypeStruct((rows, 2 * D_KV), BF16),
                   jax.ShapeDtypeStruct((rows, D_KV), F32),
                   jax.ShapeDtypeStruct((rows, D_KV), F32)),
        scratch_shapes=[pltpu.VMEM((2, tm, D_MODEL), F32)],
        compiler_params=pltpu.CompilerParams(
            dimension_semantics=("arbitrary",), vmem_limit_bytes=VMEM_LIMIT),
        name="mlp_proj",
    )(x2d, g_ffn, w_up, w_down, *tail)


def _prompt_attn_kernel(q_ref, kd_ref, vd_ref, kdp_ref, vdp_ref, x_ref, bias_ref, sink_ref, lmask_ref,
                        wo_ref, gffn_ref, wup_ref, wdown_ref, y_ref, o_sc, *, tiles_per_seq, n_tiles):
    tq = q_ref.shape[0]
    step = pl.program_id(0)
    slot = step % 2

    @pl.when(step == 0)
    def _():
        o_sc[1] = jnp.zeros(o_sc.shape[1:], o_sc.dtype)

    tile = jnp.minimum(step, n_tiles - 1)
    first = (tile % tiles_per_seq == 0).astype(jnp.int32)
    m_lo = lmask_ref[0:1, :]
    m_hi = lmask_ref[1:2, :]
    ones_lo = jnp.broadcast_to(m_lo, (2 * WINDOW, LANES))
    ones_hi = jnp.broadcast_to(m_hi, (2 * WINDOW, LANES))
    nt = (((1,), (1,)), ((), ()))
    lane = lax.broadcasted_iota(jnp.int32, (WINDOW, LANES), 1)
    pairs_per_kv = Q_PER_KV // 2
    windows = {}

    def window(r, kh):
        if (r, kh) not in windows:
            ls = slice(kh * LANES, (kh + 1) * LANES)
            if r == 0:
                kwin = jnp.concatenate([kdp_ref[:, ls], kd_ref[0:WINDOW, ls]], axis=0)
                vwin = jnp.concatenate([vdp_ref[:, ls], vd_ref[0:WINDOW, ls]], axis=0)
            else:
                kwin = kd_ref[(r - 1) * WINDOW:(r + 1) * WINDOW, ls]
                vwin = vd_ref[(r - 1) * WINDOW:(r + 1) * WINDOW, ls]
            windows[(r, kh)] = (kwin * m_lo, kwin * m_hi,
                                jnp.concatenate([vwin * m_lo, ones_lo], axis=1),
                                jnp.concatenate([vwin * m_hi, ones_hi], axis=1))
        return windows[(r, kh)]

    def scores(p):
        r, blk = divmod(p, N_LANE_BLOCKS)
        k_lo, k_hi, _, _ = window(r, blk // pairs_per_kv)
        sel = first if r == 0 else 0
        qp = q_ref[r * WINDOW:(r + 1) * WINDOW, blk * LANES:(blk + 1) * LANES]
        s0 = lax.dot_general(qp, k_lo, nt, preferred_element_type=F32) + bias_ref[sel, 2 * blk]
        s1 = lax.dot_general(qp, k_hi, nt, preferred_element_type=F32) + bias_ref[sel, 2 * blk + 1]
        return s0, s1

    def finish(p, s0, s1):
        r, blk = divmod(p, N_LANE_BLOCKS)
        _, _, v_lo, v_hi = window(r, blk // pairs_per_kv)
        h0 = 2 * blk
        m0 = jnp.maximum(jnp.max(s0, axis=-1, keepdims=True), sink_ref[:, h0:h0 + 1])
        m1 = jnp.maximum(jnp.max(s1, axis=-1, keepdims=True), sink_ref[:, h0 + 1:h0 + 2])
        p0 = jnp.exp(s0 - m0).astype(BF16)
        p1 = jnp.exp(s1 - m1).astype(BF16)
        ext = (jnp.dot(p0, v_lo, preferred_element_type=F32)
               + jnp.dot(p1, v_hi, preferred_element_type=F32))
        e0 = jnp.exp(sink_ref[:, h0:h0 + 1] - m0)
        e1 = jnp.exp(sink_ref[:, h0 + 1:h0 + 2] - m1)
        den = ext[:, LANES:] + jnp.where(lane < HEAD_DIM, e0, e1)
        o_sc[slot, r * WINDOW:(r + 1) * WINDOW, blk * LANES:(blk + 1) * LANES] = (
            (ext[:, :LANES] / den).astype(BF16))

    n_col = D_MODEL // MLP_PIECE
    n_chunks = wup_ref.shape[1] // D_MODEL
    o_prev = o_sc[1 - slot]
    state = {}

    def wo_piece(n):
        cs = slice(n * MLP_PIECE, (n + 1) * MLP_PIECE)
        state.setdefault("acc", []).append(
            x_ref[:, cs] + jnp.dot(o_prev, wo_ref[:, cs], preferred_element_type=F32))
        if n == n_col - 1:
            state["hn"] = _rms(jnp.concatenate(state["acc"], axis=1), gffn_ref[...]).astype(BF16)

    schedule = ([functools.partial(wo_piece, n) for n in range(n_col)]
                + [functools.partial(_mlp_piece, state, *p, wup_ref, wdown_ref)
                   for p in _mlp_schedule(n_chunks, n_col)])

    n_pairs = (tq // WINDOW) * N_LANE_BLOCKS
    pending = scores(0)
    for k in range(max(len(schedule), n_pairs)):
        nxt = scores(k + 1) if k + 1 < n_pairs else None
        if k < len(schedule):
            schedule[k]()
        if k < n_pairs:
            finish(k, *pending)
        pending = nxt
    y_ref[...] = jnp.concatenate(state["acc"], axis=1)


def _prompt_attention(q, kd, vd, x2, bias, sinks, lmask, w_o, g_ffn, w_up, w_down, *, bsz, length, tq):
    nt = length // tq
    n_tiles = bsz * nt
    per_win = tq // WINDOW
    att_tile = lambda s: jnp.minimum(s, n_tiles - 1)
    mlp_tile = lambda s: jnp.maximum(s - 1, 0)
    att = lambda width: pl.BlockSpec((tq, width), lambda s: (att_tile(s), 0))
    mlp = pl.BlockSpec((tq, D_MODEL), lambda s: (mlp_tile(s), 0))
    prev = pl.BlockSpec((WINDOW, 2 * D_KV), lambda s: (jnp.maximum(att_tile(s) * per_win - 1, 0), 0))
    return pl.pallas_call(
        functools.partial(_prompt_attn_kernel, tiles_per_seq=nt, n_tiles=n_tiles),
        grid=(n_tiles + 1,),
        in_specs=[att(D_MODEL), att(2 * D_KV), att(2 * D_KV), prev, prev, mlp,
                  _resident(bias), _resident(sinks), _resident(lmask), _resident(w_o), _resident(g_ffn),
                  _resident_layer(w_up, 1), _resident_layer(w_down, 1)],
        out_specs=mlp,
        out_shape=jax.ShapeDtypeStruct((bsz * length, D_MODEL), F32),
        scratch_shapes=[pltpu.VMEM((2, tq, D_MODEL), BF16)],
        compiler_params=pltpu.CompilerParams(
            dimension_semantics=("arbitrary",), vmem_limit_bytes=VMEM_LIMIT),
        name="prompt_attention",
    )(q, kd, vd, kd, vd, x2, bias, sinks, lmask, w_o, g_ffn, w_up, w_down)


def _alibi_slopes():
    return (2.0 ** (-8.0 * np.arange(1, N_HEADS + 1, dtype=np.float32) / N_HEADS)).astype(np.float32)


def _prompt_bias():
    qi = np.arange(WINDOW)[:, None]
    kj = np.arange(2 * WINDOW)[None, :]
    dist = qi - kj + WINDOW
    valid = (dist >= 0) & (dist < WINDOW)
    valid = np.stack([valid, valid & (kj >= WINDOW)], axis=0)[:, None]
    pen = -_alibi_slopes()[:, None, None] * dist.astype(np.float32)[None]
    return np.where(valid, pen[None], np.float32(MASKED)).astype(np.float32)


def _sample_attn_kernel(q_ref, kc_ref, vc_ref, kn_ref, vn_ref, bias_c_ref, bias_n_ref, sink_ref,
                        o_ref, kbuf_ref, vbuf_ref):
    g = kc_ref.shape[0]
    t_new = kn_ref.shape[0] // g
    nt = (((1,), (1,)), ((), ()))
    pad = jnp.zeros((LANES - g * t_new, D_KV), F32)
    kn_t = jnp.concatenate([kn_ref[...], pad], axis=0).T
    vn_t = jnp.concatenate([vn_ref[...], pad], axis=0).T
    lane = lax.broadcasted_iota(jnp.int32, (HEAD_DIM, WINDOW), 1)
    is_new = lane >= WINDOW - t_new

    for kh in range(N_KV_HEADS):
        hs = slice(kh * HEAD_DIM, (kh + 1) * HEAD_DIM)
        q = q_ref[kh]
        k_t = jnp.concatenate([kc_ref[s, kh] for s in range(g)], axis=1)
        v_t = jnp.concatenate([vc_ref[s, kh] for s in range(g)], axis=1)
        s_c = jnp.dot(q, k_t.astype(BF16), preferred_element_type=F32) + bias_c_ref[kh]
        s_n = jnp.dot(q, kn_t[hs].astype(BF16), preferred_element_type=F32) + bias_n_ref[kh]
        sink = sink_ref[kh]
        m = jnp.maximum(jnp.maximum(jnp.max(s_c, axis=-1, keepdims=True),
                                    jnp.max(s_n, axis=-1, keepdims=True)), sink)
        p_c = jnp.exp(s_c - m)
        p_n = jnp.exp(s_n - m)
        den = (jnp.sum(p_c, axis=-1, keepdims=True) + jnp.sum(p_n, axis=-1, keepdims=True)
               + jnp.exp(sink - m))
        o = (lax.dot_general(p_c.astype(BF16), v_t.astype(BF16), nt, preferred_element_type=F32)
             + lax.dot_general(p_n.astype(BF16), vn_t[hs].astype(BF16), nt, preferred_element_type=F32))
        o_ref[kh] = o / den

        for s in range(g):
            shift_new = (WINDOW - t_new - s * t_new) % WINDOW
            k_new = pltpu.roll(kn_t[hs], shift_new, 1)
            v_new = pltpu.roll(vn_t[hs], shift_new, 1)
            kbuf_ref[s, kh] = jnp.where(is_new, k_new, pltpu.roll(kc_ref[s, kh], WINDOW - t_new, 1))
            vbuf_ref[s, kh] = jnp.where(is_new, v_new, pltpu.roll(vc_ref[s, kh], WINDOW - t_new, 1))


def _sample_attention(q_heads, cache_kt, cache_vt, k_new, v_new, bias_c, bias_n, sink_rows, *, t_new):
    g = SAMPLE_GROUP
    bsz = cache_kt.shape[0]
    rows_q = g * Q_PER_KV * t_new
    cache_spec = pl.BlockSpec((g, N_KV_HEADS, HEAD_DIM, WINDOW), lambda i: (i, 0, 0, 0))
    q_spec = pl.BlockSpec((N_KV_HEADS, rows_q, HEAD_DIM), lambda i: (0, i, 0))
    new_spec = pl.BlockSpec((g * t_new, D_KV), lambda i: (i, 0))
    return pl.pallas_call(
        _sample_attn_kernel,
        grid=(bsz // g,),
        in_specs=[q_spec, cache_spec, cache_spec, new_spec, new_spec,
                  _resident(bias_c), _resident(bias_n), _resident(sink_rows)],
        out_specs=(q_spec, cache_spec, cache_spec),
        out_shape=(jax.ShapeDtypeStruct(q_heads.shape, F32),
                   jax.ShapeDtypeStruct(cache_kt.shape, cache_kt.dtype),
                   jax.ShapeDtypeStruct(cache_vt.shape, cache_vt.dtype)),
        compiler_params=pltpu.CompilerParams(
            dimension_semantics=("parallel",), vmem_limit_bytes=VMEM_LIMIT),
        name="sample_attention",
    )(q_heads, cache_kt, cache_vt, k_new, v_new, bias_c, bias_n, sink_rows)


def _sample_bias(t_new):
    g = SAMPLE_GROUP
    slope = _alibi_slopes().reshape(N_KV_HEADS, 1, Q_PER_KV, 1, 1, 1)
    seq_q = np.arange(g).reshape(1, g, 1, 1, 1, 1)
    t = np.arange(t_new).reshape(1, 1, 1, t_new, 1, 1)
    seq_k = np.arange(g).reshape(1, 1, 1, 1, g, 1)
    j = np.arange(WINDOW).reshape(1, 1, 1, 1, 1, WINDOW)
    dist = t + WINDOW - j
    ok = (seq_q == seq_k) & (dist >= 0) & (dist < WINDOW)
    rows = g * Q_PER_KV * t_new
    bias_c = np.where(ok, -slope * dist.astype(np.float32), np.float32(MASKED))
    bias_c = bias_c.reshape(N_KV_HEADS, rows, g * WINDOW).astype(np.float32)
    tn = np.arange(t_new).reshape(1, 1, 1, 1, 1, t_new)
    dist_n = t - tn
    ok_n = (seq_q == seq_k) & (dist_n >= 0) & (dist_n < WINDOW)
    bias_n = np.where(ok_n, -slope * dist_n.astype(np.float32), np.float32(MASKED))
    bias_n = bias_n.reshape(N_KV_HEADS, rows, g * t_new)
    bias_n = np.pad(bias_n, ((0, 0), (0, 0), (0, LANES - g * t_new)), constant_values=MASKED)
    return bias_c, bias_n.astype(np.float32)


def _out_mlp_kernel(o_ref, x_ref, wo_ref, gffn_ref, wup_ref, wdown_ref, y_ref):
    x3 = x_ref[...] + jnp.dot(o_ref[...], wo_ref[...], preferred_element_type=F32)
    y_ref[...] = _mlp(x3, gffn_ref[...], wup_ref, wdown_ref)


def _out_mlp(o, x2, w_o, g_ffn, w_up, w_down, tm):
    rows = x2.shape[0]
    row = pl.BlockSpec((tm, D_MODEL), lambda i: (i, 0))
    return pl.pallas_call(
        _out_mlp_kernel,
        grid=(rows // tm,),
        in_specs=[row, row, _resident(w_o), _resident(g_ffn), _resident_layer(w_up, 1),
                  _resident_layer(w_down, 1)],
        out_specs=row,
        out_shape=jax.ShapeDtypeStruct((rows, D_MODEL), F32),
        compiler_params=pltpu.CompilerParams(
            dimension_semantics=("parallel",), vmem_limit_bytes=VMEM_LIMIT),
        name="out_mlp",
    )(o, x2, w_o, g_ffn, w_up, w_down)


def kernel(x_prompt, x_sample, state_ssm_re, state_ssm_im, cache_k, cache_v, norm_mix, norm_ffn,
           ssm_a_re, ssm_a_im, ssm_log_dt, ssm_b_re, ssm_b_im, ssm_c_re, ssm_c_im, ssm_d, ssm_w_glu,
           kv_norm, w_kv, k_norm, w_q, q_norm, attn_sinks, w_o, w_up, w_down):
    bsz, length, d = x_prompt.shape
    dec_b, dec_t, _ = x_sample.shape

    ab_re, ab_im, bb_re, bb_im = _s5_discretize(ssm_a_re[0], ssm_a_im[0], ssm_log_dt[0],
                                                ssm_b_re[0], ssm_b_im[0])
    b_blk, c_blk = _block_diag_weights(bb_re, bb_im, ssm_c_re[0], ssm_c_im[0])
    ab_re = ab_re.reshape(1, D_STATE)
    ab_im = ab_im.reshape(1, D_STATE)
    row = lambda v: v.reshape(1, -1).astype(F32)
    w_glu = ssm_w_glu[0].astype(BF16)
    g_k = row(jnp.tile(k_norm, D_KV // HEAD_DIM))
    g_q = row(jnp.tile(q_norm[0], D_KV // HEAD_DIM))
    avg = jnp.asarray(np.kron(np.eye(D_KV // HEAD_DIM, dtype=np.float32),
                              np.full((HEAD_DIM, HEAD_DIM), 1.0 / HEAD_DIM, np.float32)), BF16)
    lane = np.arange(LANES)
    lmask = jnp.asarray(np.stack([lane < HEAD_DIM, lane >= HEAD_DIM]), BF16)
    sinks = row(attn_sinks[0])

    def layer0(x, h0r, h0i, tc, state_lanes_batch, cast=()):
        return _s5_layer(x, h0r, h0i, row(norm_mix[0]), row(ssm_d[0]), ab_re, ab_im, b_blk, c_blk,
                         w_glu, tc=tc, state_lanes_batch=state_lanes_batch, cast=cast)

    def mlp_proj(x2d, tm):
        return _mlp_proj(x2d, tm, row(norm_ffn[0]), w_up_b, w_down_b, row(kv_norm), w_kv_b, g_k,
                         row(norm_mix[1]), w_q0, g_q, avg)

    zeros = jnp.zeros((bsz, D_STATE), F32)
    x1, hr_p, hi_p, w_up_b, w_down_b, w_kv_b, w_q0, w_o0 = layer0(
        x_prompt, zeros, zeros, S5_TIME_CHUNK, False,
        cast=(w_up.reshape(-1, w_up.shape[-1]), w_down.reshape(-1, w_down.shape[-1]), w_kv, w_q[0], w_o[0]))
    w_up_b = w_up_b.reshape(w_up.shape)
    w_down_b = w_down_b.reshape(w_down.shape)
    x2, q, kd, vd, k, v = mlp_proj(x1.reshape(bsz * length, d), ROW_TILE)
    y_prompt = _prompt_attention(q, kd, vd, x2, jnp.asarray(_prompt_bias()), sinks, lmask, w_o0,
                                 row(norm_ffn[1]), w_up_b, w_down_b, bsz=bsz, length=length, tq=ROW_TILE)
    y_prompt = y_prompt.reshape(bsz, length, d)
    k_p = k.reshape(bsz, length, D_KV)[:, -WINDOW:].reshape(bsz, WINDOW, N_KV_HEADS, HEAD_DIM)
    v_p = v.reshape(bsz, length, D_KV)[:, -WINDOW:].reshape(bsz, WINDOW, N_KV_HEADS, HEAD_DIM)
    state_shape = (1, bsz, N_GROUPS, STATE_N)
    re_p = hr_p.reshape(state_shape).astype(state_ssm_re.dtype)
    im_p = hi_p.reshape(state_shape).astype(state_ssm_im.dtype)

    h0r_t = jnp.transpose(state_ssm_re[0], (1, 2, 0)).reshape(D_STATE, dec_b).astype(F32)
    h0i_t = jnp.transpose(state_ssm_im[0], (1, 2, 0)).reshape(D_STATE, dec_b).astype(F32)
    x1s, hr_s, hi_s = layer0(x_sample, h0r_t, h0i_t, dec_t, True)
    x2s, qs, _, _, ks, vs = mlp_proj(x1s.reshape(dec_b * dec_t, d), dec_b * dec_t)
    q_heads = qs.reshape(dec_b, dec_t, N_KV_HEADS, Q_PER_KV, HEAD_DIM).transpose(2, 0, 3, 1, 4)
    q_heads = q_heads.reshape(N_KV_HEADS, dec_b * Q_PER_KV * dec_t, HEAD_DIM)
    bias_c, bias_n = _sample_bias(dec_t)
    sink_rows = jnp.broadcast_to(
        attn_sinks[0].astype(F32).reshape(N_KV_HEADS, 1, Q_PER_KV, 1),
        (N_KV_HEADS, SAMPLE_GROUP, Q_PER_KV, dec_t)).reshape(N_KV_HEADS, -1, 1)
    o_heads, k_s_t, v_s_t = _sample_attention(
        q_heads, jnp.transpose(cache_k, (0, 2, 3, 1)), jnp.transpose(cache_v, (0, 2, 3, 1)),
        ks.astype(cache_k.dtype), vs.astype(cache_v.dtype), jnp.asarray(bias_c), jnp.asarray(bias_n),
        sink_rows, t_new=dec_t)
    o_s = o_heads.reshape(N_KV_HEADS, dec_b, Q_PER_KV, dec_t, HEAD_DIM).transpose(1, 3, 0, 2, 4)
    o_s = o_s.reshape(dec_b * dec_t, d).astype(BF16)
    y_sample = _out_mlp(o_s, x2s, w_o0, row(norm_ffn[1]), w_up_b, w_down_b, dec_b * dec_t)
    y_sample = y_sample.reshape(dec_b, dec_t, d)
    k_s = jnp.transpose(k_s_t, (0, 3, 1, 2))
    v_s = jnp.transpose(v_s_t, (0, 3, 1, 2))
    re_s = jnp.transpose(hr_s.reshape(N_GROUPS, STATE_N, dec_b), (2, 0, 1))[None].astype(state_ssm_re.dtype)
    im_s = jnp.transpose(hi_s.reshape(N_GROUPS, STATE_N, dec_b), (2, 0, 1))[None].astype(state_ssm_im.dtype)

    return (y_prompt, y_sample, re_p, im_p, k_p, v_p, re_s, im_s, k_s, v_s)
```

```python
import functools

import jax
import jax.numpy as jnp
import numpy as np
from jax import lax
from jax.experimental import pallas as pl
from jax.experimental.pallas import tpu as pltpu

F32 = jnp.float32
BF16 = jnp.bfloat16

D_MODEL = 1024
GROUP_SIZE = 16
N_GROUPS = D_MODEL // GROUP_SIZE
STATE_N = 64
D_STATE = N_GROUPS * STATE_N
HEAD_DIM = 64
N_HEADS = D_MODEL // HEAD_DIM
N_KV_HEADS = 4
Q_PER_KV = N_HEADS // N_KV_HEADS
D_KV = N_KV_HEADS * HEAD_DIM
WINDOW = 128
D_FF = 4 * D_MODEL
EPS = 1e-6
MASKED = -1e30

LANES = 128
GROUPS_PER_BLOCK = LANES // GROUP_SIZE
N_LANE_BLOCKS = D_MODEL // LANES
STATES_PER_BLOCK = GROUPS_PER_BLOCK * STATE_N
SUB_BATCH = 16
SCAN_LANES = 256
SAMPLE_GROUP = 8
MLP_PIECE = 256
ROW_TILE = 512
S5_TIME_CHUNK = 32
VMEM_LIMIT = 56 * 1024 * 1024


def _resident(a):
    return pl.BlockSpec(a.shape, lambda *_: (0,) * a.ndim, pipeline_mode=pl.Buffered(1))


def _resident_layer(a, layer):
    return pl.BlockSpec((None,) + a.shape[1:], lambda *_: (layer,) + (0,) * (a.ndim - 1),
                        pipeline_mode=pl.Buffered(1))


def _rms(x, g):
    return x * lax.rsqrt(jnp.mean(x * x, axis=-1, keepdims=True) + EPS) * g


def _head_mean_sq(x, avg):
    return jnp.dot((x * x).astype(BF16), avg, preferred_element_type=F32)


def _discretize_kernel(a_re_ref, a_im_ref, log_dt_ref, b_re_ref, b_im_ref,
                       ab_re_ref, ab_im_ref, bb_re_ref, bb_im_ref):
    dt = jnp.exp(log_dt_ref[...])
    a_re = a_re_ref[...]
    a_im = a_im_ref[...]
    mag = jnp.exp(dt * a_re)
    ab_re = mag * jnp.cos(dt * a_im)
    ab_im = mag * jnp.sin(dt * a_im)
    den = a_re * a_re + a_im * a_im
    nr = ab_re - 1.0
    ni = ab_im
    f_re = (nr * a_re + ni * a_im) / den
    f_im = (ni * a_re - nr * a_im) / den
    ab_re_ref[...] = ab_re
    ab_im_ref[...] = ab_im
    b_re = b_re_ref[...]
    b_im = b_im_ref[...]
    fr = f_re[:, None, :]
    fi = f_im[:, None, :]
    bb_re_ref[...] = fr * b_re - fi * b_im
    bb_im_ref[...] = fr * b_im + fi * b_re


def _s5_discretize(a_re, a_im, log_dt, b_re, b_im):
    g, n = a_re.shape
    c = b_re.shape[-1]
    out = pl.pallas_call(
        _discretize_kernel,
        out_shape=(jax.ShapeDtypeStruct((g, n), F32), jax.ShapeDtypeStruct((g, n), F32),
                   jax.ShapeDtypeStruct((g, c, n), F32), jax.ShapeDtypeStruct((g, c, n), F32)),
        name="s5_discretize",
    )(a_re, a_im, log_dt.reshape(g, 1), jnp.swapaxes(b_re, 1, 2), jnp.swapaxes(b_im, 1, 2))
    return out


def _block_diag_weights(bb_re, bb_im, c_re, c_im):
    nb, gb = N_LANE_BLOCKS, GROUPS_PER_BLOCK
    own = jnp.asarray(np.arange(LANES)[:, None] // GROUP_SIZE == np.arange(STATES_PER_BLOCK)[None, :] // STATE_N)

    def b_half(bb):
        return jnp.where(own, jnp.tile(bb.astype(BF16).reshape(nb, LANES, STATE_N), (1, 1, gb)), 0)

    def c_half(cc):
        rows = jnp.swapaxes(cc.astype(BF16), 1, 2).reshape(nb, STATES_PER_BLOCK, GROUP_SIZE)
        return jnp.where(own.T, jnp.tile(rows, (1, 1, gb)), 0)

    b_blk = jnp.concatenate([b_half(bb_re), b_half(bb_im)], axis=2)
    c_blk = jnp.concatenate([c_half(c_re), c_half(-c_im)], axis=1)
    return b_blk, c_blk


def _s5_layer_kernel(*refs, state_lanes_batch, n_cast):
    n_in = 10 + n_cast
    (x_hbm, h0r_ref, h0i_ref, g_ref, d_ref, abr_ref, abi_ref, bblk_ref, cblk_ref, wglu_ref) = refs[:10]
    o_hbm, hro_ref, hio_ref = refs[n_in:n_in + 3]
    (xbuf, obuf, in_sem, out_sem, hr_sc, hi_sc, bur, bui, xsr, xsi, hn_sc, ge_sc) = refs[n_in + 3 + n_cast:]
    for src, dst in zip(refs[10:n_in], refs[n_in + 3:n_in + 3 + n_cast]):
        dst[...] = src[...].astype(dst.dtype)

    _, tc, bb, d = xbuf.shape
    rows = tc * bb
    spb = STATES_PER_BLOCK
    step_idx = pl.program_id(0)
    n_chunks = pl.num_programs(0) - 1
    n_xbuf, n_obuf = xbuf.shape[0], obuf.shape[0]

    def in_copy(chunk, b):
        return pltpu.make_async_copy(x_hbm.at[b, pl.ds(chunk * tc, tc), :],
                                     xbuf.at[chunk % n_xbuf, :, b, :], in_sem.at[chunk % n_xbuf])

    def out_copy(chunk, b):
        return pltpu.make_async_copy(obuf.at[chunk % n_obuf, :, b, :],
                                     o_hbm.at[b, pl.ds(chunk * tc, tc), :], out_sem.at[chunk % n_obuf])

    @pl.when(step_idx == 0)
    def _():
        for b in range(bb):
            in_copy(0, b).start()
        ge_sc[1] = jnp.zeros(ge_sc.shape[1:], ge_sc.dtype)
        xbuf[n_xbuf - 1] = jnp.zeros(xbuf.shape[1:], xbuf.dtype)
        if state_lanes_batch:
            hr_sc[...] = h0r_ref[...].T
            hi_sc[...] = h0i_ref[...].T
        else:
            hr_sc[...] = h0r_ref[...]
            hi_sc[...] = h0i_ref[...]

    @pl.when(step_idx + 1 < n_chunks)
    def _():
        for b in range(bb):
            in_copy(step_idx + 1, b).start()

    @pl.when(step_idx >= n_obuf + 1)
    def _():
        for b in range(bb):
            out_copy(step_idx - n_obuf - 1, b).wait()

    prev = step_idx + (n_xbuf * n_obuf - 1)
    ge_prev_slot = prev % 2
    x_prev_slot = prev % n_xbuf
    o_slot = prev % n_obuf
    n_col = d // MLP_PIECE
    gate_first = [n for pair in zip(range(n_col, 2 * n_col), range(n_col)) for n in pair]
    z = {}

    def glu_piece(m):
        n = gate_first[m]
        z[n] = jnp.dot(ge_sc[ge_prev_slot], wglu_ref[:, n * MLP_PIECE:(n + 1) * MLP_PIECE],
                       preferred_element_type=F32)
        if n < n_col:
            cs = slice(n * MLP_PIECE, (n + 1) * MLP_PIECE)
            mix = z.pop(n) * jax.nn.sigmoid(z.pop(n + n_col))
            x_prev = xbuf[x_prev_slot, :, :, cs].reshape(rows, MLP_PIECE)
            obuf[o_slot, :, :, cs] = (x_prev + mix).reshape(tc, bb, MLP_PIECE)

    glu_pieces = [functools.partial(glu_piece, m) for m in range(2 * n_col)]

    def chunk_body():
        x_slot = step_idx % n_xbuf
        ge_slot = step_idx % 2
        for b in range(bb):
            in_copy(step_idx, b).wait()

        hn = _rms(xbuf[x_slot].reshape(rows, d), g_ref[...])
        hn_sc[...] = hn
        hnb = hn.astype(BF16)

        def input_map(j):
            bu = jnp.dot(hnb[:, j * LANES:(j + 1) * LANES], bblk_ref[j], preferred_element_type=F32)
            bur[j] = bu[:, :spb]
            bui[j] = bu[:, spb:]

        def recurrence(j):
            for half in range(spb // SCAN_LANES):
                cs = slice(half * SCAN_LANES, (half + 1) * SCAN_LANES)
                gs = slice(j * spb + half * SCAN_LANES, j * spb + (half + 1) * SCAN_LANES)
                ar = jnp.broadcast_to(abr_ref[:, gs], (SUB_BATCH, SCAN_LANES))
                ai = jnp.broadcast_to(abi_ref[:, gs], (SUB_BATCH, SCAN_LANES))
                for sb in range(bb // SUB_BATCH):
                    seqs = slice(sb * SUB_BATCH, (sb + 1) * SUB_BATCH)
                    h_re = hr_sc[seqs, gs]
                    h_im = hi_sc[seqs, gs]
                    for t in range(tc):
                        rs = slice(t * bb + sb * SUB_BATCH, t * bb + (sb + 1) * SUB_BATCH)
                        n_re = ar * h_re - ai * h_im + bur[j, rs, cs]
                        n_im = ar * h_im + ai * h_re + bui[j, rs, cs]
                        xsr[j, rs, cs] = n_re.astype(BF16)
                        xsi[j, rs, cs] = n_im.astype(BF16)
                        h_re, h_im = n_re, n_im
                    hr_sc[seqs, gs] = h_re
                    hi_sc[seqs, gs] = h_im

        def output_map(j):
            ls = slice(j * LANES, (j + 1) * LANES)
            y = (jnp.dot(xsr[j], cblk_ref[j, :spb, :], preferred_element_type=F32)
                 + jnp.dot(xsi[j], cblk_ref[j, spb:, :], preferred_element_type=F32))
            y = y + d_ref[:, ls] * hn_sc[:, ls]
            ge_sc[ge_slot, :, ls] = jax.nn.gelu(y).astype(BF16)

        pending_glu = list(glu_pieces)
        input_map(0)
        for j in range(N_LANE_BLOCKS):
            if j + 1 < N_LANE_BLOCKS:
                input_map(j + 1)
            recurrence(j)
            if j > 0:
                output_map(j - 1)
            if pending_glu:
                pending_glu.pop(0)()
        output_map(N_LANE_BLOCKS - 1)
        for piece in pending_glu:
            piece()

    @pl.when(step_idx < n_chunks)
    def _():
        chunk_body()

    @pl.when(step_idx == n_chunks)
    def _():
        for piece in glu_pieces:
            piece()

    @pl.when(step_idx >= 1)
    def _():
        for b in range(bb):
            out_copy(step_idx - 1, b).start()

    @pl.when(step_idx == n_chunks)
    def _():
        for back in range(min(n_obuf, 2), 0, -1):
            @pl.when(n_chunks >= back)
            def _(back=back):
                for b in range(bb):
                    out_copy(n_chunks - back, b).wait()
        if state_lanes_batch:
            hro_ref[...] = hr_sc[...].T
            hio_ref[...] = hi_sc[...].T
        else:
            hro_ref[...] = hr_sc[...]
            hio_ref[...] = hi_sc[...]


def _s5_layer(x, h0r, h0i, g_mix, d_skip, ab_re, ab_im, b_blk, c_blk, w_glu, *, tc, state_lanes_batch,
              cast=()):
    bsz, length, d = x.shape
    rows = tc * bsz
    n_chunks = length // tc
    weights = (g_mix, d_skip, ab_re, ab_im, b_blk, c_blk, w_glu)
    state_shape = jax.ShapeDtypeStruct(h0r.shape, F32)
    block_shape = (N_LANE_BLOCKS, rows, STATES_PER_BLOCK)
    cast_specs = [pl.BlockSpec((a.shape[0] // n_chunks, a.shape[1]),
                               lambda i: (jnp.minimum(i, n_chunks - 1), 0)) for a in cast]
    return pl.pallas_call(
        functools.partial(_s5_layer_kernel, state_lanes_batch=state_lanes_batch, n_cast=len(cast)),
        grid=(n_chunks + 1,),
        in_specs=[pl.BlockSpec(memory_space=pl.ANY), _resident(h0r), _resident(h0i)]
        + [_resident(w) for w in weights] + cast_specs,
        out_specs=[pl.BlockSpec(memory_space=pl.ANY), pl.BlockSpec(h0r.shape, lambda i: (0, 0)),
                   pl.BlockSpec(h0i.shape, lambda i: (0, 0))] + cast_specs,
        out_shape=[jax.ShapeDtypeStruct((bsz, length, d), F32), state_shape, state_shape]
        + [jax.ShapeDtypeStruct(a.shape, BF16) for a in cast],
        scratch_shapes=[pltpu.VMEM((3, tc, bsz, d), F32), pltpu.VMEM((2, tc, bsz, d), F32),
                        pltpu.SemaphoreType.DMA((3,)), pltpu.SemaphoreType.DMA((2,)),
                        pltpu.VMEM((bsz, D_STATE), F32), pltpu.VMEM((bsz, D_STATE), F32),
                        pltpu.VMEM(block_shape, F32), pltpu.VMEM(block_shape, F32),
                        pltpu.VMEM(block_shape, BF16), pltpu.VMEM(block_shape, BF16),
                        pltpu.VMEM((rows, d), F32), pltpu.VMEM((2, rows, d), BF16)],
        compiler_params=pltpu.CompilerParams(
            dimension_semantics=("arbitrary",), vmem_limit_bytes=VMEM_LIMIT),
        name="s5_layer",
    )(x, h0r, h0i, *weights, *cast)


def _mlp_schedule(n_chunks, n_col):
    schedule = [("up", 0, n) for n in range(n_col)]
    for c in range(n_chunks):
        if c + 1 < n_chunks:
            schedule += [("up", c + 1, n) for n in range(n_col)]
        schedule += [("down", c, n) for n in range(n_col)]
    return schedule


def _mlp_piece(state, kind, c, n, w_up_ref, w_down_ref):
    if kind == "up":
        hs = slice(c * D_MODEL + n * MLP_PIECE, c * D_MODEL + (n + 1) * MLP_PIECE)
        u = jnp.dot(state["hn"], w_up_ref[:, hs], preferred_element_type=F32)
        state.setdefault(("u", c), []).append(jnp.square(jnp.maximum(u, 0.0)).astype(BF16))
    else:
        if n == 0:
            state[("uc", c)] = jnp.concatenate(state.pop(("u", c)), axis=1)
        cs = slice(n * MLP_PIECE, (n + 1) * MLP_PIECE)
        state["acc"][n] = state["acc"][n] + jnp.dot(
            state[("uc", c)], w_down_ref[c * D_MODEL:(c + 1) * D_MODEL, cs], preferred_element_type=F32)


def _mlp(x, g, w_up_ref, w_down_ref):
    n_col = D_MODEL // MLP_PIECE
    state = {"hn": _rms(x, g).astype(BF16),
             "acc": [x[:, n * MLP_PIECE:(n + 1) * MLP_PIECE] for n in range(n_col)]}
    for piece in _mlp_schedule(w_up_ref.shape[1] // D_MODEL, n_col):
        _mlp_piece(state, *piece, w_up_ref, w_down_ref)
    return jnp.concatenate(state["acc"], axis=1)


def _mlp_proj_kernel(x_ref, gffn_ref, wup_ref, wdown_ref, gkv_ref, wkv_ref, gk_ref, gq1_ref, wq_ref,
                     gq_ref, avg_ref, x2_ref, q_ref, kd_ref, vd_ref, k_ref, v_ref, *scratch, pipelined):
    tm = x_ref.shape[0]
    n_col = D_MODEL // MLP_PIECE
    x = x_ref[...]
    mlp = {"hn": _rms(x, gffn_ref[...]).astype(BF16),
           "acc": [x[:, n * MLP_PIECE:(n + 1) * MLP_PIECE] for n in range(n_col)]}
    schedule = _mlp_schedule(wup_ref.shape[1] // D_MODEL, n_col)

    if pipelined:
        (x2_sc,) = scratch
        slot = pl.program_id(0) % 2

        @pl.when(pl.program_id(0) == 0)
        def _():
            x2_sc[1] = jnp.zeros(x2_sc.shape[1:], x2_sc.dtype)

        prev = x2_sc[1 - slot]
    else:
        for piece in schedule:
            _mlp_piece(mlp, *piece, wup_ref, wdown_ref)
        schedule = []
        prev = jnp.concatenate(mlp["acc"], axis=1)
    avg = avg_ref[...]
    lane = lax.broadcasted_iota(jnp.int32, (tm, LANES), 1)
    proj = {}

    def kv_matmul():
        normed = prev * lax.rsqrt(jnp.mean(prev * prev, axis=-1, keepdims=True) + EPS)
        proj["hq"] = (normed * gq1_ref[...]).astype(BF16)
        proj["kv"] = jnp.dot((normed * gkv_ref[...]).astype(BF16), wkv_ref[...],
                             preferred_element_type=F32)

    def kv_finish():
        k = proj["kv"][:, :D_KV]
        k = k * lax.rsqrt(_head_mean_sq(k, avg) + EPS) * gk_ref[...]
        v = proj["kv"][:, D_KV:]
        k_ref[...] = k
        v_ref[...] = v
        for src, dst_ref in ((k, kd_ref), (v, vd_ref)):
            for blk in range(D_KV // LANES):
                a = src[:, blk * LANES:(blk + 1) * LANES]
                swapped = pltpu.roll(a, HEAD_DIM, 1)
                dst_ref[:, 2 * blk * LANES:(2 * blk + 1) * LANES] = (
                    jnp.where(lane < HEAD_DIM, a, swapped).astype(BF16))
                dst_ref[:, (2 * blk + 1) * LANES:(2 * blk + 2) * LANES] = (
                    jnp.where(lane < HEAD_DIM, swapped, a).astype(BF16))

    def q_matmul(c):
        proj[("q", c)] = jnp.dot(proj["hq"], wq_ref[:, c * D_KV:(c + 1) * D_KV],
                                 preferred_element_type=F32)

    def q_finish(c):
        q = proj.pop(("q", c))
        q = q * lax.rsqrt(_head_mean_sq(q, avg) + EPS) * gq_ref[...]
        q_ref[:, c * D_KV:(c + 1) * D_KV] = (q * (HEAD_DIM ** -0.5)).astype(BF16)

    n_q = D_MODEL // D_KV
    epilogue = [kv_matmul, kv_finish, functools.partial(q_matmul, 0)]
    for c in range(n_q):
        if c + 1 < n_q:
            epilogue.append(functools.partial(q_matmul, c + 1))
        epilogue.append(functools.partial(q_finish, c))

    every = max(len(schedule) // len(epilogue), 1)
    for i, piece in enumerate(schedule):
        _mlp_piece(mlp, *piece, wup_ref, wdown_ref)
        if i % every == every - 1 and epilogue:
            epilogue.pop(0)()
    for stage in epilogue:
        stage()

    x2 = jnp.concatenate(mlp["acc"], axis=1)
    x2_ref[...] = x2
    if pipelined:
        x2_sc[slot] = x2


def _mlp_proj(x2d, tm, g_ffn, w_up, w_down, g_kv, w_kv, g_k, g_q1, w_q, g_q, avg):
    rows = x2d.shape[0]
    n_tiles = rows // tm
    cur = lambda width: pl.BlockSpec((tm, width), lambda s: (jnp.minimum(s, n_tiles - 1), 0))
    pipelined = n_tiles > 1
    prev = lambda width: pl.BlockSpec((tm, width), lambda s: (jnp.maximum(s - pipelined, 0), 0))
    tail = (g_kv, w_kv, g_k, g_q1, w_q, g_q, avg)
    return pl.pallas_call(
        functools.partial(_mlp_proj_kernel, pipelined=pipelined),
        grid=(n_tiles + pipelined,),
        in_specs=[cur(D_MODEL), _resident(g_ffn), _resident_layer(w_up, 0), _resident_layer(w_down, 0)]
        + [_resident(w) for w in tail],
        out_specs=(cur(D_MODEL), prev(D_MODEL), prev(2 * D_KV), prev(2 * D_KV), prev(D_KV), prev(D_KV)),
        out_shape=(jax.ShapeDtypeStruct((rows, D_MODEL), F32),
                   jax.ShapeDtypeStruct((rows, D_MODEL), BF16),
                   jax.ShapeDtypeStruct((rows, 2 * D_KV), BF16),
                   jax.ShapeDtypeStruct((rows, 2 * D_KV), BF16),
                   jax.ShapeDtypeStruct((rows, D_KV), F32),
                   jax.ShapeDtypeStruct((rows, D_KV), F32)),
        scratch_shapes=[pltpu.VMEM((2, tm, D_MODEL), F32)] if pipelined else [],
        compiler_params=pltpu.CompilerParams(
            dimension_semantics=("arbitrary",), vmem_limit_bytes=VMEM_LIMIT),
        name="mlp_proj",
    )(x2d, g_ffn, w_up, w_down, *tail)


def _prompt_attn_kernel(q_ref, kd_ref, vd_ref, kdp_ref, vdp_ref, x_ref, bias_ref, sink_ref, lmask_ref,
                        wo_ref, gffn_ref, wup_ref, wdown_ref, y_ref, o_sc, *, tiles_per_seq, n_tiles):
    tq = q_ref.shape[0]
    step = pl.program_id(0)
    slot = step % 2

    @pl.when(step == 0)
    def _():
        o_sc[1] = jnp.zeros(o_sc.shape[1:], o_sc.dtype)

    tile = jnp.minimum(step, n_tiles - 1)
    first = (tile % tiles_per_seq == 0).astype(jnp.int32)
    m_lo = lmask_ref[0:1, :]
    m_hi = lmask_ref[1:2, :]
    ones_lo = jnp.broadcast_to(m_lo, (2 * WINDOW, LANES))
    ones_hi = jnp.broadcast_to(m_hi, (2 * WINDOW, LANES))
    nt = (((1,), (1,)), ((), ()))
    lane = lax.broadcasted_iota(jnp.int32, (WINDOW, LANES), 1)
    pairs_per_kv = Q_PER_KV // 2
    windows = {}

    def window(r, kh):
        if (r, kh) not in windows:
            ls = slice(kh * LANES, (kh + 1) * LANES)
            if r == 0:
                kwin = jnp.concatenate([kdp_ref[:, ls], kd_ref[0:WINDOW, ls]], axis=0)
                vwin = jnp.concatenate([vdp_ref[:, ls], vd_ref[0:WINDOW, ls]], axis=0)
            else:
                kwin = kd_ref[(r - 1) * WINDOW:(r + 1) * WINDOW, ls]
                vwin = vd_ref[(r - 1) * WINDOW:(r + 1) * WINDOW, ls]
            windows[(r, kh)] = (kwin * m_lo, kwin * m_hi,
                                jnp.concatenate([vwin * m_lo, ones_lo], axis=1),
                                jnp.concatenate([vwin * m_hi, ones_hi], axis=1))
        return windows[(r, kh)]

    def scores(p):
        r, blk = divmod(p, N_LANE_BLOCKS)
        k_lo, k_hi, _, _ = window(r, blk // pairs_per_kv)
        sel = first if r == 0 else 0
        qp = q_ref[r * WINDOW:(r + 1) * WINDOW, blk * LANES:(blk + 1) * LANES]
        s0 = lax.dot_general(qp, k_lo, nt, preferred_element_type=F32) + bias_ref[sel, 2 * blk]
        s1 = lax.dot_general(qp, k_hi, nt, preferred_element_type=F32) + bias_ref[sel, 2 * blk + 1]
        return s0, s1

    def finish(p, s0, s1):
        r, blk = divmod(p, N_LANE_BLOCKS)
        _, _, v_lo, v_hi = window(r, blk // pairs_per_kv)
        h0 = 2 * blk
        m0 = jnp.maximum(jnp.max(s0, axis=-1, keepdims=True), sink_ref[:, h0:h0 + 1])
        m1 = jnp.maximum(jnp.max(s1, axis=-1, keepdims=True), sink_ref[:, h0 + 1:h0 + 2])
        p0 = jnp.exp(s0 - m0).astype(BF16)
        p1 = jnp.exp(s1 - m1).astype(BF16)
        ext = (jnp.dot(p0, v_lo, preferred_element_type=F32)
               + jnp.dot(p1, v_hi, preferred_element_type=F32))
        e0 = jnp.exp(sink_ref[:, h0:h0 + 1] - m0)
        e1 = jnp.exp(sink_ref[:, h0 + 1:h0 + 2] - m1)
        den = ext[:, LANES:] + jnp.where(lane < HEAD_DIM, e0, e1)
        o_sc[slot, r * WINDOW:(r + 1) * WINDOW, blk * LANES:(blk + 1) * LANES] = (
            (ext[:, :LANES] / den).astype(BF16))

    n_col = D_MODEL // MLP_PIECE
    n_chunks = wup_ref.shape[1] // D_MODEL
    o_prev = o_sc[1 - slot]
    state = {}

    def wo_piece(n):
        cs = slice(n * MLP_PIECE, (n + 1) * MLP_PIECE)
        state.setdefault("acc", []).append(
            x_ref[:, cs] + jnp.dot(o_prev, wo_ref[:, cs], preferred_element_type=F32))
        if n == n_col - 1:
            state["hn"] = _rms(jnp.concatenate(state["acc"], axis=1), gffn_ref[...]).astype(BF16)

    schedule = ([functools.partial(wo_piece, n) for n in range(n_col)]
                + [functools.partial(_mlp_piece, state, *p, wup_ref, wdown_ref)
                   for p in _mlp_schedule(n_chunks, n_col)])

    n_pairs = (tq // WINDOW) * N_LANE_BLOCKS
    pending = scores(0)
    for k in range(max(len(schedule), n_pairs)):
        nxt = scores(k + 1) if k + 1 < n_pairs else None
        if k < len(schedule):
            schedule[k]()
        if k < n_pairs:
            finish(k, *pending)
        pending = nxt
    y_ref[...] = jnp.concatenate(state["acc"], axis=1)


def _prompt_attention(q, kd, vd, x2, bias, sinks, lmask, w_o, g_ffn, w_up, w_down, *, bsz, length, tq):
    nt = length // tq
    n_tiles = bsz * nt
    per_win = tq // WINDOW
    att_tile = lambda s: jnp.minimum(s, n_tiles - 1)
    mlp_tile = lambda s: jnp.maximum(s - 1, 0)
    att = lambda width: pl.BlockSpec((tq, width), lambda s: (att_tile(s), 0))
    mlp = pl.BlockSpec((tq, D_MODEL), lambda s: (mlp_tile(s), 0))
    prev = pl.BlockSpec((WINDOW, 2 * D_KV), lambda s: (jnp.maximum(att_tile(s) * per_win - 1, 0), 0))
    return pl.pallas_call(
        functools.partial(_prompt_attn_kernel, tiles_per_seq=nt, n_tiles=n_tiles),
        grid=(n_tiles + 1,),
        in_specs=[att(D_MODEL), att(2 * D_KV), att(2 * D_KV), prev, prev, mlp,
                  _resident(bias), _resident(sinks), _resident(lmask), _resident(w_o), _resident(g_ffn),
                  _resident_layer(w_up, 1), _resident_layer(w_down, 1)],
        out_specs=mlp,
        out_shape=jax.ShapeDtypeStruct((bsz * length, D_MODEL), F32),
        scratch_shapes=[pltpu.VMEM((2, tq, D_MODEL), BF16)],
        compiler_params=pltpu.CompilerParams(
            dimension_semantics=("arbitrary",), vmem_limit_bytes=VMEM_LIMIT),
        name="prompt_attention",
    )(q, kd, vd, kd, vd, x2, bias, sinks, lmask, w_o, g_ffn, w_up, w_down)


def _alibi_slopes():
    return (2.0 ** (-8.0 * np.arange(1, N_HEADS + 1, dtype=np.float32) / N_HEADS)).astype(np.float32)


def _prompt_bias():
    qi = np.arange(WINDOW)[:, None]
    kj = np.arange(2 * WINDOW)[None, :]
    dist = qi - kj + WINDOW
    valid = (dist >= 0) & (dist < WINDOW)
    valid = np.stack([valid, valid & (kj >= WINDOW)], axis=0)[:, None]
    pen = -_alibi_slopes()[:, None, None] * dist.astype(np.float32)[None]
    return np.where(valid, pen[None], np.float32(MASKED)).astype(np.float32)


def _sample_attn_kernel(q_ref, kc_ref, vc_ref, kn_ref, vn_ref, bias_c_ref, bias_n_ref, sink_ref,
                        o_ref, kbuf_ref, vbuf_ref):
    g = kc_ref.shape[0]
    t_new = kn_ref.shape[0] // g
    nt = (((1,), (1,)), ((), ()))
    pad = jnp.zeros((LANES - g * t_new, D_KV), F32)
    kn_t = jnp.concatenate([kn_ref[...], pad], axis=0).T
    vn_t = jnp.concatenate([vn_ref[...], pad], axis=0).T
    lane = lax.broadcasted_iota(jnp.int32, (HEAD_DIM, WINDOW), 1)
    is_new = lane >= WINDOW - t_new

    for kh in range(N_KV_HEADS):
        hs = slice(kh * HEAD_DIM, (kh + 1) * HEAD_DIM)
        q = q_ref[kh]
        k_t = jnp.concatenate([kc_ref[s, kh] for s in range(g)], axis=1)
        v_t = jnp.concatenate([vc_ref[s, kh] for s in range(g)], axis=1)
        s_c = jnp.dot(q, k_t.astype(BF16), preferred_element_type=F32) + bias_c_ref[kh]
        s_n = jnp.dot(q, kn_t[hs].astype(BF16), preferred_element_type=F32) + bias_n_ref[kh]
        sink = sink_ref[kh]
        m = jnp.maximum(jnp.maximum(jnp.max(s_c, axis=-1, keepdims=True),
                                    jnp.max(s_n, axis=-1, keepdims=True)), sink)
        p_c = jnp.exp(s_c - m)
        p_n = jnp.exp(s_n - m)
        den = (jnp.sum(p_c, axis=-1, keepdims=True) + jnp.sum(p_n, axis=-1, keepdims=True)
               + jnp.exp(sink - m))
        o = (lax.dot_general(p_c.astype(BF16), v_t.astype(BF16), nt, preferred_element_type=F32)
             + lax.dot_general(p_n.astype(BF16), vn_t[hs].astype(BF16), nt, preferred_element_type=F32))
        o_ref[kh] = o / den

        for s in range(g):
            shift_new = (WINDOW - t_new - s * t_new) % WINDOW
            k_new = pltpu.roll(kn_t[hs], shift_new, 1)
            v_new = pltpu.roll(vn_t[hs], shift_new, 1)
            kbuf_ref[s, kh] = jnp.where(is_new, k_new, pltpu.roll(kc_ref[s, kh], WINDOW - t_new, 1))
            vbuf_ref[s, kh] = jnp.where(is_new, v_new, pltpu.roll(vc_ref[s, kh], WINDOW - t_new, 1))


def _sample_attention(q_heads, cache_kt, cache_vt, k_new, v_new, bias_c, bias_n, sink_rows, *, t_new):
    g = SAMPLE_GROUP
    bsz = cache_kt.shape[0]
    rows_q = g * Q_PER_KV * t_new
    cache_spec = pl.BlockSpec((g, N_KV_HEADS, HEAD_DIM, WINDOW), lambda i: (i, 0, 0, 0))
    q_spec = pl.BlockSpec((N_KV_HEADS, rows_q, HEAD_DIM), lambda i: (0, i, 0))
    new_spec = pl.BlockSpec((g * t_new, D_KV), lambda i: (i, 0))
    return pl.pallas_call(
        _sample_attn_kernel,
        grid=(bsz // g,),
        in_specs=[q_spec, cache_spec, cache_spec, new_spec, new_spec,
                  _resident(bias_c), _resident(bias_n), _resident(sink_rows)],
        out_specs=(q_spec, cache_spec, cache_spec),
        out_shape=(jax.ShapeDtypeStruct(q_heads.shape, F32),
                   jax.ShapeDtypeStruct(cache_kt.shape, cache_kt.dtype),
                   jax.ShapeDtypeStruct(cache_vt.shape, cache_vt.dtype)),
        compiler_params=pltpu.CompilerParams(
            dimension_semantics=("parallel",), vmem_limit_bytes=VMEM_LIMIT),
        name="sample_attention",
    )(q_heads, cache_kt, cache_vt, k_new, v_new, bias_c, bias_n, sink_rows)


def _sample_bias(t_new):
    g = SAMPLE_GROUP
    slope = _alibi_slopes().reshape(N_KV_HEADS, 1, Q_PER_KV, 1, 1, 1)
    seq_q = np.arange(g).reshape(1, g, 1, 1, 1, 1)
    t = np.arange(t_new).reshape(1, 1, 1, t_new, 1, 1)
    seq_k = np.arange(g).reshape(1, 1, 1, 1, g, 1)
    j = np.arange(WINDOW).reshape(1, 1, 1, 1, 1, WINDOW)
    dist = t + WINDOW - j
    ok = (seq_q == seq_k) & (dist >= 0) & (dist < WINDOW)
    rows = g * Q_PER_KV * t_new
    bias_c = np.where(ok, -slope * dist.astype(np.float32), np.float32(MASKED))
    bias_c = bias_c.reshape(N_KV_HEADS, rows, g * WINDOW).astype(np.float32)
    tn = np.arange(t_new).reshape(1, 1, 1, 1, 1, t_new)
    dist_n = t - tn
    ok_n = (seq_q == seq_k) & (dist_n >= 0) & (dist_n < WINDOW)
    bias_n = np.where(ok_n, -slope * dist_n.astype(np.float32), np.float32(MASKED))
    bias_n = bias_n.reshape(N_KV_HEADS, rows, g * t_new)
    bias_n = np.pad(bias_n, ((0, 0), (0, 0), (0, LANES - g * t_new)), constant_values=MASKED)
    return bias_c, bias_n.astype(np.float32)


def _out_mlp_kernel(o_ref, x_ref, wo_ref, gffn_ref, wup_ref, wdown_ref, y_ref):
    x3 = x_ref[...] + jnp.dot(o_ref[...], wo_ref[...], preferred_element_type=F32)
    y_ref[...] = _mlp(x3, gffn_ref[...], wup_ref, wdown_ref)


def _out_mlp(o, x2, w_o, g_ffn, w_up, w_down, tm):
    rows = x2.shape[0]
    row = pl.BlockSpec((tm, D_MODEL), lambda i: (i, 0))
    return pl.pallas_call(
        _out_mlp_kernel,
        grid=(rows // tm,),
        in_specs=[row, row, _resident(w_o), _resident(g_ffn), _resident_layer(w_up, 1),
                  _resident_layer(w_down, 1)],
        out_specs=row,
        out_shape=jax.ShapeDtypeStruct((rows, D_MODEL), F32),
        compiler_params=pltpu.CompilerParams(
            dimension_semantics=("parallel",), vmem_limit_bytes=VMEM_LIMIT),
        name="out_mlp",
    )(o, x2, w_o, g_ffn, w_up, w_down)


def kernel(x_prompt, x_sample, state_ssm_re, state_ssm_im, cache_k, cache_v, norm_mix, norm_ffn,
           ssm_a_re, ssm_a_im, ssm_log_dt, ssm_b_re, ssm_b_im, ssm_c_re, ssm_c_im, ssm_d, ssm_w_glu,
           kv_norm, w_kv, k_norm, w_q, q_norm, attn_sinks, w_o, w_up, w_down):
    bsz, length, d = x_prompt.shape
    dec_b, dec_t, _ = x_sample.shape

    ab_re, ab_im, bb_re, bb_im = _s5_discretize(ssm_a_re[0], ssm_a_im[0], ssm_log_dt[0],
                                                ssm_b_re[0], ssm_b_im[0])
    b_blk, c_blk = _block_diag_weights(bb_re, bb_im, ssm_c_re[0], ssm_c_im[0])
    ab_re = ab_re.reshape(1, D_STATE)
    ab_im = ab_im.reshape(1, D_STATE)
    row = lambda v: v.reshape(1, -1).astype(F32)
    w_glu = ssm_w_glu[0].astype(BF16)
    g_k = row(jnp.tile(k_norm, D_KV // HEAD_DIM))
    g_q = row(jnp.tile(q_norm[0], D_KV // HEAD_DIM))
    avg = jnp.asarray(np.kron(np.eye(D_KV // HEAD_DIM, dtype=np.float32),
                              np.full((HEAD_DIM, HEAD_DIM), 1.0 / HEAD_DIM, np.float32)), BF16)
    lane = np.arange(LANES)
    lmask = jnp.asarray(np.stack([lane < HEAD_DIM, lane >= HEAD_DIM]), BF16)
    sinks = row(attn_sinks[0])

    def layer0(x, h0r, h0i, tc, state_lanes_batch, cast=()):
        return _s5_layer(x, h0r, h0i, row(norm_mix[0]), row(ssm_d[0]), ab_re, ab_im, b_blk, c_blk,
                         w_glu, tc=tc, state_lanes_batch=state_lanes_batch, cast=cast)

    def mlp_proj(x2d, tm):
        return _mlp_proj(x2d, tm, row(norm_ffn[0]), w_up_b, w_down_b, row(kv_norm), w_kv_b, g_k,
                         row(norm_mix[1]), w_q0, g_q, avg)

    zeros = jnp.zeros((bsz, D_STATE), F32)
    x1, hr_p, hi_p, w_up_b, w_down_b, w_kv_b, w_q0, w_o0 = layer0(
        x_prompt, zeros, zeros, S5_TIME_CHUNK, False,
        cast=(w_up.reshape(-1, w_up.shape[-1]), w_down.reshape(-1, w_down.shape[-1]), w_kv, w_q[0], w_o[0]))
    w_up_b = w_up_b.reshape(w_up.shape)
    w_down_b = w_down_b.reshape(w_down.shape)
    x2, q, kd, vd, k, v = mlp_proj(x1.reshape(bsz * length, d), ROW_TILE)
    y_prompt = _prompt_attention(q, kd, vd, x2, jnp.asarray(_prompt_bias()), sinks, lmask, w_o0,
                                 row(norm_ffn[1]), w_up_b, w_down_b, bsz=bsz, length=length, tq=ROW_TILE)
    y_prompt = y_prompt.reshape(bsz, length, d)
    k_p = k.reshape(bsz, length, D_KV)[:, -WINDOW:].reshape(bsz, WINDOW, N_KV_HEADS, HEAD_DIM)
    v_p = v.reshape(bsz, length, D_KV)[:, -WINDOW:].reshape(bsz, WINDOW, N_KV_HEADS, HEAD_DIM)
    state_shape = (1, bsz, N_GROUPS, STATE_N)
    re_p = hr_p.reshape(state_shape).astype(state_ssm_re.dtype)
    im_p = hi_p.reshape(state_shape).astype(state_ssm_im.dtype)

    h0r_t = jnp.transpose(state_ssm_re[0], (1, 2, 0)).reshape(D_STATE, dec_b).astype(F32)
    h0i_t = jnp.transpose(state_ssm_im[0], (1, 2, 0)).reshape(D_STATE, dec_b).astype(F32)
    x1s, hr_s, hi_s = layer0(x_sample, h0r_t, h0i_t, dec_t, True)
    x2s, qs, _, _, ks, vs = mlp_proj(x1s.reshape(dec_b * dec_t, d), dec_b * dec_t)
    q_heads = qs.reshape(dec_b, dec_t, N_KV_HEADS, Q_PER_KV, HEAD_DIM).transpose(2, 0, 3, 1, 4)
    q_heads = q_heads.reshape(N_KV_HEADS, dec_b * Q_PER_KV * dec_t, HEAD_DIM)
    bias_c, bias_n = _sample_bias(dec_t)
    sink_rows = jnp.broadcast_to(
        attn_sinks[0].astype(F32).reshape(N_KV_HEADS, 1, Q_PER_KV, 1),
        (N_KV_HEADS, SAMPLE_GROUP, Q_PER_KV, dec_t)).reshape(N_KV_HEADS, -1, 1)
    o_heads, k_s_t, v_s_t = _sample_attention(
        q_heads, jnp.transpose(cache_k, (0, 2, 3, 1)), jnp.transpose(cache_v, (0, 2, 3, 1)),
        ks.astype(cache_k.dtype), vs.astype(cache_v.dtype), jnp.asarray(bias_c), jnp.asarray(bias_n),
        sink_rows, t_new=dec_t)
    o_s = o_heads.reshape(N_KV_HEADS, dec_b, Q_PER_KV, dec_t, HEAD_DIM).transpose(1, 3, 0, 2, 4)
    o_s = o_s.reshape(dec_b * dec_t, d).astype(BF16)
    y_sample = _out_mlp(o_s, x2s, w_o0, row(norm_ffn[1]), w_up_b, w_down_b, dec_b * dec_t)
    y_sample = y_sample.reshape(dec_b, dec_t, d)
    k_s = jnp.transpose(k_s_t, (0, 3, 1, 2))
    v_s = jnp.transpose(v_s_t, (0, 3, 1, 2))
    re_s = jnp.transpose(hr_s.reshape(N_GROUPS, STATE_N, dec_b), (2, 0, 1))[None].astype(state_ssm_re.dtype)
    im_s = jnp.transpose(hi_s.reshape(N_GROUPS, STATE_N, dec_b), (2, 0, 1))[None].astype(state_ssm_im.dtype)

    return (y_prompt, y_sample, re_p, im_p, k_p, v_p, re_s, im_s, k_s, v_s)
```

```python
import functools

import jax
import jax.numpy as jnp
import numpy as np
from jax import lax
from jax.experimental import pallas as pl
from jax.experimental.pallas import tpu as pltpu

F32 = jnp.float32
BF16 = jnp.bfloat16

D_MODEL = 1024
GROUP_SIZE = 16
N_GROUPS = D_MODEL // GROUP_SIZE
STATE_N = 64
D_STATE = N_GROUPS * STATE_N
HEAD_DIM = 64
N_HEADS = D_MODEL // HEAD_DIM
N_KV_HEADS = 4
Q_PER_KV = N_HEADS // N_KV_HEADS
D_KV = N_KV_HEADS * HEAD_DIM
WINDOW = 128
D_FF = 4 * D_MODEL
EPS = 1e-6
MASKED = -1e30

LANES = 128
GROUPS_PER_BLOCK = LANES // GROUP_SIZE
N_LANE_BLOCKS = D_MODEL // LANES
STATES_PER_BLOCK = GROUPS_PER_BLOCK * STATE_N
SUB_BATCH = 16
SCAN_LANES = 256
SAMPLE_GROUP = 8
MLP_PIECE = 256
ROW_TILE = 512
S5_TIME_CHUNK = 32
VMEM_LIMIT = 56 * 1024 * 1024


def _resident(a):
    return pl.BlockSpec(a.shape, lambda *_: (0,) * a.ndim, pipeline_mode=pl.Buffered(1))


def _resident_layer(a, layer):
    return pl.BlockSpec((None,) + a.shape[1:], lambda *_: (layer,) + (0,) * (a.ndim - 1),
                        pipeline_mode=pl.Buffered(1))


def _rms(x, g):
    return x * lax.rsqrt(jnp.mean(x * x, axis=-1, keepdims=True) + EPS) * g


def _head_mean_sq(x, avg):
    return jnp.dot((x * x).astype(BF16), avg, preferred_element_type=F32)


def _discretize_kernel(a_re_ref, a_im_ref, log_dt_ref, b_re_ref, b_im_ref,
                       ab_re_ref, ab_im_ref, bb_re_ref, bb_im_ref):
    dt = jnp.exp(log_dt_ref[...])
    a_re = a_re_ref[...]
    a_im = a_im_ref[...]
    mag = jnp.exp(dt * a_re)
    ab_re = mag * jnp.cos(dt * a_im)
    ab_im = mag * jnp.sin(dt * a_im)
    den = a_re * a_re + a_im * a_im
    nr = ab_re - 1.0
    ni = ab_im
    f_re = (nr * a_re + ni * a_im) / den
    f_im = (ni * a_re - nr * a_im) / den
    ab_re_ref[...] = ab_re
    ab_im_ref[...] = ab_im
    b_re = b_re_ref[...]
    b_im = b_im_ref[...]
    fr = f_re[:, None, :]
    fi = f_im[:, None, :]
    bb_re_ref[...] = fr * b_re - fi * b_im
    bb_im_ref[...] = fr * b_im + fi * b_re


def _s5_discretize(a_re, a_im, log_dt, b_re, b_im):
    g, n = a_re.shape
    c = b_re.shape[-1]
    out = pl.pallas_call(
        _discretize_kernel,
        out_shape=(jax.ShapeDtypeStruct((g, n), F32), jax.ShapeDtypeStruct((g, n), F32),
                   jax.ShapeDtypeStruct((g, c, n), F32), jax.ShapeDtypeStruct((g, c, n), F32)),
        name="s5_discretize",
    )(a_re, a_im, log_dt.reshape(g, 1), jnp.swapaxes(b_re, 1, 2), jnp.swapaxes(b_im, 1, 2))
    return out


def _block_diag_weights(bb_re, bb_im, c_re, c_im):
    nb, gb = N_LANE_BLOCKS, GROUPS_PER_BLOCK
    own = jnp.asarray(np.arange(LANES)[:, None] // GROUP_SIZE == np.arange(STATES_PER_BLOCK)[None, :] // STATE_N)

    def b_half(bb):
        return jnp.where(own, jnp.tile(bb.astype(BF16).reshape(nb, LANES, STATE_N), (1, 1, gb)), 0)

    def c_half(cc):
        rows = jnp.swapaxes(cc.astype(BF16), 1, 2).reshape(nb, STATES_PER_BLOCK, GROUP_SIZE)
        return jnp.where(own.T, jnp.tile(rows, (1, 1, gb)), 0)

    b_blk = jnp.concatenate([b_half(bb_re), b_half(bb_im)], axis=2)
    c_blk = jnp.concatenate([c_half(c_re), c_half(-c_im)], axis=1)
    return b_blk, c_blk


def _s5_layer_kernel(*refs, state_lanes_batch, n_cast):
    n_in = 10 + n_cast
    (x_hbm, h0r_ref, h0i_ref, g_ref, d_ref, abr_ref, abi_ref, bblk_ref, cblk_ref, wglu_ref) = refs[:10]
    o_hbm, hro_ref, hio_ref = refs[n_in:n_in + 3]
    (xbuf, obuf, in_sem, out_sem, hr_sc, hi_sc, bur, bui, xsr, xsi, hn_sc, ge_sc) = refs[n_in + 3 + n_cast:]
    for src, dst in zip(refs[10:n_in], refs[n_in + 3:n_in + 3 + n_cast]):
        dst[...] = src[...].astype(dst.dtype)

    _, tc, bb, d = xbuf.shape
    rows = tc * bb
    spb = STATES_PER_BLOCK
    step_idx = pl.program_id(0)
    n_chunks = pl.num_programs(0) - 1
    n_xbuf, n_obuf = xbuf.shape[0], obuf.shape[0]

    def in_copy(chunk, b):
        return pltpu.make_async_copy(x_hbm.at[b, pl.ds(chunk * tc, tc), :],
                                     xbuf.at[chunk % n_xbuf, :, b, :], in_sem.at[chunk % n_xbuf])

    def out_copy(chunk, b):
        return pltpu.make_async_copy(obuf.at[chunk % n_obuf, :, b, :],
                                     o_hbm.at[b, pl.ds(chunk * tc, tc), :], out_sem.at[chunk % n_obuf])

    @pl.when(step_idx == 0)
    def _():
        for b in range(bb):
            in_copy(0, b).start()
        ge_sc[1] = jnp.zeros(ge_sc.shape[1:], ge_sc.dtype)
        xbuf[n_xbuf - 1] = jnp.zeros(xbuf.shape[1:], xbuf.dtype)
        if state_lanes_batch:
            hr_sc[...] = h0r_ref[...].T
            hi_sc[...] = h0i_ref[...].T
        else:
            hr_sc[...] = h0r_ref[...]
            hi_sc[...] = h0i_ref[...]

    @pl.when(step_idx + 1 < n_chunks)
    def _():
        for b in range(bb):
            in_copy(step_idx + 1, b).start()

    @pl.when(step_idx >= n_obuf + 1)
    def _():
        for b in range(bb):
            out_copy(step_idx - n_obuf - 1, b).wait()

    prev = step_idx + (n_xbuf * n_obuf - 1)
    ge_prev_slot = prev % 2
    x_prev_slot = prev % n_xbuf
    o_slot = prev % n_obuf
    n_col = d // MLP_PIECE
    gate_first = [n for pair in zip(range(n_col, 2 * n_col), range(n_col)) for n in pair]
    z = {}

    def glu_piece(m):
        n = gate_first[m]
        z[n] = jnp.dot(ge_sc[ge_prev_slot], wglu_ref[:, n * MLP_PIECE:(n + 1) * MLP_PIECE],
                       preferred_element_type=F32)
        if n < n_col:
            cs = slice(n * MLP_PIECE, (n + 1) * MLP_PIECE)
            mix = z.pop(n) * jax.nn.sigmoid(z.pop(n + n_col))
            x_prev = xbuf[x_prev_slot, :, :, cs].reshape(rows, MLP_PIECE)
            obuf[o_slot, :, :, cs] = (x_prev + mix).reshape(tc, bb, MLP_PIECE)

    glu_pieces = [functools.partial(glu_piece, m) for m in range(2 * n_col)]

    def chunk_body():
        x_slot = step_idx % n_xbuf
        ge_slot = step_idx % 2
        for b in range(bb):
            in_copy(step_idx, b).wait()

        hn = _rms(xbuf[x_slot].reshape(rows, d), g_ref[...])
        hn_sc[...] = hn
        hnb = hn.astype(BF16)

        def input_map(j):
            bu = jnp.dot(hnb[:, j * LANES:(j + 1) * LANES], bblk_ref[j], preferred_element_type=F32)
            bur[j] = bu[:, :spb]
            bui[j] = bu[:, spb:]

        def recurrence(j):
            for half in range(spb // SCAN_LANES):
                cs = slice(half * SCAN_LANES, (half + 1) * SCAN_LANES)
                gs = slice(j * spb + half * SCAN_LANES, j * spb + (half + 1) * SCAN_LANES)
                ar = jnp.broadcast_to(abr_ref[:, gs], (SUB_BATCH, SCAN_LANES))
                ai = jnp.broadcast_to(abi_ref[:, gs], (SUB_BATCH, SCAN_LANES))
                for sb in range(bb // SUB_BATCH):
                    seqs = slice(sb * SUB_BATCH, (sb + 1) * SUB_BATCH)
                    h_re = hr_sc[seqs, gs]
                    h_im = hi_sc[seqs, gs]
                    for t in range(tc):
                        rs = slice(t * bb + sb * SUB_BATCH, t * bb + (sb + 1) * SUB_BATCH)
                        n_re = ar * h_re - ai * h_im + bur[j, rs, cs]
                        n_im = ar * h_im + ai * h_re + bui[j, rs, cs]
                        xsr[j, rs, cs] = n_re.astype(BF16)
                        xsi[j, rs, cs] = n_im.astype(BF16)
                        h_re, h_im = n_re, n_im
                    hr_sc[seqs, gs] = h_re
                    hi_sc[seqs, gs] = h_im

        def output_map(j):
            ls = slice(j * LANES, (j + 1) * LANES)
            y = (jnp.dot(xsr[j], cblk_ref[j, :spb, :], preferred_element_type=F32)
                 + jnp.dot(xsi[j], cblk_ref[j, spb:, :], preferred_element_type=F32))
            y = y + d_ref[:, ls] * hn_sc[:, ls]
            ge_sc[ge_slot, :, ls] = jax.nn.gelu(y).astype(BF16)

        pending_glu = list(glu_pieces)
        input_map(0)
        for j in range(N_LANE_BLOCKS):
            if j + 1 < N_LANE_BLOCKS:
                input_map(j + 1)
            recurrence(j)
            if j > 0:
                output_map(j - 1)
            if pending_glu:
                pending_glu.pop(0)()
        output_map(N_LANE_BLOCKS - 1)
        for piece in pending_glu:
            piece()

    @pl.when(step_idx < n_chunks)
    def _():
        chunk_body()

    @pl.when(step_idx == n_chunks)
    def _():
        for piece in glu_pieces:
            piece()

    @pl.when(step_idx >= 1)
    def _():
        for b in range(bb):
            out_copy(step_idx - 1, b).start()

    @pl.when(step_idx == n_chunks)
    def _():
        for back in range(min(n_obuf, 2), 0, -1):
            @pl.when(n_chunks >= back)
            def _(back=back):
                for b in range(bb):
                    out_copy(n_chunks - back, b).wait()
        if state_lanes_batch:
            hro_ref[...] = hr_sc[...].T
            hio_ref[...] = hi_sc[...].T
        else:
            hro_ref[...] = hr_sc[...]
            hio_ref[...] = hi_sc[...]


def _s5_layer(x, h0r, h0i, g_mix, d_skip, ab_re, ab_im, b_blk, c_blk, w_glu, *, tc, state_lanes_batch,
              cast=()):
    bsz, length, d = x.shape
    rows = tc * bsz
    n_chunks = length // tc
    weights = (g_mix, d_skip, ab_re, ab_im, b_blk, c_blk, w_glu)
    state_shape = jax.ShapeDtypeStruct(h0r.shape, F32)
    block_shape = (N_LANE_BLOCKS, rows, STATES_PER_BLOCK)
    cast_specs = [pl.BlockSpec((a.shape[0] // n_chunks, a.shape[1]),
                               lambda i: (jnp.minimum(i, n_chunks - 1), 0)) for a in cast]
    return pl.pallas_call(
        functools.partial(_s5_layer_kernel, state_lanes_batch=state_lanes_batch, n_cast=len(cast)),
        grid=(n_chunks + 1,),
        in_specs=[pl.BlockSpec(memory_space=pl.ANY), _resident(h0r), _resident(h0i)]
        + [_resident(w) for w in weights] + cast_specs,
        out_specs=[pl.BlockSpec(memory_space=pl.ANY), pl.BlockSpec(h0r.shape, lambda i: (0, 0)),
                   pl.BlockSpec(h0i.shape, lambda i: (0, 0))] + cast_specs,
        out_shape=[jax.ShapeDtypeStruct((bsz, length, d), F32), state_shape, state_shape]
        + [jax.ShapeDtypeStruct(a.shape, BF16) for a in cast],
        scratch_shapes=[pltpu.VMEM((3, tc, bsz, d), F32), pltpu.VMEM((2, tc, bsz, d), F32),
                        pltpu.SemaphoreType.DMA((3,)), pltpu.SemaphoreType.DMA((2,)),
                        pltpu.VMEM((bsz, D_STATE), F32), pltpu.VMEM((bsz, D_STATE), F32),
                        pltpu.VMEM(block_shape, F32), pltpu.VMEM(block_shape, F32),
                        pltpu.VMEM(block_shape, BF16), pltpu.VMEM(block_shape, BF16),
                        pltpu.VMEM((rows, d), F32), pltpu.VMEM((2, rows, d), BF16)],
        compiler_params=pltpu.CompilerParams(
            dimension_semantics=("arbitrary",), vmem_limit_bytes=VMEM_LIMIT),
        name="s5_layer",
    )(x, h0r, h0i, *weights, *cast)


def _mlp_schedule(n_chunks, n_col):
    schedule = [("up", 0, n) for n in range(n_col)]
    for c in range(n_chunks):
        if c + 1 < n_chunks:
            schedule += [("up", c + 1, n) for n in range(n_col)]
        schedule += [("down", c, n) for n in range(n_col)]
    return schedule


def _mlp_piece(state, kind, c, n, w_up_ref, w_down_ref):
    if kind == "up":
        hs = slice(c * D_MODEL + n * MLP_PIECE, c * D_MODEL + (n + 1) * MLP_PIECE)
        u = jnp.dot(state["hn"], w_up_ref[:, hs], preferred_element_type=F32)
        state.setdefault(("u", c), []).append(jnp.square(jnp.maximum(u, 0.0)).astype(BF16))
    else:
        if n == 0:
            state[("uc", c)] = jnp.concatenate(state.pop(("u", c)), axis=1)
        cs = slice(n * MLP_PIECE, (n + 1) * MLP_PIECE)
        state["acc"][n] = state["acc"][n] + jnp.dot(
            state[("uc", c)], w_down_ref[c * D_MODEL:(c + 1) * D_MODEL, cs], preferred_element_type=F32)


def _mlp(x, g, w_up_ref, w_down_ref):
    n_col = D_MODEL // MLP_PIECE
    state = {"hn": _rms(x, g).astype(BF16),
             "acc": [x[:, n * MLP_PIECE:(n + 1) * MLP_PIECE] for n in range(n_col)]}
    for piece in _mlp_schedule(w_up_ref.shape[1] // D_MODEL, n_col):
        _mlp_piece(state, *piece, w_up_ref, w_down_ref)
    return jnp.concatenate(state["acc"], axis=1)


def _mlp_proj_kernel(x_ref, gffn_ref, wup_ref, wdown_ref, gkv_ref, wkv_ref, gk_ref, gq1_ref, wq_ref,
                     gq_ref, avg_ref, x2_ref, q_ref, kd_ref, vd_ref, k_ref, v_ref, *scratch, pipelined):
    tm = x_ref.shape[0]
    if pipelined:
        (x2_sc,) = scratch
        slot = pl.program_id(0) % 2

        @pl.when(pl.program_id(0) == 0)
        def _():
            x2_sc[1] = jnp.zeros(x2_sc.shape[1:], x2_sc.dtype)

    n_col = D_MODEL // MLP_PIECE
    x = x_ref[...]
    mlp = {"hn": _rms(x, gffn_ref[...]).astype(BF16),
           "acc": [x[:, n * MLP_PIECE:(n + 1) * MLP_PIECE] for n in range(n_col)]}
    schedule = _mlp_schedule(wup_ref.shape[1] // D_MODEL, n_col)

    if pipelined:
        prev = x2_sc[1 - slot]
    else:
        for piece in schedule:
            _mlp_piece(mlp, *piece, wup_ref, wdown_ref)
        schedule = []
        prev = jnp.concatenate(mlp["acc"], axis=1)
    avg = avg_ref[...]
    lane = lax.broadcasted_iota(jnp.int32, (tm, LANES), 1)
    proj = {}

    def kv_matmul():
        normed = prev * lax.rsqrt(jnp.mean(prev * prev, axis=-1, keepdims=True) + EPS)
        proj["hq"] = (normed * gq1_ref[...]).astype(BF16)
        proj["kv"] = jnp.dot((normed * gkv_ref[...]).astype(BF16), wkv_ref[...],
                             preferred_element_type=F32)

    def kv_finish():
        k = proj["kv"][:, :D_KV]
        k = k * lax.rsqrt(_head_mean_sq(k, avg) + EPS) * gk_ref[...]
        v = proj["kv"][:, D_KV:]
        k_ref[...] = k
        v_ref[...] = v
        for src, dst_ref in ((k, kd_ref), (v, vd_ref)):
            for blk in range(D_KV // LANES):
                a = src[:, blk * LANES:(blk + 1) * LANES]
                swapped = pltpu.roll(a, HEAD_DIM, 1)
                dst_ref[:, 2 * blk * LANES:(2 * blk + 1) * LANES] = (
                    jnp.where(lane < HEAD_DIM, a, swapped).astype(BF16))
                dst_ref[:, (2 * blk + 1) * LANES:(2 * blk + 2) * LANES] = (
                    jnp.where(lane < HEAD_DIM, swapped, a).astype(BF16))

    def q_matmul(c):
        proj[("q", c)] = jnp.dot(proj["hq"], wq_ref[:, c * D_KV:(c + 1) * D_KV],
                                 preferred_element_type=F32)

    def q_finish(c):
        q = proj.pop(("q", c))
        q = q * lax.rsqrt(_head_mean_sq(q, avg) + EPS) * gq_ref[...]
        q_ref[:, c * D_KV:(c + 1) * D_KV] = (q * (HEAD_DIM ** -0.5)).astype(BF16)

    n_q = D_MODEL // D_KV
    epilogue = [kv_matmul, kv_finish, functools.partial(q_matmul, 0)]
    for c in range(n_q):
        if c + 1 < n_q:
            epilogue.append(functools.partial(q_matmul, c + 1))
        epilogue.append(functools.partial(q_finish, c))

    every = max(len(schedule) // len(epilogue), 1)
    for i, piece in enumerate(schedule):
        _mlp_piece(mlp, *piece, wup_ref, wdown_ref)
        if i % every == every - 1 and epilogue:
            epilogue.pop(0)()
    for stage in epilogue:
        stage()

    x2 = jnp.concatenate(mlp["acc"], axis=1)
    x2_ref[...] = x2
    if pipelined:
        x2_sc[slot] = x2


def _mlp_proj(x2d, tm, g_ffn, w_up, w_down, g_kv, w_kv, g_k, g_q1, w_q, g_q, avg):
    rows = x2d.shape[0]
    n_tiles = rows // tm
    cur = lambda width: pl.BlockSpec((tm, width), lambda s: (jnp.minimum(s, n_tiles - 1), 0))
    pipelined = n_tiles > 1
    prev = lambda width: pl.BlockSpec((tm, width), lambda s: (jnp.maximum(s - pipelined, 0), 0))
    tail = (g_kv, w_kv, g_k, g_q1, w_q, g_q, avg)
    return pl.pallas_call(
        functools.partial(_mlp_proj_kernel, pipelined=pipelined),
        grid=(n_tiles + pipelined,),
        in_specs=[cur(D_MODEL), _resident(g_ffn), _resident_layer(w_up, 0), _resident_layer(w_down, 0)]
        + [_resident(w) for w in tail],
        out_specs=(cur(D_MODEL), prev(D_MODEL), prev(2 * D_KV), prev(2 * D_KV), prev(D_KV), prev(D_KV)),
        out_shape=(jax.ShapeDtypeStruct((rows, D_MODEL), F32),
                   jax.ShapeDtypeStruct((rows, D_MODEL), BF16),
                   jax.ShapeDtypeStruct((rows, 2 * D_KV), BF16),
                   jax.ShapeDtypeStruct((rows, 2 * D_KV), BF16),
                   jax.ShapeDtypeStruct((rows, D_KV), F32),
                   jax.ShapeDtypeStruct((rows, D_KV), F32)),
        scratch_shapes=[pltpu.VMEM((2, tm, D_MODEL), F32)] if pipelined else [],
        compiler_params=pltpu.CompilerParams(
            dimension_semantics=("arbitrary",), vmem_limit_bytes=VMEM_LIMIT),
        name="mlp_proj",
    )(x2d, g_ffn, w_up, w_down, *tail)


def _prompt_attn_kernel(q_ref, kd_ref, vd_ref, kdp_ref, vdp_ref, x_ref, bias_ref, sink_ref, lmask_ref,
                        wo_ref, gffn_ref, wup_ref, wdown_ref, y_ref, o_sc, *, tiles_per_seq, n_tiles):
    tq = q_ref.shape[0]
    step = pl.program_id(0)
    slot = step % 2

    @pl.when(step == 0)
    def _():
        o_sc[1] = jnp.zeros(o_sc.shape[1:], o_sc.dtype)

    tile = jnp.minimum(step, n_tiles - 1)
    first = (tile % tiles_per_seq == 0).astype(jnp.int32)
    m_lo = lmask_ref[0:1, :]
    m_hi = lmask_ref[1:2, :]
    ones_lo = jnp.broadcast_to(m_lo, (2 * WINDOW, LANES))
    ones_hi = jnp.broadcast_to(m_hi, (2 * WINDOW, LANES))
    nt = (((1,), (1,)), ((), ()))
    lane = lax.broadcasted_iota(jnp.int32, (WINDOW, LANES), 1)
    pairs_per_kv = Q_PER_KV // 2
    windows = {}

    def window(r, kh):
        if (r, kh) not in windows:
            ls = slice(kh * LANES, (kh + 1) * LANES)
            if r == 0:
                kwin = jnp.concatenate([kdp_ref[:, ls], kd_ref[0:WINDOW, ls]], axis=0)
                vwin = jnp.concatenate([vdp_ref[:, ls], vd_ref[0:WINDOW, ls]], axis=0)
            else:
                kwin = kd_ref[(r - 1) * WINDOW:(r + 1) * WINDOW, ls]
                vwin = vd_ref[(r - 1) * WINDOW:(r + 1) * WINDOW, ls]
            windows[(r, kh)] = (kwin * m_lo, kwin * m_hi,
                                jnp.concatenate([vwin * m_lo, ones_lo], axis=1),
                                jnp.concatenate([vwin * m_hi, ones_hi], axis=1))
        return windows[(r, kh)]

    def scores(p):
        r, blk = divmod(p, N_LANE_BLOCKS)
        k_lo, k_hi, _, _ = window(r, blk // pairs_per_kv)
        sel = first if r == 0 else 0
        qp = q_ref[r * WINDOW:(r + 1) * WINDOW, blk * LANES:(blk + 1) * LANES]
        s0 = lax.dot_general(qp, k_lo, nt, preferred_element_type=F32) + bias_ref[sel, 2 * blk]
        s1 = lax.dot_general(qp, k_hi, nt, preferred_element_type=F32) + bias_ref[sel, 2 * blk + 1]
        return s0, s1

    def finish(p, s0, s1):
        r, blk = divmod(p, N_LANE_BLOCKS)
        _, _, v_lo, v_hi = window(r, blk // pairs_per_kv)
        h0 = 2 * blk
        m0 = jnp.maximum(jnp.max(s0, axis=-1, keepdims=True), sink_ref[:, h0:h0 + 1])
        m1 = jnp.maximum(jnp.max(s1, axis=-1, keepdims=True), sink_ref[:, h0 + 1:h0 + 2])
        p0 = jnp.exp(s0 - m0).astype(BF16)
        p1 = jnp.exp(s1 - m1).astype(BF16)
        ext = (jnp.dot(p0, v_lo, preferred_element_type=F32)
               + jnp.dot(p1, v_hi, preferred_element_type=F32))
        e0 = jnp.exp(sink_ref[:, h0:h0 + 1] - m0)
        e1 = jnp.exp(sink_ref[:, h0 + 1:h0 + 2] - m1)
        den = ext[:, LANES:] + jnp.where(lane < HEAD_DIM, e0, e1)
        o_sc[slot, r * WINDOW:(r + 1) * WINDOW, blk * LANES:(blk + 1) * LANES] = (
            (ext[:, :LANES] / den).astype(BF16))

    n_col = D_MODEL // MLP_PIECE
    n_chunks = wup_ref.shape[1] // D_MODEL
    o_prev = o_sc[1 - slot]
    state = {}

    def wo_piece(n):
        cs = slice(n * MLP_PIECE, (n + 1) * MLP_PIECE)
        state.setdefault("acc", []).append(
            x_ref[:, cs] + jnp.dot(o_prev, wo_ref[:, cs], preferred_element_type=F32))
        if n == n_col - 1:
            state["hn"] = _rms(jnp.concatenate(state["acc"], axis=1), gffn_ref[...]).astype(BF16)

    schedule = ([functools.partial(wo_piece, n) for n in range(n_col)]
                + [functools.partial(_mlp_piece, state, *p, wup_ref, wdown_ref)
                   for p in _mlp_schedule(n_chunks, n_col)])

    n_pairs = (tq // WINDOW) * N_LANE_BLOCKS
    pending = scores(0)
    for k in range(max(len(schedule), n_pairs)):
        nxt = scores(k + 1) if k + 1 < n_pairs else None
        if k < len(schedule):
            schedule[k]()
        if k < n_pairs:
            finish(k, *pending)
        pending = nxt
    y_ref[...] = jnp.concatenate(state["acc"], axis=1)


def _prompt_attention(q, kd, vd, x2, bias, sinks, lmask, w_o, g_ffn, w_up, w_down, *, bsz, length, tq):
    nt = length // tq
    n_tiles = bsz * nt
    per_win = tq // WINDOW
    att_tile = lambda s: jnp.minimum(s, n_tiles - 1)
    mlp_tile = lambda s: jnp.maximum(s - 1, 0)
    att = lambda width: pl.BlockSpec((tq, width), lambda s: (att_tile(s), 0))
    mlp = pl.BlockSpec((tq, D_MODEL), lambda s: (mlp_tile(s), 0))
    prev = pl.BlockSpec((WINDOW, 2 * D_KV), lambda s: (jnp.maximum(att_tile(s) * per_win - 1, 0), 0))
    return pl.pallas_call(
        functools.partial(_prompt_attn_kernel, tiles_per_seq=nt, n_tiles=n_tiles),
        grid=(n_tiles + 1,),
        in_specs=[att(D_MODEL), att(2 * D_KV), att(2 * D_KV), prev, prev, mlp,
                  _resident(bias), _resident(sinks), _resident(lmask), _resident(w_o), _resident(g_ffn),
                  _resident_layer(w_up, 1), _resident_layer(w_down, 1)],
        out_specs=mlp,
        out_shape=jax.ShapeDtypeStruct((bsz * length, D_MODEL), F32),
        scratch_shapes=[pltpu.VMEM((2, tq, D_MODEL), BF16)],
        compiler_params=pltpu.CompilerParams(
            dimension_semantics=("arbitrary",), vmem_limit_bytes=VMEM_LIMIT),
        name="prompt_attention",
    )(q, kd, vd, kd, vd, x2, bias, sinks, lmask, w_o, g_ffn, w_up, w_down)


def _alibi_slopes():
    return (2.0 ** (-8.0 * np.arange(1, N_HEADS + 1, dtype=np.float32) / N_HEADS)).astype(np.float32)


def _prompt_bias():
    qi = np.arange(WINDOW)[:, None]
    kj = np.arange(2 * WINDOW)[None, :]
    dist = qi - kj + WINDOW
    valid = (dist >= 0) & (dist < WINDOW)
    valid = np.stack([valid, valid & (kj >= WINDOW)], axis=0)[:, None]
    pen = -_alibi_slopes()[:, None, None] * dist.astype(np.float32)[None]
    return np.where(valid, pen[None], np.float32(MASKED)).astype(np.float32)


def _sample_attn_kernel(q_ref, kc_ref, vc_ref, kn_ref, vn_ref, bias_c_ref, bias_n_ref, sink_ref,
                        o_ref, kbuf_ref, vbuf_ref):
    g = kc_ref.shape[0]
    t_new = kn_ref.shape[0] // g
    nt = (((1,), (1,)), ((), ()))
    pad = jnp.zeros((LANES - g * t_new, D_KV), F32)
    kn_t = jnp.concatenate([kn_ref[...], pad], axis=0).T
    vn_t = jnp.concatenate([vn_ref[...], pad], axis=0).T
    lane = lax.broadcasted_iota(jnp.int32, (HEAD_DIM, WINDOW), 1)
    is_new = lane >= WINDOW - t_new

    for kh in range(N_KV_HEADS):
        hs = slice(kh * HEAD_DIM, (kh + 1) * HEAD_DIM)
        q = q_ref[kh]
        k_t = jnp.concatenate([kc_ref[s, kh] for s in range(g)], axis=1)
        v_t = jnp.concatenate([vc_ref[s, kh] for s in range(g)], axis=1)
        s_c = jnp.dot(q, k_t.astype(BF16), preferred_element_type=F32) + bias_c_ref[kh]
        s_n = jnp.dot(q, kn_t[hs].astype(BF16), preferred_element_type=F32) + bias_n_ref[kh]
        sink = sink_ref[kh]
        m = jnp.maximum(jnp.maximum(jnp.max(s_c, axis=-1, keepdims=True),
                                    jnp.max(s_n, axis=-1, keepdims=True)), sink)
        p_c = jnp.exp(s_c - m)
        p_n = jnp.exp(s_n - m)
        den = (jnp.sum(p_c, axis=-1, keepdims=True) + jnp.sum(p_n, axis=-1, keepdims=True)
               + jnp.exp(sink - m))
        o = (lax.dot_general(p_c.astype(BF16), v_t.astype(BF16), nt, preferred_element_type=F32)
             + lax.dot_general(p_n.astype(BF16), vn_t[hs].astype(BF16), nt, preferred_element_type=F32))
        o_ref[kh] = o / den

        for s in range(g):
            shift_new = (WINDOW - t_new - s * t_new) % WINDOW
            k_new = pltpu.roll(kn_t[hs], shift_new, 1)
            v_new = pltpu.roll(vn_t[hs], shift_new, 1)
            kbuf_ref[s, kh] = jnp.where(is_new, k_new, pltpu.roll(kc_ref[s, kh], WINDOW - t_new, 1))
            vbuf_ref[s, kh] = jnp.where(is_new, v_new, pltpu.roll(vc_ref[s, kh], WINDOW - t_new, 1))


def _sample_attention(q_heads, cache_kt, cache_vt, k_new, v_new, bias_c, bias_n, sink_rows, *, t_new):
    g = SAMPLE_GROUP
    bsz = cache_kt.shape[0]
    rows_q = g * Q_PER_KV * t_new
    cache_spec = pl.BlockSpec((g, N_KV_HEADS, HEAD_DIM, WINDOW), lambda i: (i, 0, 0, 0))
    q_spec = pl.BlockSpec((N_KV_HEADS, rows_q, HEAD_DIM), lambda i: (0, i, 0))
    new_spec = pl.BlockSpec((g * t_new, D_KV), lambda i: (i, 0))
    return pl.pallas_call(
        _sample_attn_kernel,
        grid=(bsz // g,),
        in_specs=[q_spec, cache_spec, cache_spec, new_spec, new_spec,
                  _resident(bias_c), _resident(bias_n), _resident(sink_rows)],
        out_specs=(q_spec, cache_spec, cache_spec),
        out_shape=(jax.ShapeDtypeStruct(q_heads.shape, F32),
                   jax.ShapeDtypeStruct(cache_kt.shape, cache_kt.dtype),
                   jax.ShapeDtypeStruct(cache_vt.shape, cache_vt.dtype)),
        compiler_params=pltpu.CompilerParams(
            dimension_semantics=("parallel",), vmem_limit_bytes=VMEM_LIMIT),
        name="sample_attention",
    )(q_heads, cache_kt, cache_vt, k_new, v_new, bias_c, bias_n, sink_rows)


def _sample_bias(t_new):
    g = SAMPLE_GROUP
    slope = _alibi_slopes().reshape(N_KV_HEADS, 1, Q_PER_KV, 1, 1, 1)
    seq_q = np.arange(g).reshape(1, g, 1, 1, 1, 1)
    t = np.arange(t_new).reshape(1, 1, 1, t_new, 1, 1)
    seq_k = np.arange(g).reshape(1, 1, 1, 1, g, 1)
    j = np.arange(WINDOW).reshape(1, 1, 1, 1, 1, WINDOW)
    dist = t + WINDOW - j
    ok = (seq_q == seq_k) & (dist >= 0) & (dist < WINDOW)
    rows = g * Q_PER_KV * t_new
    bias_c = np.where(ok, -slope * dist.astype(np.float32), np.float32(MASKED))
    bias_c = bias_c.reshape(N_KV_HEADS, rows, g * WINDOW).astype(np.float32)
    tn = np.arange(t_new).reshape(1, 1, 1, 1, 1, t_new)
    dist_n = t - tn
    ok_n = (seq_q == seq_k) & (dist_n >= 0) & (dist_n < WINDOW)
    bias_n = np.where(ok_n, -slope * dist_n.astype(np.float32), np.float32(MASKED))
    bias_n = bias_n.reshape(N_KV_HEADS, rows, g * t_new)
    bias_n = np.pad(bias_n, ((0, 0), (0, 0), (0, LANES - g * t_new)), constant_values=MASKED)
    return bias_c, bias_n.astype(np.float32)


def _out_mlp_kernel(o_ref, x_ref, wo_ref, gffn_ref, wup_ref, wdown_ref, y_ref):
    x3 = x_ref[...] + jnp.dot(o_ref[...], wo_ref[...], preferred_element_type=F32)
    y_ref[...] = _mlp(x3, gffn_ref[...], wup_ref, wdown_ref)


def _out_mlp(o, x2, w_o, g_ffn, w_up, w_down, tm):
    rows = x2.shape[0]
    row = pl.BlockSpec((tm, D_MODEL), lambda i: (i, 0))
    return pl.pallas_call(
        _out_mlp_kernel,
        grid=(rows // tm,),
        in_specs=[row, row, _resident(w_o), _resident(g_ffn), _resident_layer(w_up, 1),
                  _resident_layer(w_down, 1)],
        out_specs=row,
        out_shape=jax.ShapeDtypeStruct((rows, D_MODEL), F32),
        compiler_params=pltpu.CompilerParams(
            dimension_semantics=("parallel",), vmem_limit_bytes=VMEM_LIMIT),
        name="out_mlp",
    )(o, x2, w_o, g_ffn, w_up, w_down)


def kernel(x_prompt, x_sample, state_ssm_re, state_ssm_im, cache_k, cache_v, norm_mix, norm_ffn,
           ssm_a_re, ssm_a_im, ssm_log_dt, ssm_b_re, ssm_b_im, ssm_c_re, ssm_c_im, ssm_d, ssm_w_glu,
           kv_norm, w_kv, k_norm, w_q, q_norm, attn_sinks, w_o, w_up, w_down):
    bsz, length, d = x_prompt.shape
    dec_b, dec_t, _ = x_sample.shape

    ab_re, ab_im, bb_re, bb_im = _s5_discretize(ssm_a_re[0], ssm_a_im[0], ssm_log_dt[0],
                                                ssm_b_re[0], ssm_b_im[0])
    b_blk, c_blk = _block_diag_weights(bb_re, bb_im, ssm_c_re[0], ssm_c_im[0])
    ab_re = ab_re.reshape(1, D_STATE)
    ab_im = ab_im.reshape(1, D_STATE)
    row = lambda v: v.reshape(1, -1).astype(F32)
    w_glu = ssm_w_glu[0].astype(BF16)
    g_k = row(jnp.tile(k_norm, D_KV // HEAD_DIM))
    g_q = row(jnp.tile(q_norm[0], D_KV // HEAD_DIM))
    avg = jnp.asarray(np.kron(np.eye(D_KV // HEAD_DIM, dtype=np.float32),
                              np.full((HEAD_DIM, HEAD_DIM), 1.0 / HEAD_DIM, np.float32)), BF16)
    lane = np.arange(LANES)
    lmask = jnp.asarray(np.stack([lane < HEAD_DIM, lane >= HEAD_DIM]), BF16)
    sinks = row(attn_sinks[0])

    def layer0(x, h0r, h0i, tc, state_lanes_batch, cast=()):
        return _s5_layer(x, h0r, h0i, row(norm_mix[0]), row(ssm_d[0]), ab_re, ab_im, b_blk, c_blk,
                         w_glu, tc=tc, state_lanes_batch=state_lanes_batch, cast=cast)

    def mlp_proj(x2d, tm):
        return _mlp_proj(x2d, tm, row(norm_ffn[0]), w_up_b, w_down_b, row(kv_norm), w_kv_b, g_k,
                         row(norm_mix[1]), w_q0, g_q, avg)

    zeros = jnp.zeros((bsz, D_STATE), F32)
    x1, hr_p, hi_p, w_up_b, w_down_b, w_kv_b, w_q0, w_o0 = layer0(
        x_prompt, zeros, zeros, S5_TIME_CHUNK, False,
        cast=(w_up.reshape(-1, w_up.shape[-1]), w_down.reshape(-1, w_down.shape[-1]), w_kv, w_q[0], w_o[0]))
    w_up_b = w_up_b.reshape(w_up.shape)
    w_down_b = w_down_b.reshape(w_down.shape)
    x2, q, kd, vd, k, v = mlp_proj(x1.reshape(bsz * length, d), ROW_TILE)
    y_prompt = _prompt_attention(q, kd, vd, x2, jnp.asarray(_prompt_bias()), sinks, lmask, w_o0,
                                 row(norm_ffn[1]), w_up_b, w_down_b, bsz=bsz, length=length, tq=ROW_TILE)
    y_prompt = y_prompt.reshape(bsz, length, d)
    k_p = k.reshape(bsz, length, D_KV)[:, -WINDOW:].reshape(bsz, WINDOW, N_KV_HEADS, HEAD_DIM)
    v_p = v.reshape(bsz, length, D_KV)[:, -WINDOW:].reshape(bsz, WINDOW, N_KV_HEADS, HEAD_DIM)
    state_shape = (1, bsz, N_GROUPS, STATE_N)
    re_p = hr_p.reshape(state_shape).astype(state_ssm_re.dtype)
    im_p = hi_p.reshape(state_shape).astype(state_ssm_im.dtype)

    h0r_t = jnp.transpose(state_ssm_re[0], (1, 2, 0)).reshape(D_STATE, dec_b).astype(F32)
    h0i_t = jnp.transpose(state_ssm_im[0], (1, 2, 0)).reshape(D_STATE, dec_b).astype(F32)
    x1s, hr_s, hi_s = layer0(x_sample, h0r_t, h0i_t, dec_t, True)
    x2s, qs, _, _, ks, vs = mlp_proj(x1s.reshape(dec_b * dec_t, d), dec_b * dec_t)
    q_heads = qs.reshape(dec_b, dec_t, N_KV_HEADS, Q_PER_KV, HEAD_DIM).transpose(2, 0, 3, 1, 4)
    q_heads = q_heads.reshape(N_KV_HEADS, dec_b * Q_PER_KV * dec_t, HEAD_DIM)
    bias_c, bias_n = _sample_bias(dec_t)
    sink_rows = jnp.broadcast_to(
        attn_sinks[0].astype(F32).reshape(N_KV_HEADS, 1, Q_PER_KV, 1),
        (N_KV_HEADS, SAMPLE_GROUP, Q_PER_KV, dec_t)).reshape(N_KV_HEADS, -1, 1)
    o_heads, k_s_t, v_s_t = _sample_attention(
        q_heads, jnp.transpose(cache_k, (0, 2, 3, 1)), jnp.transpose(cache_v, (0, 2, 3, 1)),
        ks.astype(cache_k.dtype), vs.astype(cache_v.dtype), jnp.asarray(bias_c), jnp.asarray(bias_n),
        sink_rows, t_new=dec_t)
    o_s = o_heads.reshape(N_KV_HEADS, dec_b, Q_PER_KV, dec_t, HEAD_DIM).transpose(1, 3, 0, 2, 4)
    o_s = o_s.reshape(dec_b * dec_t, d).astype(BF16)
    y_sample = _out_mlp(o_s, x2s, w_o0, row(norm_ffn[1]), w_up_b, w_down_b, dec_b * dec_t)
    y_sample = y_sample.reshape(dec_b, dec_t, d)
    k_s = jnp.transpose(k_s_t, (0, 3, 1, 2))
    v_s = jnp.transpose(v_s_t, (0, 3, 1, 2))
    re_s = jnp.transpose(hr_s.reshape(N_GROUPS, STATE_N, dec_b), (2, 0, 1))[None].astype(state_ssm_re.dtype)
    im_s = jnp.transpose(hi_s.reshape(N_GROUPS, STATE_N, dec_b), (2, 0, 1))[None].astype(state_ssm_im.dtype)

    return (y_prompt, y_sample, re_p, im_p, k_p, v_p, re_s, im_s, k_s, v_s)
```
